```python
import math
import jax, jax.numpy as jnp
from jax import lax
import numpy as np

D_MODEL = 1024
BATCH = 4
SEQ = 8192
DEPTH = 2

DN_ALPHA = (2 * DEPTH) ** 0.25
DN_BETA = (8 * DEPTH) ** -0.25
LN_EPS = 1e-5

CONV_WIDTH = D_MODEL // 2
CONV_GROUPS = 8
CONV_TAPS = 3
LRU_WIDTH = D_MODEL // 2
LRU_HEADS = 8
LRU_HEAD_DIM = LRU_WIDTH // LRU_HEADS
LRU_CONV_TAPS = 4
LRU_C = 8.0
L0_IN_COLS = 3 * CONV_WIDTH + 2 * LRU_WIDTH

N_HEADS = 16
HEAD_DIM = 64
N_KV_GROUPS = 4
HEADS_PER_GROUP = N_HEADS // N_KV_GROUPS
KV_WIDTH = N_KV_GROUPS * HEAD_DIM
CMP_STRIDE = 16
CMP_BLOCK = 2 * CMP_STRIDE
CMP_HIDDEN = 256
SEL_BLOCK = 64
N_SEL = 16
WINDOW = 512
Q_BLOCK = 128
ROPE_THETA = 10000.0
L1_IN_COLS = N_HEADS * HEAD_DIM + 6 * KV_WIDTH + 3 * N_HEADS
FORCE = 1e4
NEG = -1e30
TINY = 1e-30

N_GROUPS = 4
EXPERTS_PER_GROUP = 8
N_EXPERTS = N_GROUPS * EXPERTS_PER_GROUP
TOP_K = 2
EXPERT_FF = 512
MOE_BLOCK = 256

kernel_name = "hybrid_conv_rglru_nsa_hiermoe_deepnorm"

F32 = jnp.float32


def _layernorm(x, g, b):
    xf = x.astype(F32)
    mu = jnp.mean(xf, -1, keepdims=True)
    var = jnp.mean(jnp.square(xf - mu), -1, keepdims=True)
    y = (xf - mu) * lax.rsqrt(var + LN_EPS)
    return (y * g.astype(F32) + b.astype(F32)).astype(x.dtype)


def _causal_dwconv(x, w):
    taps, S = w.shape[0], x.shape[1]
    xp = jnp.pad(x, ((0, 0), (taps - 1, 0), (0, 0)))
    y = xp[:, 0:S] * w[0]
    for k in range(1, taps):
        y = y + xp[:, k:k + S] * w[k]
    return y


def _rglru(xc, w_a, b_a, w_x, b_x, lam):
    Bn, S, C = xc.shape
    xh = xc.reshape(Bn, S, LRU_HEADS, LRU_HEAD_DIM)
    r = jax.nn.sigmoid((jnp.einsum('bshi,hij->bshj', xh, w_a).reshape(Bn, S, C) + b_a).astype(F32))
    i = jax.nn.sigmoid((jnp.einsum('bshi,hij->bshj', xh, w_x).reshape(Bn, S, C) + b_x).astype(F32))
    log_a = -LRU_C * r * jax.nn.softplus(-lam.astype(F32))
    a = jnp.exp(log_a)
    mult = jnp.sqrt(jnp.maximum(-jnp.expm1(2.0 * log_a), 0.0))
    u = mult * i * xc.astype(F32)

    def comb(left, right):
        a1, b1 = left
        a2, b2 = right
        return a1 * a2, a2 * b1 + b2

    _, h = lax.associative_scan(comb, (a, u), axis=1)
    return h.astype(xc.dtype)


def _even_mixer(x, w_in, conv_w, lru_conv_w, lru_conv_b, w_a, b_a, w_x, b_x, lam, w_out):
    z = x @ w_in
    W = CONV_WIDTH
    bg, cg, hc, gate, xl = jnp.split(z, [W, 2 * W, 3 * W, 3 * W + LRU_WIDTH], axis=-1)
    y_conv = bg * _causal_dwconv(cg * hc, conv_w)
    xc = _causal_dwconv(xl, lru_conv_w) + lru_conv_b
    y_lru = _rglru(xc, w_a, b_a, w_x, b_x, lam) * jax.nn.gelu(gate)
    return jnp.concatenate([y_conv, y_lru], axis=-1) @ w_out


def _rope(x, cos, sin):
    xf = x.astype(F32)
    x1, x2 = jnp.split(xf, 2, axis=-1)
    c = cos[None, :, None, :]
    s = sin[None, :, None, :]
    return jnp.concatenate([x1 * c - x2 * s, x2 * c + x1 * s], axis=-1).astype(x.dtype)


def _masked_softmax(s, mask):
    s = jnp.where(mask, s, NEG)
    s = s - jnp.max(s, axis=-1, keepdims=True)
    p = jnp.exp(s) * mask
    return p / jnp.maximum(jnp.sum(p, axis=-1, keepdims=True), TINY)


def _compress(k, pe, w1, w2):
    Bn, S, G, dk = k.shape
    ch = k.reshape(Bn, S // CMP_STRIDE, CMP_STRIDE, G, dk)
    blocks = jnp.concatenate([ch[:, :-1], ch[:, 1:]], axis=2)
    blocks = blocks + pe[None, None, :, None, :]
    flat = blocks.transpose(0, 1, 3, 2, 4).reshape(Bn, blocks.shape[1], G, CMP_BLOCK * dk)
    return jax.nn.gelu(flat @ w1) @ w2


def _nsa_mixer(x, w_in, cmp_k_pe, cmp_k_w1, cmp_k_w2, cmp_v_pe, cmp_v_w1, cmp_v_w2, w_out):
    Bn, S, _ = x.shape
    G, R, dk = N_KV_GROUPS, HEADS_PER_GROUP, HEAD_DIM
    z = x @ w_in
    qd = N_HEADS * HEAD_DIM
    cuts = [qd + j * KV_WIDTH for j in range(7)]
    q, kc, vc, ks, vs, kw, vw, gl = jnp.split(z, cuts, axis=-1)
    q = q.reshape(Bn, S, N_HEADS, dk)
    kc, vc, ks, vs, kw, vw = [t.reshape(Bn, S, G, dk) for t in (kc, vc, ks, vs, kw, vw)]

    pos = jnp.arange(S, dtype=F32)
    inv = ROPE_THETA ** (-jnp.arange(0, dk, 2, dtype=F32) / dk)
    ang = pos[:, None] * inv[None, :]
    cos, sin = jnp.cos(ang), jnp.sin(ang)
    q = _rope(q, cos, sin)
    kc = _rope(kc, cos, sin)
    ks = _rope(ks, cos, sin)
    kw = _rope(kw, cos, sin)
    gates = jax.nn.sigmoid(gl.astype(F32)).reshape(Bn, S, G, R, 3)

    kcmp = _compress(kc, cmp_k_pe, cmp_k_w1, cmp_k_w2)
    vcmp = _compress(vc, cmp_v_pe, cmp_v_w1, cmp_v_w2)
    n_cmp = kcmp.shape[1]
    n_slc = S // SEL_BLOCK
    n_top = min(N_SEL, n_slc)
    ksb = ks.reshape(Bn, n_slc, SEL_BLOCK, G, dk).transpose(0, 3, 1, 2, 4)
    vsb = vs.reshape(Bn, n_slc, SEL_BLOCK, G, dk).transpose(0, 3, 1, 2, 4)
    kwp = jnp.pad(kw, ((0, 0), (WINDOW, 0), (0, 0), (0, 0)))
    vwp = jnp.pad(vw, ((0, 0), (WINDOW, 0), (0, 0), (0, 0)))

    c_start = jnp.arange(n_cmp) * CMP_STRIDE
    j_idx = jnp.arange(n_slc)
    j_start = j_idx * SEL_BLOCK
    overlap = ((c_start[:, None] < j_start[None, :] + SEL_BLOCK)
               & (c_start[:, None] + CMP_BLOCK > j_start[None, :])).astype(F32)
    scale = HEAD_DIM ** -0.5
    bidx = jnp.arange(Bn)[:, None, None, None]
    gidx = jnp.arange(G)[None, :, None, None]

    def block(qi):
        qs = qi * Q_BLOCK
        t = qs + jnp.arange(Q_BLOCK)
        qb = lax.dynamic_slice_in_dim(q, qs, Q_BLOCK, 1).reshape(Bn, Q_BLOCK, G, R, dk)
        gb = lax.dynamic_slice_in_dim(gates, qs, Q_BLOCK, 1)
        s_c = jnp.einsum('bqgrd,bcgd->bgrqc', qb, kcmp).astype(F32) * scale
        m_c = (c_start + CMP_BLOCK - 1)[None, :] <= t[:, None]
        p_c = _masked_softmax(s_c, m_c)
        o_c = jnp.einsum('bgrqc,bcgd->bqgrd', p_c.astype(x.dtype), vcmp)
        imp = jnp.einsum('bgrqc,cj->bgqj', p_c, overlap)
        bt = t // SEL_BLOCK
        forced = ((j_idx[None, :] == 0) | (j_idx[None, :] == bt[:, None])
                  | (j_idx[None, :] == bt[:, None] - 1))
        future = j_idx[None, :] > bt[:, None]
        imp = jnp.where(forced, FORCE, jnp.where(future, -FORCE, imp))
        _, sel = lax.top_k(imp, n_top)
        ksel = ksb[bidx, gidx, sel]
        vsel = vsb[bidx, gidx, sel]
        s_s = jnp.einsum('bqgrd,bgqnkd->bgrqnk', qb, ksel).astype(F32) * scale
        kpos = sel[..., None] * SEL_BLOCK + jnp.arange(SEL_BLOCK)
        m_s = (kpos <= t[None, None, :, None, None]).reshape(Bn, G, 1, Q_BLOCK, n_top * SEL_BLOCK)
        p_s = _masked_softmax(s_s.reshape(Bn, G, R, Q_BLOCK, n_top * SEL_BLOCK), m_s)
        p_s = p_s.reshape(Bn, G, R, Q_BLOCK, n_top, SEL_BLOCK).astype(x.dtype)
        o_s = jnp.einsum('bgrqnk,bgqnkd->bqgrd', p_s, vsel)
        kwb = lax.dynamic_slice_in_dim(kwp, qs, WINDOW + Q_BLOCK, 1)
        vwb = lax.dynamic_slice_in_dim(vwp, qs, WINDOW + Q_BLOCK, 1)
        wpos = qs - WINDOW + jnp.arange(WINDOW + Q_BLOCK)
        m_w = ((wpos[None, :] <= t[:, None]) & (wpos[None, :] > t[:, None] - WINDOW)
               & (wpos[None, :] >= 0))
        s_w = jnp.einsum('bqgrd,bkgd->bgrqk', qb, kwb).astype(F32) * scale
        p_w = _masked_softmax(s_w, m_w).astype(x.dtype)
        o_w = jnp.einsum('bgrqk,bkgd->bqgrd', p_w, vwb)
        o = gb[..., 0:1] * o_c + gb[..., 1:2] * o_s + gb[..., 2:3] * o_w
        return o.astype(x.dtype).reshape(Bn, Q_BLOCK, N_HEADS * dk)

    out = lax.map(block, jnp.arange(S // Q_BLOCK))
    out = out.transpose(1, 0, 2, 3).reshape(Bn, S, N_HEADS * dk)
    return out @ w_out


def _hier_moe(x, w_rg, b_rg, w_re, b_re, w_gate, w_up, w_down):
    Bn, S, D = x.shape
    T = Bn * S
    xf = x.reshape(T, D)
    tok_ar = jnp.arange(T)
    lg = (xf @ w_rg).astype(F32) + b_rg.astype(F32)
    pg = jax.nn.softmax(lg, axis=-1)
    g_star = jnp.argmax(lg, axis=-1)
    p_grp = pg[tok_ar, g_star][:, None]
    le = ((xf @ w_re).astype(F32) + b_re.astype(F32)).reshape(T, N_GROUPS, EXPERTS_PER_GROUP)
    pe = jax.nn.softmax(le[tok_ar, g_star], axis=-1)
    v, idx = lax.top_k(pe, TOP_K)
    w = p_grp * v / jnp.sum(v, axis=-1, keepdims=True)
    e = g_star[:, None] * EXPERTS_PER_GROUP + idx

    A = T * TOP_K
    e_flat = e.reshape(A)
    w_flat = w.reshape(A)
    tok = jnp.repeat(tok_ar, TOP_K)
    order = jnp.argsort(e_flat)
    e_s, tok_s, w_s = e_flat[order], tok[order], w_flat[order]
    counts = jnp.bincount(e_flat, length=N_EXPERTS)
    starts = jnp.cumsum(counts) - counts
    padded = (counts + MOE_BLOCK - 1) // MOE_BLOCK * MOE_BLOCK
    pad_end = jnp.cumsum(padded)
    pad_start = pad_end - padded
    dest = pad_start[e_s] + jnp.arange(A) - starts[e_s]
    n_blocks = -(-A // MOE_BLOCK) + N_EXPERTS
    P = n_blocks * MOE_BLOCK
    xs = jnp.zeros((P, D), x.dtype).at[dest].set(xf[tok_s])
    block_e = jnp.minimum(jnp.searchsorted(pad_end, jnp.arange(n_blocks) * MOE_BLOCK, side='right'),
                          N_EXPERTS - 1)

    def expert_block(args):
        xb, eb = args
        h = jax.nn.silu(xb @ w_gate[eb]) * (xb @ w_up[eb])
        return h @ w_down[eb]

    ys = lax.map(expert_block, (xs.reshape(n_blocks, MOE_BLOCK, D), block_e)).reshape(P, D)
    y = jnp.zeros((T, D), x.dtype).at[tok_s].add(ys[dest] * w_s[:, None].astype(x.dtype))
    return y.reshape(Bn, S, D)


def _moe_params(key, prefix):
    k = jax.random.split(key, 7)
    D = D_MODEL
    return {
        prefix + "router_group_w": jax.random.normal(k[0], (D, N_GROUPS), F32) * D ** -0.5,
        prefix + "router_group_b": jax.random.normal(k[1], (N_GROUPS,), F32) * 0.01,
        prefix + "router_expert_w": jax.random.normal(k[2], (D, N_EXPERTS), F32) * D ** -0.5,
        prefix + "router_expert_b": jax.random.normal(k[3], (N_EXPERTS,), F32) * 0.01,
        prefix + "expert_w_gate": jax.random.normal(k[4], (N_EXPERTS, D, EXPERT_FF), F32) * D ** -0.5,
        prefix + "expert_w_up": jax.random.normal(k[5], (N_EXPERTS, D, EXPERT_FF), F32) * D ** -0.5,
        prefix + "expert_w_down": jax.random.normal(k[6], (N_EXPERTS, EXPERT_FF, D), F32)
        * (EXPERT_FF ** -0.5 * DN_BETA),
    }


def _ln_params(key, prefix):
    k1, k2 = jax.random.split(key)
    return {prefix + "_g": 1.0 + 0.02 * jax.random.normal(k1, (D_MODEL,), F32),
            prefix + "_b": 0.02 * jax.random.normal(k2, (D_MODEL,), F32)}


def setup_inputs(seed: int = 0) -> dict:
    key = jax.random.key(seed)
    ks = jax.random.split(key, 32)
    D = D_MODEL
    p = {}
    p["x"] = jax.random.normal(ks[0], (BATCH, SEQ, D), F32)
    p["l0_w_in"] = jax.random.normal(ks[1], (D, L0_IN_COLS), F32) * D ** -0.5
    p["l0_conv_w"] = jax.random.normal(ks[2], (CONV_TAPS, CONV_WIDTH), F32) * CONV_TAPS ** -0.5
    p["l0_lru_conv_w"] = jax.random.normal(ks[3], (LRU_CONV_TAPS, LRU_WIDTH), F32) * LRU_CONV_TAPS ** -0.5
    p["l0_lru_conv_b"] = jax.random.normal(ks[4], (LRU_WIDTH,), F32) * 0.01
    p["l0_lru_w_a"] = jax.random.normal(ks[5], (LRU_HEADS, LRU_HEAD_DIM, LRU_HEAD_DIM), F32) * LRU_HEAD_DIM ** -0.5
    p["l0_lru_b_a"] = jax.random.normal(ks[6], (LRU_WIDTH,), F32) * 0.01
    p["l0_lru_w_x"] = jax.random.normal(ks[7], (LRU_HEADS, LRU_HEAD_DIM, LRU_HEAD_DIM), F32) * LRU_HEAD_DIM ** -0.5
    p["l0_lru_b_x"] = jax.random.normal(ks[8], (LRU_WIDTH,), F32) * 0.01
    u = jax.random.uniform(ks[9], (LRU_WIDTH,), F32, minval=0.9, maxval=0.999)
    a0 = u ** (1.0 / LRU_C)
    p["l0_lru_lambda"] = jnp.log(a0) - jnp.log1p(-a0)
    p["l0_w_out"] = jax.random.normal(ks[10], (CONV_WIDTH + LRU_WIDTH, D), F32) * (D ** -0.5 * DN_BETA)
    p.update(_ln_params(ks[11], "l0_ln1"))
    p.update(_moe_params(ks[12], "l0_"))
    p.update(_ln_params(ks[13], "l0_ln2"))
    ones_kv = jnp.ones((KV_WIDTH,), F32)
    beta_kv = jnp.full((KV_WIDTH,), DN_BETA, F32)
    col_scale = jnp.concatenate([jnp.ones((N_HEADS * HEAD_DIM,), F32), ones_kv, beta_kv, ones_kv, beta_kv,
                                 ones_kv, beta_kv, jnp.ones((3 * N_HEADS,), F32)])
    p["l1_w_in"] = jax.random.normal(ks[14], (D, L1_IN_COLS), F32) * (D ** -0.5) * col_scale[None, :]
    p["l1_cmp_k_pe"] = jax.random.normal(ks[15], (CMP_BLOCK, HEAD_DIM), F32) * 0.02
    p["l1_cmp_k_w1"] = jax.random.normal(ks[16], (CMP_BLOCK * HEAD_DIM, CMP_HIDDEN), F32) * (CMP_BLOCK * HEAD_DIM) ** -0.5
    p["l1_cmp_k_w2"] = jax.random.normal(ks[17], (CMP_HIDDEN, HEAD_DIM), F32) * CMP_HIDDEN ** -0.5
    p["l1_cmp_v_pe"] = jax.random.normal(ks[18], (CMP_BLOCK, HEAD_DIM), F32) * 0.02
    p["l1_cmp_v_w1"] = jax.random.normal(ks[19], (CMP_BLOCK * HEAD_DIM, CMP_HIDDEN), F32) * (CMP_BLOCK * HEAD_DIM) ** -0.5
    p["l1_cmp_v_w2"] = jax.random.normal(ks[20], (CMP_HIDDEN, HEAD_DIM), F32) * CMP_HIDDEN ** -0.5
    p["l1_w_out"] = jax.random.normal(ks[21], (N_HEADS * HEAD_DIM, D), F32) * (D ** -0.5 * DN_BETA)
    p.update(_ln_params(ks[22], "l1_ln1"))
    p.update(_moe_params(ks[23], "l1_"))
    p.update(_ln_params(ks[24], "l1_ln2"))
    return p


def reference(x,
              l0_w_in, l0_conv_w, l0_lru_conv_w, l0_lru_conv_b, l0_lru_w_a, l0_lru_b_a, l0_lru_w_x,
              l0_lru_b_x, l0_lru_lambda, l0_w_out, l0_ln1_g, l0_ln1_b,
              l0_router_group_w, l0_router_group_b, l0_router_expert_w, l0_router_expert_b,
              l0_expert_w_gate, l0_expert_w_up, l0_expert_w_down, l0_ln2_g, l0_ln2_b,
              l1_w_in, l1_cmp_k_pe, l1_cmp_k_w1, l1_cmp_k_w2, l1_cmp_v_pe, l1_cmp_v_w1, l1_cmp_v_w2,
              l1_w_out, l1_ln1_g, l1_ln1_b,
              l1_router_group_w, l1_router_group_b, l1_router_expert_w, l1_router_expert_b,
              l1_expert_w_gate, l1_expert_w_up, l1_expert_w_down, l1_ln2_g, l1_ln2_b):
    mixers = (
        lambda h: _even_mixer(h, l0_w_in, l0_conv_w, l0_lru_conv_w, l0_lru_conv_b, l0_lru_w_a, l0_lru_b_a,
                              l0_lru_w_x, l0_lru_b_x, l0_lru_lambda, l0_w_out),
        lambda h: _nsa_mixer(h, l1_w_in, l1_cmp_k_pe, l1_cmp_k_w1, l1_cmp_k_w2, l1_cmp_v_pe, l1_cmp_v_w1,
                             l1_cmp_v_w2, l1_w_out),
    )
    norms1 = ((l0_ln1_g, l0_ln1_b), (l1_ln1_g, l1_ln1_b))
    norms2 = ((l0_ln2_g, l0_ln2_b), (l1_ln2_g, l1_ln2_b))
    moes = ((l0_router_group_w, l0_router_group_b, l0_router_expert_w, l0_router_expert_b,
             l0_expert_w_gate, l0_expert_w_up, l0_expert_w_down),
            (l1_router_group_w, l1_router_group_b, l1_router_expert_w, l1_router_expert_b,
             l1_expert_w_gate, l1_expert_w_up, l1_expert_w_down))
    for layer in range(DEPTH):
        x = _layernorm(DN_ALPHA * x + mixers[layer](x), *norms1[layer])
        x = _layernorm(DN_ALPHA * x + _hier_moe(x, *moes[layer]), *norms2[layer])
    return x
```

```python
import functools

import jax
import jax.numpy as jnp
from jax import lax
from jax.experimental import pallas as pl
from jax.experimental.pallas import tpu as pltpu

F32 = jnp.float32
BF16 = jnp.bfloat16
I32 = jnp.int32

DEPTH = 2
DN_ALPHA = (2 * DEPTH) ** 0.25
LN_EPS = 1e-5
CONV_WIDTH = 512
CONV_TAPS = 3
LRU_WIDTH = 512
LRU_CONV_TAPS = 4
LRU_C = 8.0
N_HEADS = 16
HEAD_DIM = 64
N_KV_GROUPS = 4
HEADS_PER_GROUP = N_HEADS // N_KV_GROUPS
KV_WIDTH = N_KV_GROUPS * HEAD_DIM
CMP_STRIDE = 16
CMP_BLOCK = 32
SEL_BLOCK = 64
N_SEL = 16
WINDOW = 512
ROPE_THETA = 10000.0
FORCE = 1e4
NEG = -1e30
TINY = 1e-30
N_GROUPS = 4
EXPERTS_PER_GROUP = 8
N_EXPERTS = N_GROUPS * EXPERTS_PER_GROUP
TOP_K = 2
MOE_BLOCK = 256

LANES = 128
VMEM_LIMIT_BYTES = 48 * 1024 * 1024
ROW_TILE = 512
SCAN_CHUNK = 256
ROUTE_TILE = 512
GATHER_TILE = 256
Q_TILE = 128
SEL_KEY_TILE = 512
WIN_KEY_TILE = 128


def _params(*sem):
    return pltpu.CompilerParams(dimension_semantics=sem, vmem_limit_bytes=VMEM_LIMIT_BYTES)


def _layernorm(v, g, b):
    mu = jnp.mean(v, axis=-1, keepdims=True)
    d = v - mu
    var = jnp.mean(d * d, axis=-1, keepdims=True)
    return d * lax.rsqrt(var + LN_EPS) * g + b


def _gelu_tanh(x):
    return 0.5 * x * (1.0 + jnp.tanh(0.7978845608028654 * (x + 0.044715 * (x * x * x))))


def _mm_kernel(x_ref, w_ref, *rest, rope):
    o_ref = rest[-1]
    acc = jnp.dot(x_ref[...].astype(BF16), w_ref[...], preferred_element_type=F32)
    if rope:
        cos_ref, sin_ref = rest[0], rest[1]
        tn = acc.shape[1]
        lane = lax.broadcasted_iota(I32, acc.shape, 1)
        first_half = (lane & (HEAD_DIM // 2)) == 0
        partner = jnp.where(first_half, pltpu.roll(acc, tn - HEAD_DIM // 2, 1),
                            pltpu.roll(acc, HEAD_DIM // 2, 1))
        acc = acc * cos_ref[...] + partner * sin_ref[...]
    o_ref[...] = acc.astype(o_ref.dtype)


def _matmul(x, w, out_dtype, tn, rope_tabs=None, seq=None):
    m, k = x.shape
    n = w.shape[1]
    tm = min(ROW_TILE, m)
    grid = (n // tn, m // tm)
    in_specs = [pl.BlockSpec((tm, k), lambda j, i: (i, 0)),
                pl.BlockSpec((k, tn), lambda j, i: (0, j))]
    args = [x, w]
    if rope_tabs is not None:
        ns = seq // tm
        in_specs += [pl.BlockSpec((tm, tn), lambda j, i: (i % ns, 0))] * 2
        args += list(rope_tabs)
    return pl.pallas_call(
        functools.partial(_mm_kernel, rope=rope_tabs is not None),
        grid=grid, in_specs=in_specs,
        out_specs=pl.BlockSpec((tm, tn), lambda j, i: (i, j)),
        out_shape=jax.ShapeDtypeStruct((m, n), out_dtype),
        compiler_params=_params("arbitrary", "arbitrary"),
    )(*args)


def _mm_res_ln_kernel(y_ref, w_ref, x_ref, g_ref, b_ref, o_ref):
    acc = jnp.dot(y_ref[...].astype(BF16), w_ref[...], preferred_element_type=F32)
    o_ref[...] = _layernorm(DN_ALPHA * x_ref[...] + acc, g_ref[...], b_ref[...])


def _matmul_residual_ln(y, w, x, g, b):
    m, k = y.shape
    d = w.shape[1]
    tm = min(ROW_TILE, m)
    return pl.pallas_call(
        _mm_res_ln_kernel,
        grid=(m // tm,),
        in_specs=[pl.BlockSpec((tm, k), lambda i: (i, 0)),
                  pl.BlockSpec((k, d), lambda i: (0, 0)),
                  pl.BlockSpec((tm, d), lambda i: (i, 0)),
                  pl.BlockSpec((1, d), lambda i: (0, 0)),
                  pl.BlockSpec((1, d), lambda i: (0, 0))],
        out_specs=pl.BlockSpec((tm, d), lambda i: (i, 0)),
        out_shape=jax.ShapeDtypeStruct((m, d), F32),
        compiler_params=_params("arbitrary"),
    )(y, w, x, g.reshape(1, d), b.reshape(1, d))


def _shift_rows(cur, tail, d, row8):
    rolled = pltpu.roll(cur, d, 0)
    head = jnp.where(row8 < d, pltpu.roll(tail, d, 0), rolled[:8])
    return jnp.concatenate([head, rolled[8:]], axis=0)


def _l0_mixer_kernel(z_ref, cw_ref, lw_ref, lb_ref, wa_ref, ba_ref, wx_ref, bx_ref, lam_ref,
                     o_ref, tail_c, tail_x, h_state):
    W = CONV_WIDTH
    tc = z_ref.shape[0]

    @pl.when(pl.program_id(1) == 0)
    def _():
        tail_c[...] = jnp.zeros_like(tail_c)
        tail_x[...] = jnp.zeros_like(tail_x)
        h_state[...] = jnp.zeros_like(h_state)

    row8 = lax.broadcasted_iota(I32, (8, W), 0)
    row = lax.broadcasted_iota(I32, (tc, W), 0)

    ch = z_ref[:, W:2 * W] * z_ref[:, 2 * W:3 * W]
    tc_prev = tail_c[...]
    conv = ch * cw_ref[CONV_TAPS - 1:CONV_TAPS, :]
    for d in range(1, CONV_TAPS):
        conv = conv + _shift_rows(ch, tc_prev, d, row8) * cw_ref[CONV_TAPS - 1 - d:CONV_TAPS - d, :]
    tail_c[...] = ch[tc - 8:, :]
    o_ref[:, :W] = (z_ref[:, :W] * conv).astype(o_ref.dtype)

    xl = z_ref[:, 4 * W:5 * W]
    tx_prev = tail_x[...]
    xc = xl * lw_ref[LRU_CONV_TAPS - 1:LRU_CONV_TAPS, :]
    for d in range(1, LRU_CONV_TAPS):
        xc = xc + _shift_rows(xl, tx_prev, d, row8) * lw_ref[LRU_CONV_TAPS - 1 - d:LRU_CONV_TAPS - d, :]
    xc = xc + lb_ref[...]
    tail_x[...] = xl[tc - 8:, :]

    xcb = xc.astype(BF16)
    r = jax.nn.sigmoid(jnp.dot(xcb, wa_ref[...], preferred_element_type=F32) + ba_ref[...])
    ig = jax.nn.sigmoid(jnp.dot(xcb, wx_ref[...], preferred_element_type=F32) + bx_ref[...])
    nl = -lam_ref[...]
    softplus = jnp.maximum(nl, 0.0) + jnp.log1p(jnp.exp(-jnp.abs(nl)))
    log_a = -LRU_C * r * softplus
    a = jnp.exp(log_a)
    mult = jnp.sqrt(jnp.maximum(1.0 - a * a, 0.0))
    u = mult * ig * xc

    d = 1
    while d < tc:
        keep = row >= d
        a_s = jnp.where(keep, pltpu.roll(a, d, 0), 1.0)
        u_s = jnp.where(keep, pltpu.roll(u, d, 0), 0.0)
        u = a * u_s + u
        a = a * a_s
        d *= 2
    h = a * h_state[0:1, :] + u
    h_state[...] = jnp.broadcast_to(h[tc - 1:tc, :], h_state.shape)
    o_ref[:, W:] = (h * _gelu_tanh(z_ref[:, 3 * W:4 * W])).astype(o_ref.dtype)


def _l0_mixer(z, bsz, seq, conv_w, lru_conv_w, lru_conv_b, wa_bd, b_a, wx_bd, b_x, lam):
    W = CONV_WIDTH
    tc = min(SCAN_CHUNK, seq)
    nchunk = seq // tc
    vec = lambda i, j: (0, 0)
    return pl.pallas_call(
        _l0_mixer_kernel,
        grid=(bsz, nchunk),
        in_specs=[pl.BlockSpec((tc, 5 * W), lambda i, j: (i * nchunk + j, 0)),
                  pl.BlockSpec((CONV_TAPS, W), vec),
                  pl.BlockSpec((LRU_CONV_TAPS, W), vec),
                  pl.BlockSpec((1, W), vec),
                  pl.BlockSpec((W, W), vec),
                  pl.BlockSpec((1, W), vec),
                  pl.BlockSpec((W, W), vec),
                  pl.BlockSpec((1, W), vec),
                  pl.BlockSpec((1, W), vec)],
        out_specs=pl.BlockSpec((tc, 2 * W), lambda i, j: (i * nchunk + j, 0)),
        out_shape=jax.ShapeDtypeStruct((bsz * seq, 2 * W), BF16),
        scratch_shapes=[pltpu.VMEM((8, W), F32), pltpu.VMEM((8, W), F32), pltpu.VMEM((8, W), F32)],
        compiler_params=_params("arbitrary", "arbitrary"),
    )(z, conv_w, lru_conv_w, lru_conv_b.reshape(1, W), wa_bd, b_a.reshape(1, W), wx_bd,
      b_x.reshape(1, W), lam.reshape(1, W))


def _split_bf16(v):
    hi = v.astype(BF16)
    lo = (v - hi.astype(F32)).astype(BF16)
    return hi, lo


def _route_kernel(x_ref, wh_ref, wl_ref, b_ref, info_ref, cnt_ref):
    tm = x_ref.shape[0]

    @pl.when(pl.program_id(0) == 0)
    def _():
        cnt_ref[...] = jnp.zeros_like(cnt_ref)

    xh, xl = _split_bf16(x_ref[...])
    logits = (jnp.dot(xh, wh_ref[...], preferred_element_type=F32)
              + jnp.dot(xl, wh_ref[...], preferred_element_type=F32)
              + jnp.dot(xh, wl_ref[...], preferred_element_type=F32)) + b_ref[...]
    lane = lax.broadcasted_iota(I32, logits.shape, 1).astype(F32)
    ninf = -jnp.inf

    is_g = lane < N_GROUPS
    gmax = jnp.max(jnp.where(is_g, logits, ninf), axis=-1, keepdims=True)
    g_star = jnp.min(jnp.where(is_g & (logits == gmax), lane, float(LANES)), axis=-1, keepdims=True)
    gsum = jnp.sum(jnp.where(is_g, jnp.exp(logits - gmax), 0.0), axis=-1, keepdims=True)
    p_grp = 1.0 / gsum

    lo_lane = N_GROUPS + EXPERTS_PER_GROUP * g_star
    is_e = (lane >= lo_lane) & (lane < lo_lane + EXPERTS_PER_GROUP)
    emax = jnp.max(jnp.where(is_e, logits, ninf), axis=-1, keepdims=True)
    ex = jnp.where(is_e, jnp.exp(logits - emax), 0.0)
    pe = ex / jnp.sum(ex, axis=-1, keepdims=True)
    pe_m = jnp.where(is_e, pe, ninf)
    v1 = jnp.max(pe_m, axis=-1, keepdims=True)
    l1 = jnp.min(jnp.where(pe_m == v1, lane, float(LANES)), axis=-1, keepdims=True)
    pe_m2 = jnp.where(lane == l1, ninf, pe_m)
    v2 = jnp.max(pe_m2, axis=-1, keepdims=True)
    l2 = jnp.min(jnp.where(pe_m2 == v2, lane, float(LANES)), axis=-1, keepdims=True)
    vs = v1 + v2
    w1 = p_grp * v1 / vs
    w2 = p_grp * v2 / vs

    hit1 = lane == l1
    hit2 = lane == l2
    onehot = jnp.where(hit1 | hit2, 1.0, 0.0)
    r_i = lax.broadcasted_iota(I32, (tm, tm), 0)
    c_i = lax.broadcasted_iota(I32, (tm, tm), 1)
    tri = jnp.where(c_i < r_i, 1.0, 0.0).astype(BF16)
    before = jnp.dot(tri, onehot.astype(BF16), preferred_element_type=F32) + cnt_ref[0:1, :]
    rank1 = jnp.sum(jnp.where(hit1, before, 0.0), axis=-1, keepdims=True)
    rank2 = jnp.sum(jnp.where(hit2, before, 0.0), axis=-1, keepdims=True)
    cnt_ref[...] = cnt_ref[...] + jnp.sum(onehot, axis=0, keepdims=True)

    e1 = l1 - N_GROUPS
    e2 = l2 - N_GROUPS
    info = jnp.where(lane == 0, e1, jnp.where(lane == 1, e2, jnp.where(lane == 2, w1, jnp.where(
        lane == 3, w2, jnp.where(lane == 4, rank1, jnp.where(lane == 5, rank2, 0.0))))))
    info_ref[...] = info


def _moe_route(x, w_rg, b_rg, w_re, b_re):
    t, d = x.shape
    tm = min(ROUTE_TILE, t)
    npad = LANES - N_GROUPS - N_EXPERTS
    w = jnp.concatenate([w_rg, w_re, jnp.zeros((d, npad), F32)], axis=1)
    wh = w.astype(BF16)
    wl = (w - wh.astype(F32)).astype(BF16)
    b = jnp.concatenate([b_rg, b_re, jnp.zeros((npad,), F32)]).reshape(1, LANES)
    return pl.pallas_call(
        _route_kernel,
        grid=(t // tm,),
        in_specs=[pl.BlockSpec((tm, d), lambda i: (i, 0)),
                  pl.BlockSpec((d, LANES), lambda i: (0, 0)),
                  pl.BlockSpec((d, LANES), lambda i: (0, 0)),
                  pl.BlockSpec((1, LANES), lambda i: (0, 0))],
        out_specs=[pl.BlockSpec((tm, LANES), lambda i: (i, 0)),
                   pl.BlockSpec((8, LANES), lambda i: (0, 0))],
        out_shape=[jax.ShapeDtypeStruct((t, LANES), F32), jax.ShapeDtypeStruct((8, LANES), F32)],
        compiler_params=_params("arbitrary"),
    )(x, wh, wl, b)


def _row_copy(src_ref, src_row, dst_ref, dst_row, sem):
    return pltpu.make_async_copy(src_ref.at[pl.ds(src_row, 1), :], dst_ref.at[pl.ds(dst_row, 1), :], sem)


def _dispatch_kernel(dest_ref, x_ref, xs_in_ref, xs_ref, sem):
    del xs_in_ref
    tt = x_ref.shape[0]

    def issue(i, c):
        for k in range(TOP_K):
            _row_copy(x_ref, i, xs_ref, dest_ref[0, 0, TOP_K * i + k], sem).start()
        return c

    lax.fori_loop(0, tt, issue, 0)

    def drain(i, c):
        _row_copy(x_ref, 0, xs_ref, 0, sem).wait()
        return c

    lax.fori_loop(0, TOP_K * tt, drain, 0)


def _moe_dispatch(x, dest3, n_rows):
    t, d = x.shape
    tt = dest3.shape[2] // TOP_K
    xs0 = jnp.zeros((n_rows, d), F32)
    return pl.pallas_call(
        _dispatch_kernel,
        grid=(t // tt,),
        in_specs=[pl.BlockSpec((1, 1, TOP_K * tt), lambda i: (i, 0, 0), memory_space=pltpu.SMEM),
                  pl.BlockSpec((tt, d), lambda i: (i, 0)),
                  pl.BlockSpec(memory_space=pl.ANY)],
        out_specs=pl.BlockSpec(memory_space=pl.ANY),
        out_shape=jax.ShapeDtypeStruct((n_rows, d), F32),
        scratch_shapes=[pltpu.SemaphoreType.DMA(())],
        input_output_aliases={2: 0},
        compiler_params=_params("arbitrary"),
    )(dest3, x, xs0)


def _expert_kernel(be_ref, xs_ref, wg_ref, wu_ref, wd_ref, ys_ref):
    del be_ref
    xb = xs_ref[...].astype(BF16)
    gate = jnp.dot(xb, wg_ref[0], preferred_element_type=F32)
    up = jnp.dot(xb, wu_ref[0], preferred_element_type=F32)
    h = gate * jax.nn.sigmoid(gate) * up
    ys_ref[...] = jnp.dot(h.astype(BF16), wd_ref[0], preferred_element_type=F32)


def _moe_experts(xs, block_e, wg, wu, wd):
    p, d = xs.shape
    ff = wg.shape[2]
    nb = p // MOE_BLOCK
    return pl.pallas_call(
        _expert_kernel,
        grid_spec=pltpu.PrefetchScalarGridSpec(
            num_scalar_prefetch=1, grid=(nb,),
            in_specs=[pl.BlockSpec((MOE_BLOCK, d), lambda i, be: (i, 0)),
                      pl.BlockSpec((1, d, ff), lambda i, be: (be[i], 0, 0)),
                      pl.BlockSpec((1, d, ff), lambda i, be: (be[i], 0, 0)),
                      pl.BlockSpec((1, ff, d), lambda i, be: (be[i], 0, 0))],
            out_specs=pl.BlockSpec((MOE_BLOCK, d), lambda i, be: (i, 0))),
        out_shape=jax.ShapeDtypeStruct((p, d), F32),
        compiler_params=_params("arbitrary"),
    )(block_e, xs, wg, wu, wd)


def _combine_ln_kernel(dest_ref, x_ref, info_ref, g_ref, b_ref, ys_ref, o_ref, buf, sem):
    tt = x_ref.shape[0]

    def issue(i, c):
        for k in range(TOP_K):
            _row_copy(ys_ref, dest_ref[0, 0, TOP_K * i + k], buf, k * tt + i, sem).start()
        return c

    lax.fori_loop(0, tt, issue, 0)

    def drain(i, c):
        _row_copy(ys_ref, 0, buf, 0, sem).wait()
        return c

    lax.fori_loop(0, TOP_K * tt, drain, 0)
    y = info_ref[:, 2:3] * buf[0:tt, :] + info_ref[:, 3:4] * buf[tt:2 * tt, :]
    o_ref[...] = _layernorm(DN_ALPHA * x_ref[...] + y, g_ref[...], b_ref[...])


def _moe_combine_ln(x, info, dest3, ys, g, b):
    t, d = x.shape
    tt = dest3.shape[2] // TOP_K
    return pl.pallas_call(
        _combine_ln_kernel,
        grid=(t // tt,),
        in_specs=[pl.BlockSpec((1, 1, TOP_K * tt), lambda i: (i, 0, 0), memory_space=pltpu.SMEM),
                  pl.BlockSpec((tt, d), lambda i: (i, 0)),
                  pl.BlockSpec((tt, LANES), lambda i: (i, 0)),
                  pl.BlockSpec((1, d), lambda i: (0, 0)),
                  pl.BlockSpec((1, d), lambda i: (0, 0)),
                  pl.BlockSpec(memory_space=pl.ANY)],
        out_specs=pl.BlockSpec((tt, d), lambda i: (i, 0)),
        out_shape=jax.ShapeDtypeStruct((t, d), F32),
        scratch_shapes=[pltpu.VMEM((TOP_K * tt, d), F32), pltpu.SemaphoreType.DMA(())],
        compiler_params=_params("arbitrary"),
    )(dest3, x, info, g.reshape(1, d), b.reshape(1, d), ys)


def _hier_moe_ln(x, w_rg, b_rg, w_re, b_re, w_gate, w_up, w_down, ln_g, ln_b):
    t, d = x.shape
    a_total = t * TOP_K
    info, cnt = _moe_route(x, w_rg, b_rg, w_re, b_re)
    counts = cnt[0, N_GROUPS:N_GROUPS + N_EXPERTS].astype(I32)
    padded = (counts + MOE_BLOCK - 1) // MOE_BLOCK * MOE_BLOCK
    pad_end = jnp.cumsum(padded)
    pad_start = pad_end - padded
    n_blocks = -(-a_total // MOE_BLOCK) + N_EXPERTS
    e = info[:, 0:TOP_K].astype(I32)
    rank = info[:, 4:4 + TOP_K].astype(I32)
    dest = pad_start[e] + rank
    block_e = jnp.minimum(
        jnp.searchsorted(pad_end, jnp.arange(n_blocks, dtype=I32) * MOE_BLOCK, side='right'),
        N_EXPERTS - 1).astype(I32)
    tt = min(GATHER_TILE, t)
    dest3 = dest.reshape(t // tt, 1, TOP_K * tt)
    xs = _moe_dispatch(x, dest3, n_blocks * MOE_BLOCK)
    ys = _moe_experts(xs, block_e, w_gate.astype(BF16), w_up.astype(BF16), w_down.astype(BF16))
    return _moe_combine_ln(x, info, dest3, ys, ln_g, ln_b)


def _compress_kernel(ch_ref, pe_ref, w1_ref, w2_ref, o_ref):
    half = CMP_STRIDE * HEAD_DIM
    c = ch_ref[0, 0]
    n = c.shape[0]
    first = jnp.dot((c + pe_ref[0:1, :]).astype(BF16), w1_ref[:half, :], preferred_element_type=F32)
    second = jnp.dot((c + pe_ref[1:2, :]).astype(BF16), w1_ref[half:, :], preferred_element_type=F32)
    h = first + pltpu.roll(second, n - 1, 0)
    o_ref[0, 0] = jnp.dot(_gelu_tanh(h).astype(BF16), w2_ref[...],
                          preferred_element_type=F32).astype(o_ref.dtype)


def _compress(ch, pe, w1, w2):
    bsz, g, n, half = ch.shape
    hid = w1.shape[1]
    return pl.pallas_call(
        _compress_kernel,
        grid=(bsz, g),
        in_specs=[pl.BlockSpec((1, 1, n, half), lambda b, j: (b, j, 0, 0)),
                  pl.BlockSpec((2, half), lambda b, j: (0, 0)),
                  pl.BlockSpec((2 * half, hid), lambda b, j: (0, 0)),
                  pl.BlockSpec((hid, HEAD_DIM), lambda b, j: (0, 0))],
        out_specs=pl.BlockSpec((1, 1, n, HEAD_DIM), lambda b, j: (b, j, 0, 0)),
        out_shape=jax.ShapeDtypeStruct((bsz, g, n, HEAD_DIM), BF16),
        compiler_params=_params("arbitrary", "arbitrary"),
    )(ch, pe.reshape(2, half), w1.astype(BF16), w2.astype(BF16))


def _nt_dot(a, b):
    return lax.dot_general(a, b, (((1,), (1,)), ((), ())), preferred_element_type=F32)


def _online_update(carry, s3, valid, v_tile):
    m, l, acc = carry
    r, q, k = s3.shape
    s3 = jnp.where(valid[None], s3, NEG)
    m_new = jnp.maximum(m, jnp.max(s3, axis=-1, keepdims=True))
    alpha = jnp.exp(m - m_new)
    p = jnp.where(valid[None], jnp.exp(s3 - m_new), 0.0)
    l = alpha * l + jnp.sum(p, axis=-1, keepdims=True)
    pv = jnp.dot(p.reshape(r * q, k).astype(BF16), v_tile, preferred_element_type=F32)
    acc = alpha * acc + pv.reshape(r, q, -1)
    return m_new, l, acc


def _nsa_kernel(q_ref, kc_ref, vc_ref, ks_ref, vs_ref, kw_ref, vw_ref, gl_ref, ov_ref, ex_ref, o_ref):
    R = HEADS_PER_GROUP
    tq = q_ref.shape[3]
    ncp = kc_ref.shape[2]
    nsl = ov_ref.shape[1]
    n_top = min(N_SEL, nsl)
    qi = pl.program_id(2)
    qs = qi * tq
    scale = HEAD_DIM ** -0.5
    q = (q_ref[0, 0] * scale).astype(BF16).reshape(R * tq, HEAD_DIM)
    t_q = qs + lax.broadcasted_iota(I32, (tq, 1), 0)

    s = _nt_dot(q, kc_ref[0, 0]).reshape(R, tq, ncp)
    c_idx = lax.broadcasted_iota(I32, (tq, ncp), 1)
    valid_c = (c_idx * CMP_STRIDE + (CMP_BLOCK - 1)) <= t_q
    s = jnp.where(valid_c[None], s, NEG)
    s = s - jnp.max(s, axis=-1, keepdims=True)
    p = jnp.where(valid_c[None], jnp.exp(s), 0.0)
    p = p / jnp.maximum(jnp.sum(p, axis=-1, keepdims=True), TINY)
    o_c = jnp.dot(p.reshape(R * tq, ncp).astype(BF16), vc_ref[0, 0],
                  preferred_element_type=F32).reshape(R, tq, HEAD_DIM)

    p_sum = p[0]
    for r in range(1, R):
        p_sum = p_sum + p[r]
    ph, plo = _split_bf16(p_sum)
    imp = (jnp.dot(ph, ov_ref[...], preferred_element_type=F32)
           + jnp.dot(plo, ov_ref[...], preferred_element_type=F32))
    j_idx = lax.broadcasted_iota(I32, (tq, nsl), 1).astype(F32)
    bt = (t_q >> (SEL_BLOCK.bit_length() - 1)).astype(F32)
    forced = (j_idx == 0) | (j_idx == bt) | (j_idx == bt - 1)
    future = j_idx > bt
    work = jnp.where(forced, FORCE, jnp.where(future, -FORCE, imp))
    sel = jnp.zeros((tq, nsl), F32)
    for _ in range(n_top):
        mx = jnp.max(work, axis=-1, keepdims=True)
        first = jnp.min(jnp.where(work == mx, j_idx, float(nsl)), axis=-1, keepdims=True)
        hit = j_idx == first
        sel = jnp.where(hit, 1.0, sel)
        work = jnp.where(hit, -jnp.inf, work)
    sel = jnp.where(future, 0.0, sel).astype(BF16)

    init = (jnp.full((R, tq, 1), NEG, F32), jnp.zeros((R, tq, 1), F32), jnp.zeros((R, tq, HEAD_DIM), F32))

    tk = ex_ref.shape[2]

    def sel_step(kt, carry):
        k0 = pl.multiple_of(kt * tk, tk)
        in_sel = jnp.dot(sel, ex_ref[kt], preferred_element_type=F32) > 0.5
        kpos = k0 + lax.broadcasted_iota(I32, (tq, tk), 1)
        valid = in_sel & (kpos <= t_q)
        s3 = _nt_dot(q, ks_ref[0, 0, pl.ds(k0, tk), :]).reshape(R, tq, tk)
        return _online_update(carry, s3, valid, vs_ref[0, 0, pl.ds(k0, tk), :])

    m_s, l_s, acc_s = lax.fori_loop(0, qs // tk + 1, sel_step, init)
    o_s = acc_s / jnp.maximum(l_s, TINY)

    tw = WIN_KEY_TILE

    def win_step(it, carry):
        k0 = pl.multiple_of(qs + (tq - tw) - it * tw, tw)
        kpos = k0 + lax.broadcasted_iota(I32, (tq, tw), 1)
        valid = (kpos <= t_q) & (kpos > t_q - WINDOW)
        s3 = _nt_dot(q, kw_ref[0, 0, pl.ds(k0, tw), :]).reshape(R, tq, tw)
        return _online_update(carry, s3, valid, vw_ref[0, 0, pl.ds(k0, tw), :])

    n_win = jnp.minimum((qs + (tq - tw)) // tw, (WINDOW + tq - tw) // tw) + 1
    m_w, l_w, acc_w = lax.fori_loop(0, n_win, win_step, init)
    o_w = acc_w / jnp.maximum(l_w, TINY)

    gates = jax.nn.sigmoid(gl_ref[0, 0])
    o = gates[:, :, 0:1] * o_c + gates[:, :, 1:2] * o_s + gates[:, :, 2:3] * o_w
    o_ref[0, 0] = o.astype(o_ref.dtype)


def _nsa_attention(q, kcmp, vcmp, ks, vs, kw, vw, gl):
    bsz, g, r, seq, dk = q.shape
    ncp = kcmp.shape[2]
    nsl = seq // SEL_BLOCK
    tq = min(Q_TILE, seq)
    tk = min(SEL_KEY_TILE, seq)
    c_start = jnp.arange(ncp) * CMP_STRIDE
    j_start = jnp.arange(nsl) * SEL_BLOCK
    overlap = ((c_start[:, None] < j_start[None, :] + SEL_BLOCK)
               & (c_start[:, None] + CMP_BLOCK > j_start[None, :])).astype(BF16)
    key_blk = (jnp.arange(seq) // SEL_BLOCK).reshape(seq // tk, 1, tk)
    expand = (jnp.arange(nsl)[None, :, None] == key_blk).astype(BF16)
    kv_spec = pl.BlockSpec((1, 1, seq, dk), lambda b, j, i: (b, j, 0, 0))
    cmp_spec = pl.BlockSpec((1, 1, ncp, dk), lambda b, j, i: (b, j, 0, 0))
    return pl.pallas_call(
        _nsa_kernel,
        grid=(bsz, g, seq // tq),
        in_specs=[pl.BlockSpec((1, 1, r, tq, dk), lambda b, j, i: (b, j, 0, i, 0)),
                  cmp_spec, cmp_spec, kv_spec, kv_spec, kv_spec, kv_spec,
                  pl.BlockSpec((1, 1, r, tq, 3), lambda b, j, i: (b, j, 0, i, 0)),
                  pl.BlockSpec((ncp, nsl), lambda b, j, i: (0, 0)),
                  pl.BlockSpec((seq // tk, nsl, tk), lambda b, j, i: (0, 0, 0))],
        out_specs=pl.BlockSpec((1, 1, r, tq, dk), lambda b, j, i: (b, j, 0, i, 0)),
        out_shape=jax.ShapeDtypeStruct((bsz, g, r, seq, dk), BF16),
        compiler_params=_params("arbitrary", "arbitrary", "arbitrary"),
    )(q, kcmp, vcmp, ks, vs, kw, vw, gl, overlap, expand)


def _rope_tables(seq, width):
    pos = jnp.arange(seq, dtype=F32)
    inv = ROPE_THETA ** (-jnp.arange(0, HEAD_DIM, 2, dtype=F32) / HEAD_DIM)
    ang = pos[:, None] * inv[None, :]
    cos, sin = jnp.cos(ang), jnp.sin(ang)
    cos_h = jnp.concatenate([cos, cos], axis=1)
    sin_h = jnp.concatenate([-sin, sin], axis=1)
    reps = width // HEAD_DIM
    return jnp.tile(cos_h, (1, reps)), jnp.tile(sin_h, (1, reps))


def _nsa_mixer_heads(x, bsz, seq, w_in, cmp_k_pe, cmp_k_w1, cmp_k_w2, cmp_v_pe, cmp_v_w1, cmp_v_w2):
    G, R, dk = N_KV_GROUPS, HEADS_PER_GROUP, HEAD_DIM
    qd = N_HEADS * dk
    cut = [qd + j * KV_WIDTH for j in range(7)]
    w = w_in.astype(BF16)
    w_q, w_kc, w_vc, w_ks, w_vs, w_kw, w_vw = (w[:, :qd], w[:, cut[0]:cut[1]], w[:, cut[1]:cut[2]],
                                               w[:, cut[2]:cut[3]], w[:, cut[3]:cut[4]],
                                               w[:, cut[4]:cut[5]], w[:, cut[5]:cut[6]])
    w_gl = w[:, cut[6]:]
    n_gl = w_gl.shape[1]
    gl_pad = LANES - n_gl
    tabs512 = _rope_tables(seq, 512)
    tabs256 = (tabs512[0][:, :KV_WIDTH], tabs512[1][:, :KV_WIDTH])

    roped = _matmul(x, jnp.concatenate([w_q, w_ks, w_kw], axis=1), BF16, 512, tabs512, seq)
    kc = _matmul(x, w_kc, F32, KV_WIDTH, tabs256, seq)
    vcg = _matmul(x, jnp.concatenate([w_vc, w_gl, jnp.zeros((w.shape[0], gl_pad), BF16)], axis=1),
                  F32, KV_WIDTH + LANES)
    vsw = _matmul(x, jnp.concatenate([w_vs, w_vw], axis=1), BF16, 2 * KV_WIDTH)

    def kv_layout(a):
        return a.reshape(bsz, seq, G, dk).transpose(0, 2, 1, 3)

    def chunks(a):
        a = a.reshape(bsz, seq // CMP_STRIDE, CMP_STRIDE, G, dk).transpose(0, 3, 1, 2, 4)
        return a.reshape(bsz, G, seq // CMP_STRIDE, CMP_STRIDE * dk)

    q = roped[:, :qd].reshape(bsz, seq, G, R, dk).transpose(0, 2, 3, 1, 4)
    ks = kv_layout(roped[:, qd:qd + KV_WIDTH])
    kw = kv_layout(roped[:, qd + KV_WIDTH:])
    vs = kv_layout(vsw[:, :KV_WIDTH])
    vw = kv_layout(vsw[:, KV_WIDTH:])
    gl = vcg[:, KV_WIDTH:KV_WIDTH + n_gl].reshape(bsz, seq, G, R, 3).transpose(0, 2, 3, 1, 4)
    kcmp = _compress(chunks(kc), cmp_k_pe, cmp_k_w1, cmp_k_w2)
    vcmp = _compress(chunks(vcg[:, :KV_WIDTH]), cmp_v_pe, cmp_v_w1, cmp_v_w2)
    o = _nsa_attention(q, kcmp, vcmp, ks, vs, kw, vw, gl)
    return o.transpose(0, 3, 1, 2, 4).reshape(bsz * seq, qd)


def _even_mixer_cat(x, bsz, seq, w_in, conv_w, lru_conv_w, lru_conv_b, w_a, b_a, w_x, b_x, lam):
    z = _matmul(x, w_in.astype(BF16), F32, 512)
    wa_bd = jax.scipy.linalg.block_diag(*w_a).astype(BF16)
    wx_bd = jax.scipy.linalg.block_diag(*w_x).astype(BF16)
    return _l0_mixer(z, bsz, seq, conv_w, lru_conv_w, lru_conv_b, wa_bd, b_a, wx_bd, b_x, lam)


def kernel(x, l0_w_in, l0_conv_w, l0_lru_conv_w, l0_lru_conv_b, l0_lru_w_a, l0_lru_b_a, l0_lru_w_x, l0_lru_b_x, l0_lru_lambda, l0_w_out, l0_ln1_g, l0_ln1_b, l0_router_group_w, l0_router_group_b, l0_router_expert_w, l0_router_expert_b, l0_expert_w_gate, l0_expert_w_up, l0_expert_w_down, l0_ln2_g, l0_ln2_b, l1_w_in, l1_cmp_k_pe, l1_cmp_k_w1, l1_cmp_k_w2, l1_cmp_v_pe, l1_cmp_v_w1, l1_cmp_v_w2, l1_w_out, l1_ln1_g, l1_ln1_b, l1_router_group_w, l1_router_group_b, l1_router_expert_w, l1_router_expert_b, l1_expert_w_gate, l1_expert_w_up, l1_expert_w_down, l1_ln2_g, l1_ln2_b):
    bsz, seq, d = x.shape
    h = x.reshape(bsz * seq, d)
    y = _even_mixer_cat(h, bsz, seq, l0_w_in, l0_conv_w, l0_lru_conv_w, l0_lru_conv_b, l0_lru_w_a,
                        l0_lru_b_a, l0_lru_w_x, l0_lru_b_x, l0_lru_lambda)
    h = _matmul_residual_ln(y, l0_w_out.astype(BF16), h, l0_ln1_g, l0_ln1_b)
    h = _hier_moe_ln(h, l0_router_group_w, l0_router_group_b, l0_router_expert_w, l0_router_expert_b,
                     l0_expert_w_gate, l0_expert_w_up, l0_expert_w_down, l0_ln2_g, l0_ln2_b)
    y = _nsa_mixer_heads(h, bsz, seq, l1_w_in, l1_cmp_k_pe, l1_cmp_k_w1, l1_cmp_k_w2, l1_cmp_v_pe,
                         l1_cmp_v_w1, l1_cmp_v_w2)
    h = _matmul_residual_ln(y, l1_w_out.astype(BF16), h, l1_ln1_g, l1_ln1_b)
    h = _hier_moe_ln(h, l1_router_group_w, l1_router_group_b, l1_router_expert_w, l1_router_expert_b,
                     l1_expert_w_gate, l1_expert_w_up, l1_expert_w_down, l1_ln2_g, l1_ln2_b)
    return h.reshape(bsz, seq, d)
```

```python
import functools

import jax
import jax.numpy as jnp
from jax import lax
from jax.experimental import pallas as pl
from jax.experimental.pallas import tpu as pltpu

F32 = jnp.float32
BF16 = jnp.bfloat16
I32 = jnp.int32

DEPTH = 2
DN_ALPHA = (2 * DEPTH) ** 0.25
LN_EPS = 1e-5
CONV_WIDTH = 512
CONV_TAPS = 3
LRU_WIDTH = 512
LRU_CONV_TAPS = 4
LRU_C = 8.0
N_HEADS = 16
HEAD_DIM = 64
N_KV_GROUPS = 4
HEADS_PER_GROUP = N_HEADS // N_KV_GROUPS
KV_WIDTH = N_KV_GROUPS * HEAD_DIM
CMP_STRIDE = 16
CMP_BLOCK = 32
SEL_BLOCK = 64
N_SEL = 16
WINDOW = 512
ROPE_THETA = 10000.0
FORCE = 1e4
NEG = -1e30
TINY = 1e-30
QK_SCALE = HEAD_DIM ** -0.5 * 1.4426950408889634
N_GROUPS = 4
EXPERTS_PER_GROUP = 8
N_EXPERTS = N_GROUPS * EXPERTS_PER_GROUP
TOP_K = 2
MOE_BLOCK = 256

LANES = 128
VMEM_LIMIT_BYTES = 48 * 1024 * 1024
ROW_TILE = 512
SCAN_CHUNK = 256
ROUTE_TILE = 512
GATHER_TILE = 256
Q_TILE = 256
SEL_KEY_TILE = 512
V_PAD = 16


def _params(*sem):
    return pltpu.CompilerParams(dimension_semantics=sem, vmem_limit_bytes=VMEM_LIMIT_BYTES)


def _layernorm(v, g, b):
    mu = jnp.mean(v, axis=-1, keepdims=True)
    d = v - mu
    var = jnp.mean(d * d, axis=-1, keepdims=True)
    return d * lax.rsqrt(var + LN_EPS) * g + b


def _gelu_tanh(x):
    return 0.5 * x * (1.0 + jnp.tanh(0.7978845608028654 * (x + 0.044715 * (x * x * x))))


def _mm_kernel(x_ref, w_ref, *rest, rope):
    o_ref = rest[-1]
    acc = jnp.dot(x_ref[...].astype(BF16), w_ref[...], preferred_element_type=F32)
    if rope:
        cos_ref, sin_ref = rest[0], rest[1]
        tn = acc.shape[1]
        lane = lax.broadcasted_iota(I32, acc.shape, 1)
        first_half = (lane & (HEAD_DIM // 2)) == 0
        partner = jnp.where(first_half, pltpu.roll(acc, tn - HEAD_DIM // 2, 1),
                            pltpu.roll(acc, HEAD_DIM // 2, 1))
        acc = acc * cos_ref[0] + partner * sin_ref[0]
    o_ref[...] = acc.astype(o_ref.dtype)


def _matmul(x, w, out_dtype, tn, rope_tabs=None, seq=None, first_plain_col_block=0):
    m, k = x.shape
    n = w.shape[1]
    tm = min(ROW_TILE, m)
    grid = (n // tn, m // tm)
    in_specs = [pl.BlockSpec((tm, k), lambda j, i: (i, 0)),
                pl.BlockSpec((k, tn), lambda j, i: (0, j))]
    args = [x, w]
    if rope_tabs is not None:
        ns = seq // tm
        tab = lambda j, i: (jnp.where(j < first_plain_col_block, 0, 1), i % ns, 0)
        in_specs += [pl.BlockSpec((1, tm, tn), tab)] * 2
        args += list(rope_tabs)
    return pl.pallas_call(
        functools.partial(_mm_kernel, rope=rope_tabs is not None),
        grid=grid, in_specs=in_specs,
        out_specs=pl.BlockSpec((tm, tn), lambda j, i: (i, j)),
        out_shape=jax.ShapeDtypeStruct((m, n), out_dtype),
        compiler_params=_params("arbitrary", "arbitrary"),
    )(*args)


def _mm_res_ln_kernel(y_ref, w_ref, x_ref, g_ref, b_ref, o_ref):
    acc = jnp.dot(y_ref[...].astype(BF16), w_ref[...], preferred_element_type=F32)
    o_ref[...] = _layernorm(DN_ALPHA * x_ref[...] + acc, g_ref[...], b_ref[...])


def _matmul_residual_ln(y, w, x, g, b):
    m, k = y.shape
    d = w.shape[1]
    tm = min(ROW_TILE, m)
    return pl.pallas_call(
        _mm_res_ln_kernel,
        grid=(m // tm,),
        in_specs=[pl.BlockSpec((tm, k), lambda i: (i, 0)),
                  pl.BlockSpec((k, d), lambda i: (0, 0)),
                  pl.BlockSpec((tm, d), lambda i: (i, 0)),
                  pl.BlockSpec((1, d), lambda i: (0, 0)),
                  pl.BlockSpec((1, d), lambda i: (0, 0))],
        out_specs=pl.BlockSpec((tm, d), lambda i: (i, 0)),
        out_shape=jax.ShapeDtypeStruct((m, d), F32),
        compiler_params=_params("arbitrary"),
    )(y, w, x, g.reshape(1, d), b.reshape(1, d))


def _shift_rows(cur, tail, d, row8):
    rolled = pltpu.roll(cur, d, 0)
    head = jnp.where(row8 < d, pltpu.roll(tail, d, 0), rolled[:8])
    return jnp.concatenate([head, rolled[8:]], axis=0)


def _l0_mixer_kernel(z_ref, cw_ref, lw_ref, lb_ref, wa_ref, ba_ref, wx_ref, bx_ref, lam_ref,
                     o_ref, tail_c, tail_x, h_state):
    W = CONV_WIDTH
    tc = z_ref.shape[0]

    @pl.when(pl.program_id(1) == 0)
    def _():
        tail_c[...] = jnp.zeros_like(tail_c)
        tail_x[...] = jnp.zeros_like(tail_x)
        h_state[...] = jnp.zeros_like(h_state)

    row8 = lax.broadcasted_iota(I32, (8, W), 0)
    row = lax.broadcasted_iota(I32, (tc, W), 0)

    ch = z_ref[:, W:2 * W] * z_ref[:, 2 * W:3 * W]
    tc_prev = tail_c[...]
    conv = ch * cw_ref[CONV_TAPS - 1:CONV_TAPS, :]
    for d in range(1, CONV_TAPS):
        conv = conv + _shift_rows(ch, tc_prev, d, row8) * cw_ref[CONV_TAPS - 1 - d:CONV_TAPS - d, :]
    tail_c[...] = ch[tc - 8:, :]
    o_ref[:, :W] = (z_ref[:, :W] * conv).astype(o_ref.dtype)

    xl = z_ref[:, 4 * W:5 * W]
    tx_prev = tail_x[...]
    xc = xl * lw_ref[LRU_CONV_TAPS - 1:LRU_CONV_TAPS, :]
    for d in range(1, LRU_CONV_TAPS):
        xc = xc + _shift_rows(xl, tx_prev, d, row8) * lw_ref[LRU_CONV_TAPS - 1 - d:LRU_CONV_TAPS - d, :]
    xc = xc + lb_ref[...]
    tail_x[...] = xl[tc - 8:, :]

    xcb = xc.astype(BF16)
    r = jax.nn.sigmoid(jnp.dot(xcb, wa_ref[...], preferred_element_type=F32) + ba_ref[...])
    ig = jax.nn.sigmoid(jnp.dot(xcb, wx_ref[...], preferred_element_type=F32) + bx_ref[...])
    nl = -lam_ref[...]
    softplus = jnp.maximum(nl, 0.0) + jnp.log1p(jnp.exp(-jnp.abs(nl)))
    log_a = -LRU_C * r * softplus
    a = jnp.exp(log_a)
    mult = jnp.sqrt(jnp.maximum(1.0 - a * a, 0.0))
    u = mult * ig * xc

    d = 1
    while d < tc:
        keep = row >= d
        a_s = jnp.where(keep, pltpu.roll(a, d, 0), 1.0)
        u_s = jnp.where(keep, pltpu.roll(u, d, 0), 0.0)
        u = a * u_s + u
        a = a * a_s
        d *= 2
    h = a * h_state[0:1, :] + u
    h_state[...] = jnp.broadcast_to(h[tc - 1:tc, :], h_state.shape)
    o_ref[:, W:] = (h * _gelu_tanh(z_ref[:, 3 * W:4 * W])).astype(o_ref.dtype)


def _l0_mixer(z, bsz, seq, conv_w, lru_conv_w, lru_conv_b, wa_bd, b_a, wx_bd, b_x, lam):
    W = CONV_WIDTH
    tc = min(SCAN_CHUNK, seq)
    nchunk = seq // tc
    vec = lambda i, j: (0, 0)
    return pl.pallas_call(
        _l0_mixer_kernel,
        grid=(bsz, nchunk),
        in_specs=[pl.BlockSpec((tc, 5 * W), lambda i, j: (i * nchunk + j, 0)),
                  pl.BlockSpec((CONV_TAPS, W), vec),
                  pl.BlockSpec((LRU_CONV_TAPS, W), vec),
                  pl.BlockSpec((1, W), vec),
                  pl.BlockSpec((W, W), vec),
                  pl.BlockSpec((1, W), vec),
                  pl.BlockSpec((W, W), vec),
                  pl.BlockSpec((1, W), vec),
                  pl.BlockSpec((1, W), vec)],
        out_specs=pl.BlockSpec((tc, 2 * W), lambda i, j: (i * nchunk + j, 0)),
        out_shape=jax.ShapeDtypeStruct((bsz * seq, 2 * W), BF16),
        scratch_shapes=[pltpu.VMEM((8, W), F32), pltpu.VMEM((8, W), F32), pltpu.VMEM((8, W), F32)],
        compiler_params=_params("arbitrary", "arbitrary"),
    )(z, conv_w, lru_conv_w, lru_conv_b.reshape(1, W), wa_bd, b_a.reshape(1, W), wx_bd,
      b_x.reshape(1, W), lam.reshape(1, W))


def _split_bf16(v):
    hi = v.astype(BF16)
    lo = (v - hi.astype(F32)).astype(BF16)
    return hi, lo


def _route_kernel(x_ref, wh_ref, wl_ref, b_ref, info_ref, cnt_ref):
    tm = x_ref.shape[0]

    @pl.when(pl.program_id(0) == 0)
    def _():
        cnt_ref[...] = jnp.zeros_like(cnt_ref)

    xh, xl = _split_bf16(x_ref[...])
    logits = (jnp.dot(xh, wh_ref[...], preferred_element_type=F32)
              + jnp.dot(xl, wh_ref[...], preferred_element_type=F32)
              + jnp.dot(xh, wl_ref[...], preferred_element_type=F32)) + b_ref[...]
    lane = lax.broadcasted_iota(I32, logits.shape, 1).astype(F32)
    ninf = -jnp.inf

    is_g = lane < N_GROUPS
    gmax = jnp.max(jnp.where(is_g, logits, ninf), axis=-1, keepdims=True)
    g_star = jnp.min(jnp.where(is_g & (logits == gmax), lane, float(LANES)), axis=-1, keepdims=True)
    gsum = jnp.sum(jnp.where(is_g, jnp.exp(logits - gmax), 0.0), axis=-1, keepdims=True)
    p_grp = 1.0 / gsum

    lo_lane = N_GROUPS + EXPERTS_PER_GROUP * g_star
    is_e = (lane >= lo_lane) & (lane < lo_lane + EXPERTS_PER_GROUP)
    emax = jnp.max(jnp.where(is_e, logits, ninf), axis=-1, keepdims=True)
    ex = jnp.where(is_e, jnp.exp(logits - emax), 0.0)
    pe = ex / jnp.sum(ex, axis=-1, keepdims=True)
    pe_m = jnp.where(is_e, pe, ninf)
    v1 = jnp.max(pe_m, axis=-1, keepdims=True)
    l1 = jnp.min(jnp.where(pe_m == v1, lane, float(LANES)), axis=-1, keepdims=True)
    pe_m2 = jnp.where(lane == l1, ninf, pe_m)
    v2 = jnp.max(pe_m2, axis=-1, keepdims=True)
    l2 = jnp.min(jnp.where(pe_m2 == v2, lane, float(LANES)), axis=-1, keepdims=True)
    vs = v1 + v2
    w1 = p_grp * v1 / vs
    w2 = p_grp * v2 / vs

    hit1 = lane == l1
    hit2 = lane == l2
    onehot = jnp.where(hit1 | hit2, 1.0, 0.0)
    r_i = lax.broadcasted_iota(I32, (tm, tm), 0)
    c_i = lax.broadcasted_iota(I32, (tm, tm), 1)
    tri = jnp.where(c_i < r_i, 1.0, 0.0).astype(BF16)
    before = jnp.dot(tri, onehot.astype(BF16), preferred_element_type=F32) + cnt_ref[0:1, :]
    rank1 = jnp.sum(jnp.where(hit1, before, 0.0), axis=-1, keepdims=True)
    rank2 = jnp.sum(jnp.where(hit2, before, 0.0), axis=-1, keepdims=True)
    cnt_ref[...] = cnt_ref[...] + jnp.sum(onehot, axis=0, keepdims=True)

    e1 = l1 - N_GROUPS
    e2 = l2 - N_GROUPS
    info = jnp.where(lane == 0, e1, jnp.where(lane == 1, e2, jnp.where(lane == 2, w1, jnp.where(
        lane == 3, w2, jnp.where(lane == 4, rank1, jnp.where(lane == 5, rank2, 0.0))))))
    info_ref[...] = info


def _moe_route(x, w_rg, b_rg, w_re, b_re):
    t, d = x.shape
    tm = min(ROUTE_TILE, t)
    npad = LANES - N_GROUPS - N_EXPERTS
    w = jnp.concatenate([w_rg, w_re, jnp.zeros((d, npad), F32)], axis=1)
    wh = w.astype(BF16)
    wl = (w - wh.astype(F32)).astype(BF16)
    b = jnp.concatenate([b_rg, b_re, jnp.zeros((npad,), F32)]).reshape(1, LANES)
    return pl.pallas_call(
        _route_kernel,
        grid=(t // tm,),
        in_specs=[pl.BlockSpec((tm, d), lambda i: (i, 0)),
                  pl.BlockSpec((d, LANES), lambda i: (0, 0)),
                  pl.BlockSpec((d, LANES), lambda i: (0, 0)),
                  pl.BlockSpec((1, LANES), lambda i: (0, 0))],
        out_specs=[pl.BlockSpec((tm, LANES), lambda i: (i, 0)),
                   pl.BlockSpec((8, LANES), lambda i: (0, 0))],
        out_shape=[jax.ShapeDtypeStruct((t, LANES), F32), jax.ShapeDtypeStruct((8, LANES), F32)],
        compiler_params=_params("arbitrary"),
    )(x, wh, wl, b)


def _row_copy(src_ref, src_row, dst_ref, dst_row, sem):
    return pltpu.make_async_copy(src_ref.at[pl.ds(src_row, 1), :], dst_ref.at[pl.ds(dst_row, 1), :], sem)


def _dispatch_kernel(dest_ref, x_ref, xs_in_ref, xs_ref, sem):
    del xs_in_ref
    tt = x_ref.shape[0]

    def issue(i, c):
        for k in range(TOP_K):
            _row_copy(x_ref, i, xs_ref, dest_ref[0, 0, TOP_K * i + k], sem).start()
        return c

    lax.fori_loop(0, tt, issue, 0, unroll=8)
    for _ in range(TOP_K):
        pltpu.make_async_copy(x_ref, xs_ref.at[pl.ds(0, tt), :], sem).wait()


def _moe_dispatch(x, dest3, n_rows):
    t, d = x.shape
    tt = dest3.shape[2] // TOP_K
    xs0 = jnp.zeros((n_rows, d), F32)
    return pl.pallas_call(
        _dispatch_kernel,
        grid=(t // tt,),
        in_specs=[pl.BlockSpec((1, 1, TOP_K * tt), lambda i: (i, 0, 0), memory_space=pltpu.SMEM),
                  pl.BlockSpec((tt, d), lambda i: (i, 0)),
                  pl.BlockSpec(memory_space=pl.ANY)],
        out_specs=pl.BlockSpec(memory_space=pl.ANY),
        out_shape=jax.ShapeDtypeStruct((n_rows, d), F32),
        scratch_shapes=[pltpu.SemaphoreType.DMA(())],
        input_output_aliases={2: 0},
        compiler_params=_params("arbitrary"),
    )(dest3, x, xs0)


def _expert_kernel(be_ref, xs_ref, wg_ref, wu_ref, wd_ref, ys_ref):
    del be_ref
    xb = xs_ref[...].astype(BF16)
    gate = jnp.dot(xb, wg_ref[0], preferred_element_type=F32)
    up = jnp.dot(xb, wu_ref[0], preferred_element_type=F32)
    h = gate * jax.nn.sigmoid(gate) * up
    ys_ref[...] = jnp.dot(h.astype(BF16), wd_ref[0], preferred_element_type=F32)


def _moe_experts(xs, block_e, wg, wu, wd):
    p, d = xs.shape
    ff = wg.shape[2]
    nb = p // MOE_BLOCK
    return pl.pallas_call(
        _expert_kernel,
        grid_spec=pltpu.PrefetchScalarGridSpec(
            num_scalar_prefetch=1, grid=(nb,),
            in_specs=[pl.BlockSpec((MOE_BLOCK, d), lambda i, be: (i, 0)),
                      pl.BlockSpec((1, d, ff), lambda i, be: (be[i], 0, 0)),
                      pl.BlockSpec((1, d, ff), lambda i, be: (be[i], 0, 0)),
                      pl.BlockSpec((1, ff, d), lambda i, be: (be[i], 0, 0))],
            out_specs=pl.BlockSpec((MOE_BLOCK, d), lambda i, be: (i, 0))),
        out_shape=jax.ShapeDtypeStruct((p, d), F32),
        compiler_params=_params("arbitrary"),
    )(block_e, xs, wg, wu, wd)


def _combine_ln_kernel(dest_ref, x_ref, info_ref, g_ref, b_ref, ys_ref, o_ref, buf, sem):
    tt = x_ref.shape[0]

    def issue(i, c):
        for k in range(TOP_K):
            _row_copy(ys_ref, dest_ref[0, 0, TOP_K * i + k], buf, k * tt + i, sem).start()
        return c

    lax.fori_loop(0, tt, issue, 0, unroll=8)
    pltpu.make_async_copy(ys_ref.at[pl.ds(0, TOP_K * tt), :], buf, sem).wait()
    y = info_ref[:, 2:3] * buf[0:tt, :] + info_ref[:, 3:4] * buf[tt:2 * tt, :]
    o_ref[...] = _layernorm(DN_ALPHA * x_ref[...] + y, g_ref[...], b_ref[...])


def _moe_combine_ln(x, info, dest3, ys, g, b):
    t, d = x.shape
    tt = dest3.shape[2] // TOP_K
    return pl.pallas_call(
        _combine_ln_kernel,
        grid=(t // tt,),
        in_specs=[pl.BlockSpec((1, 1, TOP_K * tt), lambda i: (i, 0, 0), memory_space=pltpu.SMEM),
                  pl.BlockSpec((tt, d), lambda i: (i, 0)),
                  pl.BlockSpec((tt, LANES), lambda i: (i, 0)),
                  pl.BlockSpec((1, d), lambda i: (0, 0)),
                  pl.BlockSpec((1, d), lambda i: (0, 0)),
                  pl.BlockSpec(memory_space=pl.ANY)],
        out_specs=pl.BlockSpec((tt, d), lambda i: (i, 0)),
        out_shape=jax.ShapeDtypeStruct((t, d), F32),
        scratch_shapes=[pltpu.VMEM((TOP_K * tt, d), F32), pltpu.SemaphoreType.DMA(())],
        compiler_params=_params("arbitrary"),
    )(dest3, x, info, g.reshape(1, d), b.reshape(1, d), ys)


def _hier_moe_ln(x, w_rg, b_rg, w_re, b_re, w_gate, w_up, w_down, ln_g, ln_b):
    t, d = x.shape
    a_total = t * TOP_K
    info, cnt = _moe_route(x, w_rg, b_rg, w_re, b_re)
    counts = cnt[0, N_GROUPS:N_GROUPS + N_EXPERTS].astype(I32)
    padded = (counts + MOE_BLOCK - 1) // MOE_BLOCK * MOE_BLOCK
    pad_end = jnp.cumsum(padded)
    pad_start = pad_end - padded
    n_blocks = -(-a_total // MOE_BLOCK) + N_EXPERTS
    e = info[:, 0:TOP_K].astype(I32)
    rank = info[:, 4:4 + TOP_K].astype(I32)
    dest = pad_start[e] + rank
    blk_start = jnp.arange(n_blocks, dtype=I32) * MOE_BLOCK
    block_e = jnp.minimum(jnp.sum((pad_end[None, :] <= blk_start[:, None]).astype(I32), axis=1),
                          N_EXPERTS - 1).astype(I32)
    tt = min(GATHER_TILE, t)
    dest3 = dest.reshape(t // tt, 1, TOP_K * tt)
    xs = _moe_dispatch(x, dest3, n_blocks * MOE_BLOCK)
    ys = _moe_experts(xs, block_e, w_gate.astype(BF16), w_up.astype(BF16), w_down.astype(BF16))
    return _moe_combine_ln(x, info, dest3, ys, ln_g, ln_b)


def _compress_kernel(ch_ref, pe_ref, w1_ref, w2_ref, o_ref):
    half = CMP_STRIDE * HEAD_DIM
    c = ch_ref[0, 0]
    n = c.shape[0]
    first = jnp.dot((c + pe_ref[0:1, :]).astype(BF16), w1_ref[:half, :], preferred_element_type=F32)
    second = jnp.dot((c + pe_ref[1:2, :]).astype(BF16), w1_ref[half:, :], preferred_element_type=F32)
    h = first + pltpu.roll(second, n - 1, 0)
    o_ref[0, 0] = jnp.dot(_gelu_tanh(h).astype(BF16), w2_ref[...],
                          preferred_element_type=F32).astype(o_ref.dtype)


def _compress(ch, pe, w1, w2):
    bsz, g, n, half = ch.shape
    hid = w1.shape[1]
    return pl.pallas_call(
        _compress_kernel,
        grid=(bsz, g),
        in_specs=[pl.BlockSpec((1, 1, n, half), lambda b, j: (b, j, 0, 0)),
                  pl.BlockSpec((2, half), lambda b, j: (0, 0)),
                  pl.BlockSpec((2 * half, hid), lambda b, j: (0, 0)),
                  pl.BlockSpec((hid, HEAD_DIM), lambda b, j: (0, 0))],
        out_specs=pl.BlockSpec((1, 1, n, HEAD_DIM), lambda b, j: (b, j, 0, 0)),
        out_shape=jax.ShapeDtypeStruct((bsz, g, n, HEAD_DIM), BF16),
        compiler_params=_params("arbitrary", "arbitrary"),
    )(ch, pe.reshape(2, half), w1.astype(BF16), w2.astype(BF16))


def _softmax_step(carry, s_t, v_t):
    m, acc = carry
    m_new = jnp.maximum(m, jnp.max(s_t, axis=0, keepdims=True))
    p = jnp.exp2(s_t - m_new).astype(BF16)
    acc = jnp.exp2(m - m_new) * acc + jnp.dot(v_t, p, preferred_element_type=F32)
    return m_new, acc


def _softmax_finish(acc):
    return acc[:HEAD_DIM] / jnp.maximum(acc[HEAD_DIM:HEAD_DIM + 1], TINY)


def _tile_heads(a):
    return jnp.concatenate([a] * HEADS_PER_GROUP, axis=1)


def _nsa_kernel(q_ref, kc_ref, vc_ref, ks_ref, vs_ref, kw_ref, vw_ref, gl_ref, ov_ref, o_ref):
    R = HEADS_PER_GROUP
    dk = HEAD_DIM
    tq = q_ref.shape[4] // R
    ncp = kc_ref.shape[2]
    nsl = ov_ref.shape[0]
    seq = kw_ref.shape[2]
    n_top = min(N_SEL, nsl)
    qi = pl.program_id(2)
    qs = qi * tq
    q_t = q_ref[0, 0, 0]
    t_lane = qs + lax.broadcasted_iota(I32, (1, tq), 1)

    init = (jnp.full((1, R * tq), NEG, F32), jnp.zeros((dk + V_PAD, R * tq), F32))

    nw = WINDOW + tq
    w0 = pl.multiple_of(jnp.maximum(qs - WINDOW, 0), tq)
    wpos = w0 + lax.broadcasted_iota(I32, (nw, tq), 0)
    bias_w = jnp.where((wpos <= t_lane) & (wpos > t_lane - WINDOW), 0.0, NEG)
    s = jnp.dot(kw_ref[0, 0, pl.ds(w0, nw), :], q_t, preferred_element_type=F32) + _tile_heads(bias_w)
    c0 = w0 // tq
    vw_slab = jnp.concatenate([vw_ref[0, 0, c0 + i] for i in range(nw // tq)], axis=1)
    o_w = _softmax_finish(_softmax_step(init, s, vw_slab)[1])

    c_row = lax.broadcasted_iota(I32, (ncp, tq), 0)
    bias_c = jnp.where(c_row * CMP_STRIDE + (CMP_BLOCK - 1) <= t_lane, 0.0, NEG)
    s_t = jnp.dot(kc_ref[0, 0], q_t, preferred_element_type=F32) + _tile_heads(bias_c)
    p = jnp.exp2(s_t - jnp.max(s_t, axis=0, keepdims=True))
    inv_l = 1.0 / jnp.maximum(jnp.sum(p, axis=0, keepdims=True), TINY)
    has_c = jnp.where(t_lane >= CMP_BLOCK - 1, 1.0, 0.0)
    inv_l = inv_l * _tile_heads(has_c)
    o_c = jnp.dot(vc_ref[0, 0], p.astype(BF16), preferred_element_type=F32) * inv_l
    pn = p * inv_l
    p_sum = pn[:, 0:tq]
    for r in range(1, R):
        p_sum = p_sum + pn[:, r * tq:(r + 1) * tq]

    ph, plo = _split_bf16(p_sum)
    imp = (jnp.dot(ov_ref[...], ph, preferred_element_type=F32)
           + jnp.dot(ov_ref[...], plo, preferred_element_type=F32))
    j_row = lax.broadcasted_iota(I32, (nsl, tq), 0).astype(F32)
    bt = (t_lane >> (SEL_BLOCK.bit_length() - 1)).astype(F32)
    forced = (j_row == 0) | (j_row == bt) | (j_row == bt - 1)
    future = j_row > bt
    work = jnp.where(forced, FORCE, jnp.where(future, -FORCE, imp))
    picked = jnp.zeros((nsl, tq), F32)
    for _ in range(n_top):
        mx = jnp.max(work, axis=0, keepdims=True)
        first = jnp.min(jnp.where(work == mx, j_row, float(nsl)), axis=0, keepdims=True)
        hit = j_row == first
        picked = jnp.where(hit, 1.0, picked)
        work = jnp.where(hit, -jnp.inf, work)
    sel_bias = jnp.where((picked > 0.5) & jnp.logical_not(future), 0.0, NEG).astype(BF16)
    q_aug = jnp.concatenate([q_t, _tile_heads(sel_bias)], axis=0)

    tk = min(SEL_KEY_TILE, seq)

    def sel_scores(kt):
        k0 = pl.multiple_of(kt * tk, tk)
        s = jnp.dot(ks_ref[0, 0, pl.ds(k0, tk), :], q_aug, preferred_element_type=F32)
        return s, vs_ref[0, 0, kt], k0

    def sel_step(kt, carry):
        s, v_t, _ = sel_scores(kt)
        return _softmax_step(carry, s, v_t)

    def sel_pair(i, carry):
        return sel_step(2 * i + 1, sel_step(2 * i, carry))

    n_full = qs // tk
    carry = lax.fori_loop(0, n_full // 2, sel_pair, init)
    carry = lax.cond(n_full % 2 == 1, lambda c: sel_step(n_full - 1, c), lambda c: c, carry)
    s, v_t, k0 = sel_scores(n_full)
    kpos = k0 + lax.broadcasted_iota(I32, (tk, tq), 0)
    s = s + _tile_heads(jnp.where(kpos <= t_lane, 0.0, NEG))
    o_s = _softmax_finish(_softmax_step(carry, s, v_t)[1])

    gates = jax.nn.sigmoid(gl_ref[0, 0, 0])
    o = gates[0:1, :] * o_c + gates[1:2, :] * o_s + gates[2:3, :] * o_w
    o_ref[0, 0, 0] = o.astype(o_ref.dtype)


def _nsa_attention(q_t, kcmp, vcmp_t, ks_aug, vs_t, kw, vw_t, gl_t):
    bsz, g, nq, dk, rq = q_t.shape
    seq = kw.shape[2]
    ncp = kcmp.shape[2]
    nsl = seq // SEL_BLOCK
    assert seq >= WINDOW + rq // HEADS_PER_GROUP and seq % min(SEL_KEY_TILE, seq) == 0
    c_start = jnp.arange(ncp) * CMP_STRIDE
    j_start = jnp.arange(nsl) * SEL_BLOCK
    overlap_t = ((c_start[None, :] < j_start[:, None] + SEL_BLOCK)
                 & (c_start[None, :] + CMP_BLOCK > j_start[:, None])).astype(BF16)
    per_bg = lambda shape: pl.BlockSpec((1, 1) + shape, lambda b, j, i: (b, j) + (0,) * len(shape))
    per_q = lambda shape: pl.BlockSpec((1, 1, 1) + shape, lambda b, j, i: (b, j, i, 0, 0))
    tk = min(SEL_KEY_TILE, seq)
    tq = rq // HEADS_PER_GROUP
    return pl.pallas_call(
        _nsa_kernel,
        grid=(bsz, g, nq),
        in_specs=[per_q((dk, rq)),
                  per_bg((ncp, dk)), per_bg((dk, ncp)),
                  per_bg((seq, dk + nsl)), per_bg((seq // tk, dk + V_PAD, tk)),
                  per_bg((seq, dk)), per_bg((seq // tq, dk + V_PAD, tq)),
                  per_q((3, rq)),
                  pl.BlockSpec((nsl, ncp), lambda b, j, i: (0, 0))],
        out_specs=per_q((dk, rq)),
        out_shape=jax.ShapeDtypeStruct((bsz, g, nq, dk, rq), BF16),
        compiler_params=_params("arbitrary", "arbitrary", "arbitrary"),
    )(q_t, kcmp, vcmp_t, ks_aug, vs_t, kw, vw_t, gl_t, overlap_t)


def _rope_tables(seq, width):
    pos = jnp.arange(seq, dtype=F32)
    inv = ROPE_THETA ** (-jnp.arange(0, HEAD_DIM, 2, dtype=F32) / HEAD_DIM)
    ang = pos[:, None] * inv[None, :]
    cos, sin = jnp.cos(ang), jnp.sin(ang)
    cos_h = jnp.concatenate([cos, cos], axis=1)
    sin_h = jnp.concatenate([-sin, sin], axis=1)
    reps = width // HEAD_DIM
    return jnp.tile(cos_h, (1, reps)), jnp.tile(sin_h, (1, reps))


def _nsa_mixer_heads(x, bsz, seq, w_in, cmp_k_pe, cmp_k_w1, cmp_k_w2, cmp_v_pe, cmp_v_w1, cmp_v_w2):
    G, R, dk = N_KV_GROUPS, HEADS_PER_GROUP, HEAD_DIM
    qd = N_HEADS * dk
    tq = min(Q_TILE, seq)
    nq = seq // tq
    nsl = seq // SEL_BLOCK
    cut = [qd + j * KV_WIDTH for j in range(7)]
    w = w_in.astype(BF16)
    w_q, w_kc, w_vc, w_ks, w_vs, w_kw, w_vw = (w[:, :qd], w[:, cut[0]:cut[1]], w[:, cut[1]:cut[2]],
                                               w[:, cut[2]:cut[3]], w[:, cut[3]:cut[4]],
                                               w[:, cut[4]:cut[5]], w[:, cut[5]:cut[6]])
    w_gl = w[:, cut[6]:]
    n_gl = w_gl.shape[1]
    gl_pad = LANES - n_gl
    cos, sin = _rope_tables(seq, 512)
    tabs512 = (jnp.stack([cos * QK_SCALE, cos]), jnp.stack([sin * QK_SCALE, sin]))
    tabs256 = (tabs512[0][:, :, :KV_WIDTH], tabs512[1][:, :, :KV_WIDTH])

    roped = _matmul(x, jnp.concatenate([w_q, w_ks, w_kw], axis=1), BF16, 512, tabs512, seq,
                    first_plain_col_block=qd // 512)
    kc = _matmul(x, w_kc, F32, KV_WIDTH, tabs256, seq)
    vcg = _matmul(x, jnp.concatenate([w_vc, w_gl, jnp.zeros((w.shape[0], gl_pad), BF16)], axis=1),
                  F32, KV_WIDTH + LANES)
    vsw = _matmul(x, jnp.concatenate([w_vs, w_vw], axis=1), BF16, 2 * KV_WIDTH)

    def keys_rows(a):
        return a.reshape(bsz, seq, G, dk).transpose(0, 2, 1, 3)

    def keys_lanes(a, tile):
        a = a.reshape(bsz, seq // tile, tile, G, dk).transpose(0, 3, 1, 4, 2)
        extra = jnp.zeros((V_PAD, tile), BF16).at[0].set(1.0)
        return jnp.concatenate([a, jnp.broadcast_to(extra, a.shape[:3] + (V_PAD, tile))], axis=3)

    def chunks(a):
        a = a.reshape(bsz, seq // CMP_STRIDE, CMP_STRIDE, G, dk).transpose(0, 3, 1, 2, 4)
        return a.reshape(bsz, G, seq // CMP_STRIDE, CMP_STRIDE * dk)

    q_t = roped[:, :qd].reshape(bsz, nq, tq, G, R, dk).transpose(0, 3, 1, 5, 4, 2)
    q_t = q_t.reshape(bsz, G, nq, dk, R * tq)
    block_onehot = (jnp.arange(seq)[:, None] // SEL_BLOCK == jnp.arange(nsl)[None, :]).astype(BF16)
    ks_aug = jnp.concatenate([keys_rows(roped[:, qd:qd + KV_WIDTH]),
                              jnp.broadcast_to(block_onehot, (bsz, G, seq, nsl))], axis=-1)
    kw = keys_rows(roped[:, qd + KV_WIDTH:])
    vs_t = keys_lanes(vsw[:, :KV_WIDTH], min(SEL_KEY_TILE, seq))
    vw_t = keys_lanes(vsw[:, KV_WIDTH:], tq)
    gl_t = vcg[:, KV_WIDTH:KV_WIDTH + n_gl].reshape(bsz, nq, tq, G, R, 3).transpose(0, 3, 1, 5, 4, 2)
    gl_t = gl_t.reshape(bsz, G, nq, 3, R * tq)
    kcmp = _compress(chunks(kc), cmp_k_pe, cmp_k_w1, cmp_k_w2)
    vcmp_t = _compress(chunks(vcg[:, :KV_WIDTH]), cmp_v_pe, cmp_v_w1, cmp_v_w2).transpose(0, 1, 3, 2)
    o_t = _nsa_attention(q_t, kcmp, vcmp_t, ks_aug, vs_t, kw, vw_t, gl_t)
    o = o_t.reshape(bsz, G, nq, dk, R, tq).transpose(0, 2, 5, 1, 4, 3)
    return o.reshape(bsz * seq, qd)


def _even_mixer_cat(x, bsz, seq, w_in, conv_w, lru_conv_w, lru_conv_b, w_a, b_a, w_x, b_x, lam):
    z = _matmul(x, w_in.astype(BF16), F32, 512)
    wa_bd = jax.scipy.linalg.block_diag(*w_a).astype(BF16)
    wx_bd = jax.scipy.linalg.block_diag(*w_x).astype(BF16)
    return _l0_mixer(z, bsz, seq, conv_w, lru_conv_w, lru_conv_b, wa_bd, b_a, wx_bd, b_x, lam)


def kernel(x, l0_w_in, l0_conv_w, l0_lru_conv_w, l0_lru_conv_b, l0_lru_w_a, l0_lru_b_a, l0_lru_w_x, l0_lru_b_x, l0_lru_lambda, l0_w_out, l0_ln1_g, l0_ln1_b, l0_router_group_w, l0_router_group_b, l0_router_expert_w, l0_router_expert_b, l0_expert_w_gate, l0_expert_w_up, l0_expert_w_down, l0_ln2_g, l0_ln2_b, l1_w_in, l1_cmp_k_pe, l1_cmp_k_w1, l1_cmp_k_w2, l1_cmp_v_pe, l1_cmp_v_w1, l1_cmp_v_w2, l1_w_out, l1_ln1_g, l1_ln1_b, l1_router_group_w, l1_router_group_b, l1_router_expert_w, l1_router_expert_b, l1_expert_w_gate, l1_expert_w_up, l1_expert_w_down, l1_ln2_g, l1_ln2_b):
    bsz, seq, d = x.shape
    h = x.reshape(bsz * seq, d)
    y = _even_mixer_cat(h, bsz, seq, l0_w_in, l0_conv_w, l0_lru_conv_w, l0_lru_conv_b, l0_lru_w_a,
                        l0_lru_b_a, l0_lru_w_x, l0_lru_b_x, l0_lru_lambda)
    h = _matmul_residual_ln(y, l0_w_out.astype(BF16), h, l0_ln1_g, l0_ln1_b)
    h = _hier_moe_ln(h, l0_router_group_w, l0_router_group_b, l0_router_expert_w, l0_router_expert_b,
                     l0_expert_w_gate, l0_expert_w_up, l0_expert_w_down, l0_ln2_g, l0_ln2_b)
    y = _nsa_mixer_heads(h, bsz, seq, l1_w_in, l1_cmp_k_pe, l1_cmp_k_w1, l1_cmp_k_w2, l1_cmp_v_pe,
                         l1_cmp_v_w1, l1_cmp_v_w2)
    h = _matmul_residual_ln(y, l1_w_out.astype(BF16), h, l1_ln1_g, l1_ln1_b)
    h = _hier_moe_ln(h, l1_router_group_w, l1_router_group_b, l1_router_expert_w, l1_router_expert_b,
                     l1_expert_w_gate, l1_expert_w_up, l1_expert_w_down, l1_ln2_g, l1_ln2_b)
    return h.reshape(bsz, seq, d)
```

```python
import functools

import jax
import jax.numpy as jnp
from jax import lax
from jax.experimental import pallas as pl
from jax.experimental.pallas import tpu as pltpu

F32 = jnp.float32
BF16 = jnp.bfloat16
I32 = jnp.int32

DEPTH = 2
DN_ALPHA = (2 * DEPTH) ** 0.25
LN_EPS = 1e-5
CONV_WIDTH = 512
CONV_TAPS = 3
LRU_WIDTH = 512
LRU_CONV_TAPS = 4
LRU_C = 8.0
N_HEADS = 16
HEAD_DIM = 64
N_KV_GROUPS = 4
HEADS_PER_GROUP = N_HEADS // N_KV_GROUPS
KV_WIDTH = N_KV_GROUPS * HEAD_DIM
CMP_STRIDE = 16
CMP_BLOCK = 32
SEL_BLOCK = 64
N_SEL = 16
WINDOW = 512
ROPE_THETA = 10000.0
FORCE = 1e4
NEG = -1e30
TINY = 1e-30
QK_SCALE = HEAD_DIM ** -0.5 * 1.4426950408889634
N_GROUPS = 4
EXPERTS_PER_GROUP = 8
N_EXPERTS = N_GROUPS * EXPERTS_PER_GROUP
TOP_K = 2
MOE_BLOCK = 256

LANES = 128
VMEM_LIMIT_BYTES = 48 * 1024 * 1024
ROW_TILE = 512
SCAN_CHUNK = 256
ROUTE_TILE = 512
GATHER_TILE = 256
Q_TILE = 256
SEL_KEY_TILE = 512
V_PAD = 16


def _params(*sem):
    return pltpu.CompilerParams(dimension_semantics=sem, vmem_limit_bytes=VMEM_LIMIT_BYTES)


def _layernorm(v, g, b):
    mu = jnp.mean(v, axis=-1, keepdims=True)
    d = v - mu
    var = jnp.mean(d * d, axis=-1, keepdims=True)
    return d * lax.rsqrt(var + LN_EPS) * g + b


def _gelu_tanh(x):
    return 0.5 * x * (1.0 + jnp.tanh(0.7978845608028654 * (x + 0.044715 * (x * x * x))))


def _nt_dot(a, b):
    return lax.dot_general(a, b, (((1,), (1,)), ((), ())), preferred_element_type=F32)


def _mm_res_ln_kernel(y_ref, w_ref, x_ref, g_ref, b_ref, o_ref):
    acc = jnp.dot(y_ref[...].astype(BF16), w_ref[...], preferred_element_type=F32)
    o_ref[...] = _layernorm(DN_ALPHA * x_ref[...] + acc, g_ref[...], b_ref[...])


def _matmul_residual_ln(y, w, x, g, b):
    m, k = y.shape
    d = w.shape[1]
    tm = min(ROW_TILE, m)
    return pl.pallas_call(
        _mm_res_ln_kernel,
        grid=(m // tm,),
        in_specs=[pl.BlockSpec((tm, k), lambda i: (i, 0)),
                  pl.BlockSpec((k, d), lambda i: (0, 0)),
                  pl.BlockSpec((tm, d), lambda i: (i, 0)),
                  pl.BlockSpec((1, d), lambda i: (0, 0)),
                  pl.BlockSpec((1, d), lambda i: (0, 0))],
        out_specs=pl.BlockSpec((tm, d), lambda i: (i, 0)),
        out_shape=jax.ShapeDtypeStruct((m, d), F32),
        compiler_params=_params("arbitrary"),
    )(y, w, x, g.reshape(1, d), b.reshape(1, d))


def _shift_rows(cur, tail, d, row8):
    rolled = pltpu.roll(cur, d, 0)
    head = jnp.where(row8 < d, pltpu.roll(tail, d, 0), rolled[:8])
    return jnp.concatenate([head, rolled[8:]], axis=0)


def _l0_mixer_kernel(x_ref, win_ref, cw_ref, lw_ref, lb_ref, wa_ref, ba_ref, wx_ref, bx_ref, lam_ref,
                     o_ref, tail_c, tail_x, h_state):
    W = CONV_WIDTH
    tc = x_ref.shape[0]
    z = jnp.dot(x_ref[...].astype(BF16), win_ref[...], preferred_element_type=F32)

    @pl.when(pl.program_id(1) == 0)
    def _():
        tail_c[...] = jnp.zeros_like(tail_c)
        tail_x[...] = jnp.zeros_like(tail_x)
        h_state[...] = jnp.zeros_like(h_state)

    row8 = lax.broadcasted_iota(I32, (8, W), 0)
    row = lax.broadcasted_iota(I32, (tc, W), 0)

    ch = z[:, W:2 * W] * z[:, 2 * W:3 * W]
    tc_prev = tail_c[...]
    conv = ch * cw_ref[CONV_TAPS - 1:CONV_TAPS, :]
    for d in range(1, CONV_TAPS):
        conv = conv + _shift_rows(ch, tc_prev, d, row8) * cw_ref[CONV_TAPS - 1 - d:CONV_TAPS - d, :]
    tail_c[...] = ch[tc - 8:, :]
    o_ref[:, :W] = (z[:, :W] * conv).astype(o_ref.dtype)

    xl = z[:, 4 * W:5 * W]
    tx_prev = tail_x[...]
    xc = xl * lw_ref[LRU_CONV_TAPS - 1:LRU_CONV_TAPS, :]
    for d in range(1, LRU_CONV_TAPS):
        xc = xc + _shift_rows(xl, tx_prev, d, row8) * lw_ref[LRU_CONV_TAPS - 1 - d:LRU_CONV_TAPS - d, :]
    xc = xc + lb_ref[...]
    tail_x[...] = xl[tc - 8:, :]

    xcb = xc.astype(BF16)
    r = jax.nn.sigmoid(jnp.dot(xcb, wa_ref[...], preferred_element_type=F32) + ba_ref[...])
    ig = jax.nn.sigmoid(jnp.dot(xcb, wx_ref[...], preferred_element_type=F32) + bx_ref[...])
    nl = -lam_ref[...]
    softplus = jnp.maximum(nl, 0.0) + jnp.log1p(jnp.exp(-jnp.abs(nl)))
    log_a = -LRU_C * r * softplus
    a = jnp.exp(log_a)
    mult = jnp.sqrt(jnp.maximum(1.0 - a * a, 0.0))
    u = mult * ig * xc

    d = 1
    while d < tc:
        keep = row >= d
        a_s = jnp.where(keep, pltpu.roll(a, d, 0), 1.0)
        u_s = jnp.where(keep, pltpu.roll(u, d, 0), 0.0)
        u = a * u_s + u
        a = a * a_s
        d *= 2
    h = a * h_state[0:1, :] + u
    h_state[...] = jnp.broadcast_to(h[tc - 1:tc, :], h_state.shape)
    o_ref[:, W:] = (h * _gelu_tanh(z[:, 3 * W:4 * W])).astype(o_ref.dtype)


def _l0_mixer(x, w_in, bsz, seq, conv_w, lru_conv_w, lru_conv_b, wa_bd, b_a, wx_bd, b_x, lam):
    W = CONV_WIDTH
    tc = min(SCAN_CHUNK, seq)
    nchunk = seq // tc
    vec = lambda i, j: (0, 0)
    return pl.pallas_call(
        _l0_mixer_kernel,
        grid=(bsz, nchunk),
        in_specs=[pl.BlockSpec((tc, x.shape[1]), lambda i, j: (i * nchunk + j, 0)),
                  pl.BlockSpec(w_in.shape, vec),
                  pl.BlockSpec((CONV_TAPS, W), vec),
                  pl.BlockSpec((LRU_CONV_TAPS, W), vec),
                  pl.BlockSpec((1, W), vec),
                  pl.BlockSpec((W, W), vec),
                  pl.BlockSpec((1, W), vec),
                  pl.BlockSpec((W, W), vec),
                  pl.BlockSpec((1, W), vec),
                  pl.BlockSpec((1, W), vec)],
        out_specs=pl.BlockSpec((tc, 2 * W), lambda i, j: (i * nchunk + j, 0)),
        out_shape=jax.ShapeDtypeStruct((bsz * seq, 2 * W), BF16),
        scratch_shapes=[pltpu.VMEM((8, W), F32), pltpu.VMEM((8, W), F32), pltpu.VMEM((8, W), F32)],
        compiler_params=_params("arbitrary", "arbitrary"),
    )(x, w_in, conv_w, lru_conv_w, lru_conv_b.reshape(1, W), wa_bd, b_a.reshape(1, W), wx_bd,
      b_x.reshape(1, W), lam.reshape(1, W))


def _split_bf16(v):
    hi = v.astype(BF16)
    lo = (v - hi.astype(F32)).astype(BF16)
    return hi, lo


def _route_kernel(x_ref, wh_ref, wl_ref, b_ref, info_ref, cnt_ref):
    tm = x_ref.shape[0]

    @pl.when(pl.program_id(0) == 0)
    def _():
        cnt_ref[...] = jnp.zeros_like(cnt_ref)

    xh, xl = _split_bf16(x_ref[...])
    logits = (jnp.dot(xh, wh_ref[...], preferred_element_type=F32)
              + jnp.dot(xl, wh_ref[...], preferred_element_type=F32)
              + jnp.dot(xh, wl_ref[...], preferred_element_type=F32)) + b_ref[...]
    lane = lax.broadcasted_iota(I32, logits.shape, 1).astype(F32)
    ninf = -jnp.inf

    is_g = lane < N_GROUPS
    gmax = jnp.max(jnp.where(is_g, logits, ninf), axis=-1, keepdims=True)
    g_star = jnp.min(jnp.where(is_g & (logits == gmax), lane, float(LANES)), axis=-1, keepdims=True)
    gsum = jnp.sum(jnp.where(is_g, jnp.exp(logits - gmax), 0.0), axis=-1, keepdims=True)
    p_grp = 1.0 / gsum

    lo_lane = N_GROUPS + EXPERTS_PER_GROUP * g_star
    is_e = (lane >= lo_lane) & (lane < lo_lane + EXPERTS_PER_GROUP)
    emax = jnp.max(jnp.where(is_e, logits, ninf), axis=-1, keepdims=True)
    ex = jnp.where(is_e, jnp.exp(logits - emax), 0.0)
    pe = ex / jnp.sum(ex, axis=-1, keepdims=True)
    pe_m = jnp.where(is_e, pe, ninf)
    v1 = jnp.max(pe_m, axis=-1, keepdims=True)
    l1 = jnp.min(jnp.where(pe_m == v1, lane, float(LANES)), axis=-1, keepdims=True)
    pe_m2 = jnp.where(lane == l1, ninf, pe_m)
    v2 = jnp.max(pe_m2, axis=-1, keepdims=True)
    l2 = jnp.min(jnp.where(pe_m2 == v2, lane, float(LANES)), axis=-1, keepdims=True)
    vs = v1 + v2
    w1 = p_grp * v1 / vs
    w2 = p_grp * v2 / vs

    hit1 = lane == l1
    hit2 = lane == l2
    onehot = jnp.where(hit1 | hit2, 1.0, 0.0)
    r_i = lax.broadcasted_iota(I32, (tm, tm), 0)
    c_i = lax.broadcasted_iota(I32, (tm, tm), 1)
    tri = jnp.where(c_i < r_i, 1.0, 0.0).astype(BF16)
    before = jnp.dot(tri, onehot.astype(BF16), preferred_element_type=F32) + cnt_ref[0:1, :]
    rank1 = jnp.sum(jnp.where(hit1, before, 0.0), axis=-1, keepdims=True)
    rank2 = jnp.sum(jnp.where(hit2, before, 0.0), axis=-1, keepdims=True)
    cnt_ref[...] = cnt_ref[...] + jnp.sum(onehot, axis=0, keepdims=True)

    e1 = l1 - N_GROUPS
    e2 = l2 - N_GROUPS
    info = jnp.where(lane == 0, e1, jnp.where(lane == 1, e2, jnp.where(lane == 2, w1, jnp.where(
        lane == 3, w2, jnp.where(lane == 4, rank1, jnp.where(lane == 5, rank2, 0.0))))))
    info_ref[...] = info


def _moe_route(x, w_rg, b_rg, w_re, b_re):
    t, d = x.shape
    tm = min(ROUTE_TILE, t)
    npad = LANES - N_GROUPS - N_EXPERTS
    w = jnp.concatenate([w_rg, w_re, jnp.zeros((d, npad), F32)], axis=1)
    wh = w.astype(BF16)
    wl = (w - wh.astype(F32)).astype(BF16)
    b = jnp.concatenate([b_rg, b_re, jnp.zeros((npad,), F32)]).reshape(1, LANES)
    return pl.pallas_call(
        _route_kernel,
        grid=(t // tm,),
        in_specs=[pl.BlockSpec((tm, d), lambda i: (i, 0)),
                  pl.BlockSpec((d, LANES), lambda i: (0, 0)),
                  pl.BlockSpec((d, LANES), lambda i: (0, 0)),
                  pl.BlockSpec((1, LANES), lambda i: (0, 0))],
        out_specs=[pl.BlockSpec((tm, LANES), lambda i: (i, 0)),
                   pl.BlockSpec((8, LANES), lambda i: (0, 0))],
        out_shape=[jax.ShapeDtypeStruct((t, LANES), F32), jax.ShapeDtypeStruct((8, LANES), F32)],
        compiler_params=_params("arbitrary"),
    )(x, wh, wl, b)


def _row_copy(src_ref, src_row, dst_ref, dst_row, sem):
    return pltpu.make_async_copy(src_ref.at[pl.ds(src_row, 1), :], dst_ref.at[pl.ds(dst_row, 1), :], sem)


def _dispatch_kernel(dest_ref, x_ref, xs_in_ref, xs_ref, sem):
    del xs_in_ref
    tt = x_ref.shape[0]

    def issue(i, c):
        for k in range(TOP_K):
            _row_copy(x_ref, i, xs_ref, dest_ref[0, 0, TOP_K * i + k], sem).start(priority=k)
        return c

    lax.fori_loop(0, tt, issue, 0, unroll=8)
    for _ in range(TOP_K):
        pltpu.make_async_copy(x_ref, xs_ref.at[pl.ds(0, tt), :], sem).wait()


def _moe_dispatch(x, dest3, n_rows):
    t, d = x.shape
    tt = dest3.shape[2] // TOP_K
    xs0 = jnp.zeros((n_rows, d), F32)
    return pl.pallas_call(
        _dispatch_kernel,
        grid=(t // tt,),
        in_specs=[pl.BlockSpec((1, 1, TOP_K * tt), lambda i: (i, 0, 0), memory_space=pltpu.SMEM),
                  pl.BlockSpec((tt, d), lambda i: (i, 0)),
                  pl.BlockSpec(memory_space=pl.ANY)],
        out_specs=pl.BlockSpec(memory_space=pl.ANY),
        out_shape=jax.ShapeDtypeStruct((n_rows, d), F32),
        scratch_shapes=[pltpu.SemaphoreType.DMA(())],
        input_output_aliases={2: 0},
        compiler_params=_params("arbitrary"),
    )(dest3, x, xs0)


def _expert_kernel(be_ref, xs_ref, wg_ref, wu_ref, wd_ref, ys_ref, wg_bf, wu_bf, wd_bf):
    i = pl.program_id(0)

    @pl.when((i == 0) | (be_ref[i] != be_ref[jnp.maximum(i - 1, 0)]))
    def _():
        wg_bf[...] = wg_ref[0].astype(BF16)
        wu_bf[...] = wu_ref[0].astype(BF16)
        wd_bf[...] = wd_ref[0].astype(BF16)

    xb = xs_ref[...].astype(BF16)
    gate = jnp.dot(xb, wg_bf[...], preferred_element_type=F32)
    up = jnp.dot(xb, wu_bf[...], preferred_element_type=F32)
    h = gate * jax.nn.sigmoid(gate) * up
    ys_ref[...] = jnp.dot(h.astype(BF16), wd_bf[...], preferred_element_type=F32)


def _moe_experts(xs, block_e, wg, wu, wd):
    p, d = xs.shape
    ff = wg.shape[2]
    nb = p // MOE_BLOCK
    return pl.pallas_call(
        _expert_kernel,
        grid_spec=pltpu.PrefetchScalarGridSpec(
            num_scalar_prefetch=1, grid=(nb,),
            in_specs=[pl.BlockSpec((MOE_BLOCK, d), lambda i, be: (i, 0)),
                      pl.BlockSpec((1, d, ff), lambda i, be: (be[i], 0, 0)),
                      pl.BlockSpec((1, d, ff), lambda i, be: (be[i], 0, 0)),
                      pl.BlockSpec((1, ff, d), lambda i, be: (be[i], 0, 0))],
            out_specs=pl.BlockSpec((MOE_BLOCK, d), lambda i, be: (i, 0)),
            scratch_shapes=[pltpu.VMEM((d, ff), BF16), pltpu.VMEM((d, ff), BF16), pltpu.VMEM((ff, d), BF16)]),
        out_shape=jax.ShapeDtypeStruct((p, d), F32),
        compiler_params=_params("arbitrary"),
    )(block_e, xs, wg, wu, wd)


def _combine_ln_kernel(dest_ref, x_ref, info_ref, g_ref, b_ref, ys_ref, o_ref, buf, sem):
    tt = x_ref.shape[0]

    def issue(i, c):
        for k in range(TOP_K):
            _row_copy(ys_ref, dest_ref[0, 0, TOP_K * i + k], buf, k * tt + i, sem).start(priority=k)
        return c

    lax.fori_loop(0, tt, issue, 0, unroll=8)
    pltpu.make_async_copy(ys_ref.at[pl.ds(0, TOP_K * tt), :], buf, sem).wait()
    y = info_ref[:, 2:3] * buf[0:tt, :] + info_ref[:, 3:4] * buf[tt:2 * tt, :]
    o_ref[...] = _layernorm(DN_ALPHA * x_ref[...] + y, g_ref[...], b_ref[...])


def _moe_combine_ln(x, info, dest3, ys, g, b):
    t, d = x.shape
    tt = dest3.shape[2] // TOP_K
    return pl.pallas_call(
        _combine_ln_kernel,
        grid=(t // tt,),
        in_specs=[pl.BlockSpec((1, 1, TOP_K * tt), lambda i: (i, 0, 0), memory_space=pltpu.SMEM),
                  pl.BlockSpec((tt, d), lambda i: (i, 0)),
                  pl.BlockSpec((tt, LANES), lambda i: (i, 0)),
                  pl.BlockSpec((1, d), lambda i: (0, 0)),
                  pl.BlockSpec((1, d), lambda i: (0, 0)),
                  pl.BlockSpec(memory_space=pl.ANY)],
        out_specs=pl.BlockSpec((tt, d), lambda i: (i, 0)),
        out_shape=jax.ShapeDtypeStruct((t, d), F32),
        scratch_shapes=[pltpu.VMEM((TOP_K * tt, d), F32), pltpu.SemaphoreType.DMA(())],
        compiler_params=_params("arbitrary"),
    )(dest3, x, info, g.reshape(1, d), b.reshape(1, d), ys)


def _hier_moe_ln(x, w_rg, b_rg, w_re, b_re, w_gate, w_up, w_down, ln_g, ln_b):
    t, d = x.shape
    a_total = t * TOP_K
    info, cnt = _moe_route(x, w_rg, b_rg, w_re, b_re)
    counts = cnt[0, N_GROUPS:N_GROUPS + N_EXPERTS].astype(I32)
    padded = (counts + MOE_BLOCK - 1) // MOE_BLOCK * MOE_BLOCK
    pad_end = jnp.cumsum(padded)
    pad_start = pad_end - padded
    n_blocks = -(-a_total // MOE_BLOCK) + N_EXPERTS
    e = info[:, 0:TOP_K].astype(I32)
    rank = info[:, 4:4 + TOP_K].astype(I32)
    dest = pad_start[e] + rank
    blk_start = jnp.arange(n_blocks, dtype=I32) * MOE_BLOCK
    block_e = jnp.minimum(jnp.sum((pad_end[None, :] <= blk_start[:, None]).astype(I32), axis=1),
                          N_EXPERTS - 1).astype(I32)
    tt = min(GATHER_TILE, t)
    dest3 = dest.reshape(t // tt, 1, TOP_K * tt)
    xs = _moe_dispatch(x, dest3, n_blocks * MOE_BLOCK)
    ys = _moe_experts(xs, block_e, w_gate, w_up, w_down)
    return _moe_combine_ln(x, info, dest3, ys, ln_g, ln_b)


def _compress_kernel(ch_ref, pe_ref, w1_ref, w2_ref, o_ref):
    half = CMP_STRIDE * HEAD_DIM
    c = ch_ref[0, 0]
    n = c.shape[0]
    first = jnp.dot((c + pe_ref[0:1, :]).astype(BF16), w1_ref[:half, :], preferred_element_type=F32)
    second = jnp.dot((c + pe_ref[1:2, :]).astype(BF16), w1_ref[half:, :], preferred_element_type=F32)
    h = first + pltpu.roll(second, n - 1, 0)
    o_ref[0, 0] = jnp.dot(_gelu_tanh(h).astype(BF16), w2_ref[...],
                          preferred_element_type=F32).astype(o_ref.dtype)


def _compress(ch, pe, w1, w2):
    bsz, g, n, half = ch.shape
    hid = w1.shape[1]
    return pl.pallas_call(
        _compress_kernel,
        grid=(bsz, g),
        in_specs=[pl.BlockSpec((1, 1, n, half), lambda b, j: (b, j, 0, 0)),
                  pl.BlockSpec((2, half), lambda b, j: (0, 0)),
                  pl.BlockSpec((2 * half, hid), lambda b, j: (0, 0)),
                  pl.BlockSpec((hid, HEAD_DIM), lambda b, j: (0, 0))],
        out_specs=pl.BlockSpec((1, 1, n, HEAD_DIM), lambda b, j: (b, j, 0, 0)),
        out_shape=jax.ShapeDtypeStruct((bsz, g, n, HEAD_DIM), BF16),
        compiler_params=_params("arbitrary", "arbitrary"),
    )(ch, pe.reshape(2, half), w1.astype(BF16), w2.astype(BF16))


def _softmax_step(carry, s_t, v_t):
    m, acc = carry
    m_new = jnp.maximum(m, jnp.max(s_t, axis=0, keepdims=True))
    p = jnp.exp2(s_t - m_new).astype(BF16)
    acc = jnp.exp2(m - m_new) * acc + jnp.dot(v_t, p, preferred_element_type=F32)
    return m_new, acc


def _softmax_finish(acc):
    return acc[:HEAD_DIM] / jnp.maximum(acc[HEAD_DIM:HEAD_DIM + 1], TINY)


def _tile_heads(a):
    return jnp.concatenate([a] * HEADS_PER_GROUP, axis=1)


def _nsa_kernel(q_ref, kc_ref, vc_ref, ks_ref, vs_ref, kw_ref, vw_ref, gl_ref, ov_ref, o_ref):
    R = HEADS_PER_GROUP
    dk = HEAD_DIM
    tq = q_ref.shape[4] // R
    ncp = kc_ref.shape[2]
    nsl = ov_ref.shape[0]
    seq = kw_ref.shape[2]
    n_top = min(N_SEL, nsl)
    qi = pl.program_id(2)
    qs = qi * tq
    q_t = q_ref[0, 0, 0]
    t_lane = qs + lax.broadcasted_iota(I32, (1, tq), 1)

    init = (jnp.full((1, R * tq), NEG, F32), jnp.zeros((dk + V_PAD, R * tq), F32))

    nw = WINDOW + tq
    w0 = pl.multiple_of(jnp.maximum(qs - WINDOW, 0), tq)
    wpos = w0 + lax.broadcasted_iota(I32, (nw, tq), 0)
    bias_w = jnp.where((wpos <= t_lane) & (wpos > t_lane - WINDOW), 0.0, NEG)
    s = jnp.dot(kw_ref[0, 0, pl.ds(w0, nw), :], q_t, preferred_element_type=F32) + _tile_heads(bias_w)
    c0 = w0 // tq
    vw_slab = jnp.concatenate([vw_ref[0, 0, c0 + i] for i in range(nw // tq)], axis=1)
    o_w = _softmax_finish(_softmax_step(init, s, vw_slab)[1])

    c_row = lax.broadcasted_iota(I32, (ncp, tq), 0)
    bias_c = jnp.where(c_row * CMP_STRIDE + (CMP_BLOCK - 1) <= t_lane, 0.0, NEG)
    s_t = jnp.dot(kc_ref[0, 0], q_t, preferred_element_type=F32) + _tile_heads(bias_c)
    p = jnp.exp2(s_t - jnp.max(s_t, axis=0, keepdims=True))
    inv_l = 1.0 / jnp.maximum(jnp.sum(p, axis=0, keepdims=True), TINY)
    has_c = jnp.where(t_lane >= CMP_BLOCK - 1, 1.0, 0.0)
    inv_l = inv_l * _tile_heads(has_c)
    o_c = jnp.dot(vc_ref[0, 0], p.astype(BF16), preferred_element_type=F32) * inv_l
    pn = p * inv_l
    p_sum = pn[:, 0:tq]
    for r in range(1, R):
        p_sum = p_sum + pn[:, r * tq:(r + 1) * tq]

    ph, plo = _split_bf16(p_sum)
    imp = (jnp.dot(ov_ref[...], ph, preferred_element_type=F32)
           + jnp.dot(ov_ref[...], plo, preferred_element_type=F32))
    j_row = lax.broadcasted_iota(I32, (nsl, tq), 0).astype(F32)
    bt = (t_lane >> (SEL_BLOCK.bit_length() - 1)).astype(F32)
    forced = (j_row == 0) | (j_row == bt) | (j_row == bt - 1)
    future = j_row > bt
    work = jnp.where(forced, FORCE, jnp.where(future, -FORCE, imp))
    picked = jnp.zeros((nsl, tq), F32)
    for _ in range(n_top):
        mx = jnp.max(work, axis=0, keepdims=True)
        first = jnp.min(jnp.where(work == mx, j_row, float(nsl)), axis=0, keepdims=True)
        hit = j_row == first
        picked = jnp.where(hit, 1.0, picked)
        work = jnp.where(hit, -jnp.inf, work)
    sel_bias = jnp.where((picked > 0.5) & jnp.logical_not(future), 0.0, NEG).astype(BF16)
    q_aug = jnp.concatenate([q_t, _tile_heads(sel_bias)], axis=0)

    tk = min(SEL_KEY_TILE, seq)

    def sel_scores(kt):
        k0 = pl.multiple_of(kt * tk, tk)
        s = jnp.dot(ks_ref[0, 0, pl.ds(k0, tk), :], q_aug, preferred_element_type=F32)
        return s, vs_ref[0, 0, kt], k0

    def sel_step(kt, carry):
        s, v_t, _ = sel_scores(kt)
        return _softmax_step(carry, s, v_t)

    def sel_pair(i, carry):
        return sel_step(2 * i + 1, sel_step(2 * i, carry))

    n_full = qs // tk
    carry = lax.fori_loop(0, n_full // 2, sel_pair, init)
    carry = lax.cond(n_full % 2 == 1, lambda c: sel_step(n_full - 1, c), lambda c: c, carry)
    s, v_t, k0 = sel_scores(n_full)
    kpos = k0 + lax.broadcasted_iota(I32, (tk, tq), 0)
    s = s + _tile_heads(jnp.where(kpos <= t_lane, 0.0, NEG))
    o_s = _softmax_finish(_softmax_step(carry, s, v_t)[1])

    gates = jax.nn.sigmoid(gl_ref[0, 0, 0])
    o = gates[0:1, :] * o_c + gates[1:2, :] * o_s + gates[2:3, :] * o_w
    o_ref[0, 0, 0] = o.astype(o_ref.dtype)


def _nsa_attention(q_t, kcmp, vcmp_t, ks_aug, vs_t, kw, vw_t, gl_t):
    bsz, g, nq, dk, rq = q_t.shape
    seq = kw.shape[2]
    ncp = kcmp.shape[2]
    nsl = seq // SEL_BLOCK
    assert seq >= WINDOW + rq // HEADS_PER_GROUP and seq % min(SEL_KEY_TILE, seq) == 0
    c_start = jnp.arange(ncp) * CMP_STRIDE
    j_start = jnp.arange(nsl) * SEL_BLOCK
    overlap_t = ((c_start[None, :] < j_start[:, None] + SEL_BLOCK)
                 & (c_start[None, :] + CMP_BLOCK > j_start[:, None])).astype(BF16)
    per_bg = lambda shape: pl.BlockSpec((1, 1) + shape, lambda b, j, i: (b, j) + (0,) * len(shape))
    per_q = lambda shape: pl.BlockSpec((1, 1, 1) + shape, lambda b, j, i: (b, j, i, 0, 0))
    tk = min(SEL_KEY_TILE, seq)
    tq = rq // HEADS_PER_GROUP
    return pl.pallas_call(
        _nsa_kernel,
        grid=(bsz, g, nq),
        in_specs=[per_q((dk, rq)),
                  per_bg((ncp, dk)), per_bg((dk, ncp)),
                  per_bg((seq, dk + nsl)), per_bg((seq // tk, dk + V_PAD, tk)),
                  per_bg((seq, dk)), per_bg((seq // tq, dk + V_PAD, tq)),
                  per_q((3, rq)),
                  pl.BlockSpec((nsl, ncp), lambda b, j, i: (0, 0))],
        out_specs=per_q((dk, rq)),
        out_shape=jax.ShapeDtypeStruct((bsz, g, nq, dk, rq), BF16),
        compiler_params=_params("arbitrary", "arbitrary", "arbitrary"),
    )(q_t, kcmp, vcmp_t, ks_aug, vs_t, kw, vw_t, gl_t, overlap_t)


def _rope_rows(v, cos, sin):
    w = v.shape[1]
    lane = lax.broadcasted_iota(I32, v.shape, 1)
    first_half = (lane & (HEAD_DIM // 2)) == 0
    partner = jnp.where(first_half, pltpu.roll(v, w - HEAD_DIM // 2, 1), pltpu.roll(v, HEAD_DIM // 2, 1))
    return v * cos + partner * sin


def _nsa_proj_kernel(x_ref, wq_ref, wk_ref, wv_ref, wc_ref, cq_ref, sq_ref, ck_ref, sk_ref,
                     q_ref, ks_ref, kw_ref, kc_ref, vcg_ref, vs_ref, vw_ref, *, row_tiles_per_seq):
    G, R, dk = N_KV_GROUPS, HEADS_PER_GROUP, HEAD_DIM
    tm = x_ref.shape[0]
    tq = q_ref.shape[4] // R
    tk = vs_ref.shape[4]
    nsl = ks_ref.shape[3] - dk
    xb = x_ref[...].astype(BF16)

    q_t = _nt_dot(wq_ref[...], xb).reshape(N_HEADS, dk, tm)
    partner = jnp.concatenate([q_t[:, dk // 2:], q_t[:, :dk // 2]], axis=1)
    q_t = (q_t * cq_ref[...][None] + partner * sq_ref[...][None]).astype(BF16)
    for g in range(G):
        for h in range(tm // tq):
            q_ref[0, g, h] = jnp.concatenate(
                [q_t[g * R + r][:, h * tq:(h + 1) * tq] for r in range(R)], axis=1)

    kk = jnp.dot(xb, wk_ref[...], preferred_element_type=F32)
    cos_k, sin_k = ck_ref[...], sk_ref[...]
    k_sel = _rope_rows(kk[:, :KV_WIDTH], cos_k, sin_k)
    k_win = _rope_rows(kk[:, KV_WIDTH:2 * KV_WIDTH], cos_k, sin_k)
    kc_ref[...] = _rope_rows(kk[:, 2 * KV_WIDTH:], cos_k, sin_k)
    pos = (pl.program_id(0) % row_tiles_per_seq) * tm + lax.broadcasted_iota(I32, (tm, nsl), 0)
    block_onehot = jnp.where((pos >> (SEL_BLOCK.bit_length() - 1))
                             == lax.broadcasted_iota(I32, (tm, nsl), 1), 1.0, 0.0)
    for g in range(G):
        ks_ref[0, g] = jnp.concatenate([k_sel[:, g * dk:(g + 1) * dk], block_onehot],
                                       axis=1).astype(BF16)
        kw_ref[0, g] = k_win[:, g * dk:(g + 1) * dk].astype(BF16)

    v_t = _nt_dot(wv_ref[...], xb)
    extra = jnp.where(lax.broadcasted_iota(I32, (V_PAD, tm), 0) == 0, 1.0, 0.0)
    for g in range(G):
        vs = jnp.concatenate([v_t[g * dk:(g + 1) * dk], extra], axis=0).astype(BF16)
        for c in range(tm // tk):
            vs_ref[0, g, c] = vs[:, c * tk:(c + 1) * tk]
        vw = jnp.concatenate([v_t[KV_WIDTH + g * dk:KV_WIDTH + (g + 1) * dk], extra],
                             axis=0).astype(BF16)
        for h in range(tm // tq):
            vw_ref[0, g, h] = vw[:, h * tq:(h + 1) * tq]

    vcg_ref[...] = jnp.dot(xb, wc_ref[...], preferred_element_type=F32)


def _nsa_project(x, bsz, seq, w_in):
    G, R, dk = N_KV_GROUPS, HEADS_PER_GROUP, HEAD_DIM
    t, d = x.shape
    qd = N_HEADS * dk
    tm = min(ROW_TILE, seq)
    tq = min(Q_TILE, seq)
    tk = min(SEL_KEY_TILE, seq)
    nsl = seq // SEL_BLOCK
    ns = seq // tm
    assert tm % tq == 0 and tm % tk == 0
    cut = [qd + j * KV_WIDTH for j in range(7)]
    w = w_in.astype(BF16)
    w_q, w_kc, w_vc, w_ks, w_vs, w_kw, w_vw, w_gl = (
        w[:, :qd], w[:, cut[0]:cut[1]], w[:, cut[1]:cut[2]], w[:, cut[2]:cut[3]],
        w[:, cut[3]:cut[4]], w[:, cut[4]:cut[5]], w[:, cut[5]:cut[6]], w[:, cut[6]:])
    n_gl = w_gl.shape[1]
    wq_t = w_q.T
    wk = jnp.concatenate([w_ks, w_kw, w_kc], axis=1)
    wv_t = jnp.concatenate([w_vs, w_vw], axis=1).T
    wc = jnp.concatenate([w_vc, w_gl, jnp.zeros((d, LANES - n_gl), BF16)], axis=1)
    pos = jnp.arange(seq, dtype=F32)
    inv = ROPE_THETA ** (-jnp.arange(0, dk, 2, dtype=F32) / dk)
    ang = pos[:, None] * inv[None, :]
    cos, sin = jnp.cos(ang), jnp.sin(ang)
    cos_h = jnp.concatenate([cos, cos], axis=1)
    sin_h = jnp.concatenate([-sin, sin], axis=1)
    cq, sq = (cos_h * QK_SCALE).T, (sin_h * QK_SCALE).T
    ck, sk = jnp.tile(cos_h, (1, G)), jnp.tile(sin_h, (1, G))
    full = lambda a: pl.BlockSpec(a.shape, lambda i: (0,) * a.ndim)
    b_of = lambda i: i // ns
    s_of = lambda i: i % ns
    out_shape = [jax.ShapeDtypeStruct((bsz, G, seq // tq, dk, R * tq), BF16),
                 jax.ShapeDtypeStruct((bsz, G, seq, dk + nsl), BF16),
                 jax.ShapeDtypeStruct((bsz, G, seq, dk), BF16),
                 jax.ShapeDtypeStruct((t, KV_WIDTH), F32),
                 jax.ShapeDtypeStruct((t, KV_WIDTH + LANES), F32),
                 jax.ShapeDtypeStruct((bsz, G, seq // tk, dk + V_PAD, tk), BF16),
                 jax.ShapeDtypeStruct((bsz, G, seq // tq, dk + V_PAD, tq), BF16)]
    out_specs = [pl.BlockSpec((1, G, tm // tq, dk, R * tq), lambda i: (b_of(i), 0, s_of(i), 0, 0)),
                 pl.BlockSpec((1, G, tm, dk + nsl), lambda i: (b_of(i), 0, s_of(i), 0)),
                 pl.BlockSpec((1, G, tm, dk), lambda i: (b_of(i), 0, s_of(i), 0)),
                 pl.BlockSpec((tm, KV_WIDTH), lambda i: (i, 0)),
                 pl.BlockSpec((tm, KV_WIDTH + LANES), lambda i: (i, 0)),
                 pl.BlockSpec((1, G, tm // tk, dk + V_PAD, tk), lambda i: (b_of(i), 0, s_of(i), 0, 0)),
                 pl.BlockSpec((1, G, tm // tq, dk + V_PAD, tq), lambda i: (b_of(i), 0, s_of(i), 0, 0))]
    return pl.pallas_call(
        functools.partial(_nsa_proj_kernel, row_tiles_per_seq=ns),
        grid=(t // tm,),
        in_specs=[pl.BlockSpec((tm, d), lambda i: (i, 0)),
                  full(wq_t), full(wk), full(wv_t), full(wc),
                  pl.BlockSpec((dk, tm), lambda i: (0, s_of(i))),
                  pl.BlockSpec((dk, tm), lambda i: (0, s_of(i))),
                  pl.BlockSpec((tm, KV_WIDTH), lambda i: (s_of(i), 0)),
                  pl.BlockSpec((tm, KV_WIDTH), lambda i: (s_of(i), 0))],
        out_specs=out_specs, out_shape=out_shape,
        compiler_params=_params("arbitrary"),
    )(x, wq_t, wk, wv_t, wc, cq, sq, ck, sk)


def _nsa_mixer_heads(x, bsz, seq, w_in, cmp_k_pe, cmp_k_w1, cmp_k_w2, cmp_v_pe, cmp_v_w1, cmp_v_w2):
    G, R, dk = N_KV_GROUPS, HEADS_PER_GROUP, HEAD_DIM
    qd = N_HEADS * dk
    tq = min(Q_TILE, seq)
    nq = seq // tq
    n_gl = N_HEADS * 3
    q_t, ks_aug, kw, kc, vcg, vs_t, vw_t = _nsa_project(x, bsz, seq, w_in)

    def chunks(a):
        a = a.reshape(bsz, seq // CMP_STRIDE, CMP_STRIDE, G, dk).transpose(0, 3, 1, 2, 4)
        return a.reshape(bsz, G, seq // CMP_STRIDE, CMP_STRIDE * dk)

    gl_t = vcg[:, KV_WIDTH:KV_WIDTH + n_gl].reshape(bsz, nq, tq, G, R, 3).transpose(0, 3, 1, 5, 4, 2)
    gl_t = gl_t.reshape(bsz, G, nq, 3, R * tq)
    kcmp = _compress(chunks(kc), cmp_k_pe, cmp_k_w1, cmp_k_w2)
    vcmp_t = _compress(chunks(vcg[:, :KV_WIDTH]), cmp_v_pe, cmp_v_w1, cmp_v_w2).transpose(0, 1, 3, 2)
    o_t = _nsa_attention(q_t, kcmp, vcmp_t, ks_aug, vs_t, kw, vw_t, gl_t)
    o = o_t.reshape(bsz, G, nq, dk, R, tq).transpose(0, 2, 5, 1, 4, 3)
    return o.reshape(bsz * seq, qd)


def _even_mixer_cat(x, bsz, seq, w_in, conv_w, lru_conv_w, lru_conv_b, w_a, b_a, w_x, b_x, lam):
    wa_bd = jax.scipy.linalg.block_diag(*w_a).astype(BF16)
    wx_bd = jax.scipy.linalg.block_diag(*w_x).astype(BF16)
    return _l0_mixer(x, w_in.astype(BF16), bsz, seq, conv_w, lru_conv_w, lru_conv_b, wa_bd, b_a, wx_bd, b_x, lam)


def kernel(x, l0_w_in, l0_conv_w, l0_lru_conv_w, l0_lru_conv_b, l0_lru_w_a, l0_lru_b_a, l0_lru_w_x, l0_lru_b_x, l0_lru_lambda, l0_w_out, l0_ln1_g, l0_ln1_b, l0_router_group_w, l0_router_group_b, l0_router_expert_w, l0_router_expert_b, l0_expert_w_gate, l0_expert_w_up, l0_expert_w_down, l0_ln2_g, l0_ln2_b, l1_w_in, l1_cmp_k_pe, l1_cmp_k_w1, l1_cmp_k_w2, l1_cmp_v_pe, l1_cmp_v_w1, l1_cmp_v_w2, l1_w_out, l1_ln1_g, l1_ln1_b, l1_router_group_w, l1_router_group_b, l1_router_expert_w, l1_router_expert_b, l1_expert_w_gate, l1_expert_w_up, l1_expert_w_down, l1_ln2_g, l1_ln2_b):
    bsz, seq, d = x.shape
    h = x.reshape(bsz * seq, d)
    y = _even_mixer_cat(h, bsz, seq, l0_w_in, l0_conv_w, l0_lru_conv_w, l0_lru_conv_b, l0_lru_w_a,
                        l0_lru_b_a, l0_lru_w_x, l0_lru_b_x, l0_lru_lambda)
    h = _matmul_residual_ln(y, l0_w_out.astype(BF16), h, l0_ln1_g, l0_ln1_b)
    h = _hier_moe_ln(h, l0_router_group_w, l0_router_group_b, l0_router_expert_w, l0_router_expert_b,
                     l0_expert_w_gate, l0_expert_w_up, l0_expert_w_down, l0_ln2_g, l0_ln2_b)
    y = _nsa_mixer_heads(h, bsz, seq, l1_w_in, l1_cmp_k_pe, l1_cmp_k_w1, l1_cmp_k_w2, l1_cmp_v_pe,
                         l1_cmp_v_w1, l1_cmp_v_w2)
    h = _matmul_residual_ln(y, l1_w_out.astype(BF16), h, l1_ln1_g, l1_ln1_b)
    h = _hier_moe_ln(h, l1_router_group_w, l1_router_group_b, l1_router_expert_w, l1_router_expert_b,
                     l1_expert_w_gate, l1_expert_w_up, l1_expert_w_down, l1_ln2_g, l1_ln2_b)
    return h.reshape(bsz, seq, d)
```

```python
import functools

import jax
import jax.numpy as jnp
from jax import lax
from jax.experimental import pallas as pl
from jax.experimental.pallas import tpu as pltpu

F32 = jnp.float32
BF16 = jnp.bfloat16
I32 = jnp.int32

DEPTH = 2
DN_ALPHA = (2 * DEPTH) ** 0.25
LN_EPS = 1e-5
CONV_WIDTH = 512
CONV_TAPS = 3
LRU_WIDTH = 512
LRU_CONV_TAPS = 4
LRU_C = 8.0
N_HEADS = 16
HEAD_DIM = 64
N_KV_GROUPS = 4
HEADS_PER_GROUP = N_HEADS // N_KV_GROUPS
KV_WIDTH = N_KV_GROUPS * HEAD_DIM
CMP_STRIDE = 16
CMP_BLOCK = 32
SEL_BLOCK = 64
N_SEL = 16
WINDOW = 512
ROPE_THETA = 10000.0
FORCE = 1e4
NEG = -1e30
TINY = 1e-30
QK_SCALE = HEAD_DIM ** -0.5 * 1.4426950408889634
N_GROUPS = 4
EXPERTS_PER_GROUP = 8
N_EXPERTS = N_GROUPS * EXPERTS_PER_GROUP
TOP_K = 2
MOE_BLOCK = 256

LANES = 128
VMEM_LIMIT_BYTES = 48 * 1024 * 1024
ROW_TILE = 512
SCAN_CHUNK = 256
ROUTE_TILE = 512
GATHER_TILE = 512
Q_TILE = 512
SEL_KEY_TILE = 512
V_PAD = 16


def _params(*sem):
    return pltpu.CompilerParams(dimension_semantics=sem, vmem_limit_bytes=VMEM_LIMIT_BYTES)


def _layernorm(v, g, b):
    mu = jnp.mean(v, axis=-1, keepdims=True)
    d = v - mu
    var = jnp.mean(d * d, axis=-1, keepdims=True)
    return d * lax.rsqrt(var + LN_EPS) * g + b


def _gelu_tanh(x):
    return 0.5 * x * (1.0 + jnp.tanh(0.7978845608028654 * (x + 0.044715 * (x * x * x))))


def _nt_dot(a, b):
    return lax.dot_general(a, b, (((1,), (1,)), ((), ())), preferred_element_type=F32)


def _mm_res_ln_kernel(y_ref, w_ref, x_ref, g_ref, b_ref, o_ref):
    acc = jnp.dot(y_ref[...].astype(BF16), w_ref[...], preferred_element_type=F32)
    o_ref[...] = _layernorm(DN_ALPHA * x_ref[...] + acc, g_ref[...], b_ref[...])


def _matmul_residual_ln(y, w, x, g, b):
    m, k = y.shape
    d = w.shape[1]
    tm = min(ROW_TILE, m)
    return pl.pallas_call(
        _mm_res_ln_kernel,
        grid=(m // tm,),
        in_specs=[pl.BlockSpec((tm, k), lambda i: (i, 0)),
                  pl.BlockSpec((k, d), lambda i: (0, 0)),
                  pl.BlockSpec((tm, d), lambda i: (i, 0)),
                  pl.BlockSpec((1, d), lambda i: (0, 0)),
                  pl.BlockSpec((1, d), lambda i: (0, 0))],
        out_specs=pl.BlockSpec((tm, d), lambda i: (i, 0)),
        out_shape=jax.ShapeDtypeStruct((m, d), F32),
        compiler_params=_params("arbitrary"),
    )(y, w, x, g.reshape(1, d), b.reshape(1, d))


def _shift_rows(cur, tail, d, row8):
    rolled = pltpu.roll(cur, d, 0)
    head = jnp.where(row8 < d, pltpu.roll(tail, d, 0), rolled[:8])
    return jnp.concatenate([head, rolled[8:]], axis=0)


def _l0_mixer_kernel(x_ref, win_ref, cw_ref, lw_ref, lb_ref, wa_ref, ba_ref, wx_ref, bx_ref, lam_ref,
                     o_ref, tail_c, tail_x, h_state):
    W = CONV_WIDTH
    tc = x_ref.shape[0]
    z = jnp.dot(x_ref[...].astype(BF16), win_ref[...], preferred_element_type=F32)

    @pl.when(pl.program_id(1) == 0)
    def _():
        tail_c[...] = jnp.zeros_like(tail_c)
        tail_x[...] = jnp.zeros_like(tail_x)
        h_state[...] = jnp.zeros_like(h_state)

    row8 = lax.broadcasted_iota(I32, (8, W), 0)
    row = lax.broadcasted_iota(I32, (tc, W), 0)

    ch = z[:, W:2 * W] * z[:, 2 * W:3 * W]
    tc_prev = tail_c[...]
    conv = ch * cw_ref[CONV_TAPS - 1:CONV_TAPS, :]
    for d in range(1, CONV_TAPS):
        conv = conv + _shift_rows(ch, tc_prev, d, row8) * cw_ref[CONV_TAPS - 1 - d:CONV_TAPS - d, :]
    tail_c[...] = ch[tc - 8:, :]
    o_ref[:, :W] = (z[:, :W] * conv).astype(o_ref.dtype)

    xl = z[:, 4 * W:5 * W]
    tx_prev = tail_x[...]
    xc = xl * lw_ref[LRU_CONV_TAPS - 1:LRU_CONV_TAPS, :]
    for d in range(1, LRU_CONV_TAPS):
        xc = xc + _shift_rows(xl, tx_prev, d, row8) * lw_ref[LRU_CONV_TAPS - 1 - d:LRU_CONV_TAPS - d, :]
    xc = xc + lb_ref[...]
    tail_x[...] = xl[tc - 8:, :]

    xcb = xc.astype(BF16)
    r = jax.nn.sigmoid(jnp.dot(xcb, wa_ref[...], preferred_element_type=F32) + ba_ref[...])
    ig = jax.nn.sigmoid(jnp.dot(xcb, wx_ref[...], preferred_element_type=F32) + bx_ref[...])
    nl = -lam_ref[...]
    softplus = jnp.maximum(nl, 0.0) + jnp.log1p(jnp.exp(-jnp.abs(nl)))
    log_a = -LRU_C * r * softplus
    a = jnp.exp(log_a)
    mult = jnp.sqrt(jnp.maximum(1.0 - a * a, 0.0))
    u = mult * ig * xc

    d = 1
    while d < tc:
        keep = row >= d
        a_s = jnp.where(keep, pltpu.roll(a, d, 0), 1.0)
        u_s = jnp.where(keep, pltpu.roll(u, d, 0), 0.0)
        u = a * u_s + u
        a = a * a_s
        d *= 2
    h = a * h_state[0:1, :] + u
    h_state[...] = jnp.broadcast_to(h[tc - 1:tc, :], h_state.shape)
    o_ref[:, W:] = (h * _gelu_tanh(z[:, 3 * W:4 * W])).astype(o_ref.dtype)


def _l0_mixer(x, w_in, bsz, seq, conv_w, lru_conv_w, lru_conv_b, wa_bd, b_a, wx_bd, b_x, lam):
    W = CONV_WIDTH
    tc = min(SCAN_CHUNK, seq)
    nchunk = seq // tc
    vec = lambda i, j: (0, 0)
    return pl.pallas_call(
        _l0_mixer_kernel,
        grid=(bsz, nchunk),
        in_specs=[pl.BlockSpec((tc, x.shape[1]), lambda i, j: (i * nchunk + j, 0)),
                  pl.BlockSpec(w_in.shape, vec),
                  pl.BlockSpec((CONV_TAPS, W), vec),
                  pl.BlockSpec((LRU_CONV_TAPS, W), vec),
                  pl.BlockSpec((1, W), vec),
                  pl.BlockSpec((W, W), vec),
                  pl.BlockSpec((1, W), vec),
                  pl.BlockSpec((W, W), vec),
                  pl.BlockSpec((1, W), vec),
                  pl.BlockSpec((1, W), vec)],
        out_specs=pl.BlockSpec((tc, 2 * W), lambda i, j: (i * nchunk + j, 0)),
        out_shape=jax.ShapeDtypeStruct((bsz * seq, 2 * W), BF16),
        scratch_shapes=[pltpu.VMEM((8, W), F32), pltpu.VMEM((8, W), F32), pltpu.VMEM((8, W), F32)],
        compiler_params=_params("arbitrary", "arbitrary"),
    )(x, w_in, conv_w, lru_conv_w, lru_conv_b.reshape(1, W), wa_bd, b_a.reshape(1, W), wx_bd,
      b_x.reshape(1, W), lam.reshape(1, W))


def _split_bf16(v):
    hi = v.astype(BF16)
    lo = (v - hi.astype(F32)).astype(BF16)
    return hi, lo


def _route_kernel(x_ref, wh_ref, wl_ref, b_ref, info_ref, info_t_ref, cnt_ref):
    tm = x_ref.shape[0]

    @pl.when(pl.program_id(0) == 0)
    def _():
        cnt_ref[...] = jnp.zeros_like(cnt_ref)

    xh, xl = _split_bf16(x_ref[...])
    logits = (jnp.dot(xh, wh_ref[...], preferred_element_type=F32)
              + jnp.dot(xl, wh_ref[...], preferred_element_type=F32)
              + jnp.dot(xh, wl_ref[...], preferred_element_type=F32)) + b_ref[...]
    lane = lax.broadcasted_iota(I32, logits.shape, 1).astype(F32)
    ninf = -jnp.inf

    is_g = lane < N_GROUPS
    gmax = jnp.max(jnp.where(is_g, logits, ninf), axis=-1, keepdims=True)
    g_star = jnp.min(jnp.where(is_g & (logits == gmax), lane, float(LANES)), axis=-1, keepdims=True)
    gsum = jnp.sum(jnp.where(is_g, jnp.exp(logits - gmax), 0.0), axis=-1, keepdims=True)
    p_grp = 1.0 / gsum

    lo_lane = N_GROUPS + EXPERTS_PER_GROUP * g_star
    is_e = (lane >= lo_lane) & (lane < lo_lane + EXPERTS_PER_GROUP)
    emax = jnp.max(jnp.where(is_e, logits, ninf), axis=-1, keepdims=True)
    ex = jnp.where(is_e, jnp.exp(logits - emax), 0.0)
    pe = ex / jnp.sum(ex, axis=-1, keepdims=True)
    pe_m = jnp.where(is_e, pe, ninf)
    v1 = jnp.max(pe_m, axis=-1, keepdims=True)
    l1 = jnp.min(jnp.where(pe_m == v1, lane, float(LANES)), axis=-1, keepdims=True)
    pe_m2 = jnp.where(lane == l1, ninf, pe_m)
    v2 = jnp.max(pe_m2, axis=-1, keepdims=True)
    l2 = jnp.min(jnp.where(pe_m2 == v2, lane, float(LANES)), axis=-1, keepdims=True)
    vs = v1 + v2
    w1 = p_grp * v1 / vs
    w2 = p_grp * v2 / vs

    hit1 = lane == l1
    hit2 = lane == l2
    onehot = jnp.where(hit1 | hit2, 1.0, 0.0)
    r_i = lax.broadcasted_iota(I32, (tm, tm), 0)
    c_i = lax.broadcasted_iota(I32, (tm, tm), 1)
    tri = jnp.where(c_i < r_i, 1.0, 0.0).astype(BF16)
    before = jnp.dot(tri, onehot.astype(BF16), preferred_element_type=F32) + cnt_ref[0:1, :]
    rank1 = jnp.sum(jnp.where(hit1, before, 0.0), axis=-1, keepdims=True)
    rank2 = jnp.sum(jnp.where(hit2, before, 0.0), axis=-1, keepdims=True)
    cnt_ref[...] = cnt_ref[...] + jnp.sum(onehot, axis=0, keepdims=True)

    e1 = l1 - N_GROUPS
    e2 = l2 - N_GROUPS
    info = jnp.where(lane == 0, e1, jnp.where(lane == 1, e2, jnp.where(lane == 2, w1, jnp.where(
        lane == 3, w2, jnp.where(lane == 4, rank1, jnp.where(lane == 5, rank2, 0.0))))))
    info_ref[...] = info
    info_t_ref[...] = info.T[0:8, :]


def _moe_route(x, w_rg, b_rg, w_re, b_re):
    t, d = x.shape
    tm = min(ROUTE_TILE, t)
    npad = LANES - N_GROUPS - N_EXPERTS
    w = jnp.concatenate([w_rg, w_re, jnp.zeros((d, npad), F32)], axis=1)
    wh = w.astype(BF16)
    wl = (w - wh.astype(F32)).astype(BF16)
    b = jnp.concatenate([b_rg, b_re, jnp.zeros((npad,), F32)]).reshape(1, LANES)
    return pl.pallas_call(
        _route_kernel,
        grid=(t // tm,),
        in_specs=[pl.BlockSpec((tm, d), lambda i: (i, 0)),
                  pl.BlockSpec((d, LANES), lambda i: (0, 0)),
                  pl.BlockSpec((d, LANES), lambda i: (0, 0)),
                  pl.BlockSpec((1, LANES), lambda i: (0, 0))],
        out_specs=[pl.BlockSpec((tm, LANES), lambda i: (i, 0)),
                   pl.BlockSpec((8, tm), lambda i: (0, i)),
                   pl.BlockSpec((8, LANES), lambda i: (0, 0))],
        out_shape=[jax.ShapeDtypeStruct((t, LANES), F32), jax.ShapeDtypeStruct((8, t), F32),
                   jax.ShapeDtypeStruct((8, LANES), F32)],
        compiler_params=_params("arbitrary"),
    )(x, wh, wl, b)


def _row_copy(src_ref, src_row, dst_ref, dst_row, sem):
    return pltpu.make_async_copy(src_ref.at[pl.ds(src_row, 1), :], dst_ref.at[pl.ds(dst_row, 1), :], sem)


def _dispatch_kernel(meta_ref, dest_ref, x_ref, xs_ref, zeros, sem, zsem):
    tt = x_ref.shape[0]
    n_blocks = xs_ref.shape[0] // MOE_BLOCK

    def zero_block(row0):
        return pltpu.make_async_copy(zeros, xs_ref.at[pl.ds(row0, MOE_BLOCK), :], zsem)

    @pl.when(pl.program_id(0) == 0)
    def _():
        zeros[...] = jnp.zeros_like(zeros)
        n_used = meta_ref[N_EXPERTS]

        def over_fill_targets(fn):
            for e in range(N_EXPERTS):
                @pl.when(meta_ref[e] >= 0)
                def _():
                    fn(pl.multiple_of(meta_ref[e], MOE_BLOCK))

            def trailing(j, c):
                fn(pl.multiple_of(j * MOE_BLOCK, MOE_BLOCK))
                return c

            lax.fori_loop(n_used, n_blocks, trailing, 0)

        over_fill_targets(lambda row0: zero_block(row0).start())
        over_fill_targets(lambda row0: zero_block(row0).wait())

    def issue(i, c):
        for k in range(TOP_K):
            _row_copy(x_ref, i, xs_ref, dest_ref[0, 0, k * tt + i], sem).start(priority=k)
        return c

    lax.fori_loop(0, tt, issue, 0, unroll=8)
    for _ in range(TOP_K):
        pltpu.make_async_copy(x_ref, xs_ref.at[pl.ds(0, tt), :], sem).wait()


def _moe_dispatch(x, dest3, meta, n_rows):
    t, d = x.shape
    tt = dest3.shape[2] // TOP_K
    return pl.pallas_call(
        _dispatch_kernel,
        grid_spec=pltpu.PrefetchScalarGridSpec(
            num_scalar_prefetch=1, grid=(t // tt,),
            in_specs=[pl.BlockSpec((1, 1, TOP_K * tt), lambda i, m: (i, 0, 0), memory_space=pltpu.SMEM),
                      pl.BlockSpec((tt, d), lambda i, m: (i, 0))],
            out_specs=pl.BlockSpec(memory_space=pl.ANY),
            scratch_shapes=[pltpu.VMEM((MOE_BLOCK, d), F32), pltpu.SemaphoreType.DMA(()),
                            pltpu.SemaphoreType.DMA(())]),
        out_shape=jax.ShapeDtypeStruct((n_rows, d), F32),
        compiler_params=_params("arbitrary"),
    )(meta, dest3, x)


def _expert_kernel(be_ref, xs_ref, wg_ref, wu_ref, wd_ref, ys_ref, wg_bf, wu_bf, wd_bf):
    i = pl.program_id(0)

    @pl.when((i == 0) | (be_ref[i] != be_ref[jnp.maximum(i - 1, 0)]))
    def _():
        wg_bf[...] = wg_ref[0].astype(BF16)
        wu_bf[...] = wu_ref[0].astype(BF16)
        wd_bf[...] = wd_ref[0].astype(BF16)

    xb = xs_ref[...].astype(BF16)
    gate = jnp.dot(xb, wg_bf[...], preferred_element_type=F32)
    up = jnp.dot(xb, wu_bf[...], preferred_element_type=F32)
    h = gate * jax.nn.sigmoid(gate) * up
    ys_ref[...] = jnp.dot(h.astype(BF16), wd_bf[...], preferred_element_type=F32)


def _moe_experts(xs, block_e, wg, wu, wd):
    p, d = xs.shape
    ff = wg.shape[2]
    nb = p // MOE_BLOCK
    return pl.pallas_call(
        _expert_kernel,
        grid_spec=pltpu.PrefetchScalarGridSpec(
            num_scalar_prefetch=1, grid=(nb,),
            in_specs=[pl.BlockSpec((MOE_BLOCK, d), lambda i, be: (i, 0)),
                      pl.BlockSpec((1, d, ff), lambda i, be: (be[i], 0, 0)),
                      pl.BlockSpec((1, d, ff), lambda i, be: (be[i], 0, 0)),
                      pl.BlockSpec((1, ff, d), lambda i, be: (be[i], 0, 0))],
            out_specs=pl.BlockSpec((MOE_BLOCK, d), lambda i, be: (i, 0)),
            scratch_shapes=[pltpu.VMEM((d, ff), BF16), pltpu.VMEM((d, ff), BF16), pltpu.VMEM((ff, d), BF16)]),
        out_shape=jax.ShapeDtypeStruct((p, d), F32),
        compiler_params=_params("arbitrary"),
    )(block_e, xs, wg, wu, wd)


def _combine_ln_kernel(dest_ref, dest_next_ref, x_ref, info_ref, g_ref, b_ref, ys_ref, o_ref, buf, sem):
    tt = x_ref.shape[0]
    i = pl.program_id(0)
    slot = i % 2

    def issue_tile(d_ref, s):
        def issue(r, c):
            for k in range(TOP_K):
                _row_copy(ys_ref, d_ref[0, 0, k * tt + r], buf.at[s], k * tt + r, sem.at[s]).start(priority=k)
            return c

        lax.fori_loop(0, tt, issue, 0, unroll=8)

    @pl.when(i == 0)
    def _():
        issue_tile(dest_ref, 0)

    @pl.when(i + 1 < pl.num_programs(0))
    def _():
        issue_tile(dest_next_ref, 1 - slot)

    pltpu.make_async_copy(ys_ref.at[pl.ds(0, TOP_K * tt), :], buf.at[slot], sem.at[slot]).wait()
    rows = buf[slot]
    y = info_ref[:, 2:3] * rows[0:tt, :] + info_ref[:, 3:4] * rows[tt:2 * tt, :]
    o_ref[...] = _layernorm(DN_ALPHA * x_ref[...] + y, g_ref[...], b_ref[...])


def _moe_combine_ln(x, info, dest3, ys, g, b):
    t, d = x.shape
    tt = dest3.shape[2] // TOP_K
    n = t // tt
    return pl.pallas_call(
        _combine_ln_kernel,
        grid=(n,),
        in_specs=[pl.BlockSpec((1, 1, TOP_K * tt), lambda i: (i, 0, 0), memory_space=pltpu.SMEM),
                  pl.BlockSpec((1, 1, TOP_K * tt), lambda i: (jnp.minimum(i + 1, n - 1), 0, 0),
                               memory_space=pltpu.SMEM),
                  pl.BlockSpec((tt, d), lambda i: (i, 0)),
                  pl.BlockSpec((tt, LANES), lambda i: (i, 0)),
                  pl.BlockSpec((1, d), lambda i: (0, 0)),
                  pl.BlockSpec((1, d), lambda i: (0, 0)),
                  pl.BlockSpec(memory_space=pl.ANY)],
        out_specs=pl.BlockSpec((tt, d), lambda i: (i, 0)),
        out_shape=jax.ShapeDtypeStruct((t, d), F32),
        scratch_shapes=[pltpu.VMEM((2, TOP_K * tt, d), F32), pltpu.SemaphoreType.DMA((2,))],
        compiler_params=_params("arbitrary"),
    )(dest3, dest3, x, info, g.reshape(1, d), b.reshape(1, d), ys)


def _hier_moe_ln(x, w_rg, b_rg, w_re, b_re, w_gate, w_up, w_down, ln_g, ln_b):
    t, d = x.shape
    a_total = t * TOP_K
    info, info_t, cnt = _moe_route(x, w_rg, b_rg, w_re, b_re)
    counts = cnt[0, N_GROUPS:N_GROUPS + N_EXPERTS].astype(I32)
    padded = (counts + MOE_BLOCK - 1) // MOE_BLOCK * MOE_BLOCK
    pad_end = jnp.cumsum(padded)
    pad_start = pad_end - padded
    n_blocks = -(-a_total // MOE_BLOCK) + N_EXPERTS
    e = info_t[0:TOP_K].astype(I32)
    rank = info_t[4:4 + TOP_K].astype(I32)
    dest = pad_start[e] + rank
    blk_start = jnp.arange(n_blocks, dtype=I32) * MOE_BLOCK
    block_e = jnp.minimum(jnp.sum((pad_end[None, :] <= blk_start[:, None]).astype(I32), axis=1),
                          N_EXPERTS - 1).astype(I32)
    tt = min(GATHER_TILE, t)
    dest3 = dest.reshape(TOP_K, t // tt, tt).transpose(1, 0, 2).reshape(t // tt, 1, TOP_K * tt)
    last_blk = jnp.where(counts % MOE_BLOCK != 0, pad_end - MOE_BLOCK, -1)
    meta = jnp.concatenate([last_blk, pad_end[-1:] // MOE_BLOCK]).astype(I32)
    xs = _moe_dispatch(x, dest3, meta, n_blocks * MOE_BLOCK)
    ys = _moe_experts(xs, block_e, w_gate, w_up, w_down)
    return _moe_combine_ln(x, info, dest3, ys, ln_g, ln_b)


def _compress_kernel(ch_ref, pe_ref, w1_ref, w2_ref, o_ref):
    half = CMP_STRIDE * HEAD_DIM
    c = ch_ref[0, 0]
    n = c.shape[0]
    first = jnp.dot((c + pe_ref[0:1, :]).astype(BF16), w1_ref[:half, :], preferred_element_type=F32)
    second = jnp.dot((c + pe_ref[1:2, :]).astype(BF16), w1_ref[half:, :], preferred_element_type=F32)
    h = first + pltpu.roll(second, n - 1, 0)
    o_ref[0, 0] = jnp.dot(_gelu_tanh(h).astype(BF16), w2_ref[...],
                          preferred_element_type=F32).astype(o_ref.dtype)


def _compress(ch, pe, w1, w2):
    bsz, g, n, half = ch.shape
    hid = w1.shape[1]
    return pl.pallas_call(
        _compress_kernel,
        grid=(bsz, g),
        in_specs=[pl.BlockSpec((1, 1, n, half), lambda b, j: (b, j, 0, 0)),
                  pl.BlockSpec((2, half), lambda b, j: (0, 0)),
                  pl.BlockSpec((2 * half, hid), lambda b, j: (0, 0)),
                  pl.BlockSpec((hid, HEAD_DIM), lambda b, j: (0, 0))],
        out_specs=pl.BlockSpec((1, 1, n, HEAD_DIM), lambda b, j: (b, j, 0, 0)),
        out_shape=jax.ShapeDtypeStruct((bsz, g, n, HEAD_DIM), BF16),
        compiler_params=_params("arbitrary", "arbitrary"),
    )(ch, pe.reshape(2, half), w1.astype(BF16), w2.astype(BF16))


def _softmax_step(carry, s_t, v_t):
    m, acc = carry
    m_new = jnp.maximum(m, jnp.max(s_t, axis=0, keepdims=True))
    p = jnp.exp2(s_t - m_new).astype(BF16)
    acc = jnp.exp2(m - m_new) * acc + jnp.dot(v_t, p, preferred_element_type=F32)
    return m_new, acc


def _softmax_finish(acc):
    return acc[:HEAD_DIM] / jnp.maximum(acc[HEAD_DIM:HEAD_DIM + 1], TINY)


def _tile_heads(a):
    return jnp.concatenate([a] * HEADS_PER_GROUP, axis=1)


def _nsa_kernel(q_ref, kc_ref, vc_ref, ks_ref, vs_ref, kw_ref, vw_ref, gl_ref, ov_ref, o_ref):
    R = HEADS_PER_GROUP
    dk = HEAD_DIM
    tq = q_ref.shape[4] // R
    ncp = kc_ref.shape[2]
    nsl = ov_ref.shape[0]
    seq = kw_ref.shape[2]
    n_top = min(N_SEL, nsl)
    qi = pl.program_id(2)
    qs = qi * tq
    q_t = q_ref[0, 0, 0]
    t_lane = qs + lax.broadcasted_iota(I32, (1, tq), 1)

    init = (jnp.full((1, R * tq), NEG, F32), jnp.zeros((dk + V_PAD, R * tq), F32))

    nw = WINDOW + tq
    w0 = pl.multiple_of(jnp.maximum(qs - WINDOW, 0), tq)
    wpos = w0 + lax.broadcasted_iota(I32, (nw, tq), 0)
    bias_w = jnp.where((wpos <= t_lane) & (wpos > t_lane - WINDOW), 0.0, NEG)
    s = jnp.dot(kw_ref[0, 0, pl.ds(w0, nw), :], q_t, preferred_element_type=F32) + _tile_heads(bias_w)
    c0 = w0 // tq
    vw_slab = jnp.concatenate([vw_ref[0, 0, c0 + i] for i in range(nw // tq)], axis=1)
    o_w = _softmax_finish(_softmax_step(init, s, vw_slab)[1])

    c_row = lax.broadcasted_iota(I32, (ncp, tq), 0)
    bias_c = jnp.where(c_row * CMP_STRIDE + (CMP_BLOCK - 1) <= t_lane, 0.0, NEG)
    s_t = jnp.dot(kc_ref[0, 0], q_t, preferred_element_type=F32) + _tile_heads(bias_c)
    p = jnp.exp2(s_t - jnp.max(s_t, axis=0, keepdims=True))
    inv_l = 1.0 / jnp.maximum(jnp.sum(p, axis=0, keepdims=True), TINY)
    has_c = jnp.where(t_lane >= CMP_BLOCK - 1, 1.0, 0.0)
    inv_l = inv_l * _tile_heads(has_c)
    o_c = jnp.dot(vc_ref[0, 0], p.astype(BF16), preferred_element_type=F32) * inv_l
    pn = p * inv_l
    p_sum = pn[:, 0:tq]
    for r in range(1, R):
        p_sum = p_sum + pn[:, r * tq:(r + 1) * tq]

    ph, plo = _split_bf16(p_sum)
    imp = (jnp.dot(ov_ref[...], ph, preferred_element_type=F32)
           + jnp.dot(ov_ref[...], plo, preferred_element_type=F32))
    j_row = lax.broadcasted_iota(I32, (nsl, tq), 0).astype(F32)
    bt = (t_lane >> (SEL_BLOCK.bit_length() - 1)).astype(F32)
    forced = (j_row == 0) | (j_row == bt) | (j_row == bt - 1)
    future = j_row > bt
    work = jnp.where(forced, FORCE, jnp.where(future, -FORCE, imp))
    picked = jnp.zeros((nsl, tq), F32)
    for _ in range(n_top):
        mx = jnp.max(work, axis=0, keepdims=True)
        first = jnp.min(jnp.where(work == mx, j_row, float(nsl)), axis=0, keepdims=True)
        hit = j_row == first
        picked = jnp.where(hit, 1.0, picked)
        work = jnp.where(hit, -jnp.inf, work)
    sel_bias = jnp.where((picked > 0.5) & jnp.logical_not(future), 0.0, NEG).astype(BF16)
    q_aug = jnp.concatenate([q_t, _tile_heads(sel_bias)], axis=0)

    tk = min(SEL_KEY_TILE, seq)

    def sel_scores(kt):
        k0 = pl.multiple_of(kt * tk, tk)
        s = jnp.dot(ks_ref[0, 0, pl.ds(k0, tk), :], q_aug, preferred_element_type=F32)
        return s, vs_ref[0, 0, kt], k0

    def sel_step(kt, carry):
        s, v_t, _ = sel_scores(kt)
        return _softmax_step(carry, s, v_t)

    def sel_pair(i, carry):
        return sel_step(2 * i + 1, sel_step(2 * i, carry))

    n_full = qs // tk
    carry = lax.fori_loop(0, n_full // 2, sel_pair, init)
    carry = lax.cond(n_full % 2 == 1, lambda c: sel_step(n_full - 1, c), lambda c: c, carry)
    s, v_t, k0 = sel_scores(n_full)
    kpos = k0 + lax.broadcasted_iota(I32, (tk, tq), 0)
    s = s + _tile_heads(jnp.where(kpos <= t_lane, 0.0, NEG))
    o_s = _softmax_finish(_softmax_step(carry, s, v_t)[1])

    gates = jax.nn.sigmoid(gl_ref[0, 0, 0])
    o = gates[0:1, :] * o_c + gates[1:2, :] * o_s + gates[2:3, :] * o_w
    o_ref[0] = jnp.concatenate([o[:, r * tq:(r + 1) * tq].T for r in range(R)], axis=1).astype(o_ref.dtype)


def _nsa_attention(q_t, kcmp, vcmp_t, ks_aug, vs_t, kw, vw_t, gl_t):
    bsz, g, nq, dk, rq = q_t.shape
    seq = kw.shape[2]
    ncp = kcmp.shape[2]
    nsl = seq // SEL_BLOCK
    assert seq >= WINDOW + rq // HEADS_PER_GROUP and seq % min(SEL_KEY_TILE, seq) == 0
    c_start = jnp.arange(ncp) * CMP_STRIDE
    j_start = jnp.arange(nsl) * SEL_BLOCK
    overlap_t = ((c_start[None, :] < j_start[:, None] + SEL_BLOCK)
                 & (c_start[None, :] + CMP_BLOCK > j_start[:, None])).astype(BF16)
    per_bg = lambda shape: pl.BlockSpec((1, 1) + shape, lambda b, j, i: (b, j) + (0,) * len(shape))
    per_q = lambda shape: pl.BlockSpec((1, 1, 1) + shape, lambda b, j, i: (b, j, i, 0, 0))
    tk = min(SEL_KEY_TILE, seq)
    tq = rq // HEADS_PER_GROUP
    return pl.pallas_call(
        _nsa_kernel,
        grid=(bsz, g, nq),
        in_specs=[per_q((dk, rq)),
                  per_bg((ncp, dk)), per_bg((dk, ncp)),
                  per_bg((seq, dk + nsl)), per_bg((seq // tk, dk + V_PAD, tk)),
                  per_bg((seq, dk)), per_bg((seq // tq, dk + V_PAD, tq)),
                  per_q((3, rq)),
                  pl.BlockSpec((nsl, ncp), lambda b, j, i: (0, 0))],
        out_specs=pl.BlockSpec((1, tq, HEADS_PER_GROUP * dk), lambda b, j, i: (b, i, j)),
        out_shape=jax.ShapeDtypeStruct((bsz, seq, g * HEADS_PER_GROUP * dk), BF16),
        compiler_params=_params("arbitrary", "arbitrary", "arbitrary"),
    )(q_t, kcmp, vcmp_t, ks_aug, vs_t, kw, vw_t, gl_t, overlap_t)


def _rope_rows(v, cos, sin):
    w = v.shape[1]
    lane = lax.broadcasted_iota(I32, v.shape, 1)
    first_half = (lane & (HEAD_DIM // 2)) == 0
    partner = jnp.where(first_half, pltpu.roll(v, w - HEAD_DIM // 2, 1), pltpu.roll(v, HEAD_DIM // 2, 1))
    return v * cos + partner * sin


def _nsa_proj_kernel(x_ref, wq_ref, wk_ref, wv_ref, wc_ref, cq_ref, sq_ref, ck_ref, sk_ref,
                     q_ref, ks_ref, kw_ref, kc_ref, vcg_ref, vs_ref, vw_ref, *, row_tiles_per_seq):
    G, R, dk = N_KV_GROUPS, HEADS_PER_GROUP, HEAD_DIM
    tm = x_ref.shape[0]
    tq = q_ref.shape[4] // R
    tk = vs_ref.shape[4]
    nsl = ks_ref.shape[3] - dk
    xb = x_ref[...].astype(BF16)

    q_t = _nt_dot(wq_ref[...], xb).reshape(N_HEADS, dk, tm)
    partner = jnp.concatenate([q_t[:, dk // 2:], q_t[:, :dk // 2]], axis=1)
    q_t = (q_t * cq_ref[...][None] + partner * sq_ref[...][None]).astype(BF16)
    for g in range(G):
        for h in range(tm // tq):
            q_ref[0, g, h] = jnp.concatenate(
                [q_t[g * R + r][:, h * tq:(h + 1) * tq] for r in range(R)], axis=1)

    kk = jnp.dot(xb, wk_ref[...], preferred_element_type=F32)
    cos_k, sin_k = ck_ref[...], sk_ref[...]
    k_sel = _rope_rows(kk[:, :KV_WIDTH], cos_k, sin_k)
    k_win = _rope_rows(kk[:, KV_WIDTH:2 * KV_WIDTH], cos_k, sin_k)
    kc_ref[...] = _rope_rows(kk[:, 2 * KV_WIDTH:], cos_k, sin_k)
    pos = (pl.program_id(0) % row_tiles_per_seq) * tm + lax.broadcasted_iota(I32, (tm, nsl), 0)
    block_onehot = jnp.where((pos >> (SEL_BLOCK.bit_length() - 1))
                             == lax.broadcasted_iota(I32, (tm, nsl), 1), 1.0, 0.0)
    for g in range(G):
        ks_ref[0, g] = jnp.concatenate([k_sel[:, g * dk:(g + 1) * dk], block_onehot],
                                       axis=1).astype(BF16)
        kw_ref[0, g] = k_win[:, g * dk:(g + 1) * dk].astype(BF16)

    v_t = _nt_dot(wv_ref[...], xb)
    extra = jnp.where(lax.broadcasted_iota(I32, (V_PAD, tm), 0) == 0, 1.0, 0.0)
    for g in range(G):
        vs = jnp.concatenate([v_t[g * dk:(g + 1) * dk], extra], axis=0).astype(BF16)
        for c in range(tm // tk):
            vs_ref[0, g, c] = vs[:, c * tk:(c + 1) * tk]
        vw = jnp.concatenate([v_t[KV_WIDTH + g * dk:KV_WIDTH + (g + 1) * dk], extra],
                             axis=0).astype(BF16)
        for h in range(tm // tq):
            vw_ref[0, g, h] = vw[:, h * tq:(h + 1) * tq]

    vcg_ref[...] = jnp.dot(xb, wc_ref[...], preferred_element_type=F32)


def _nsa_project(x, bsz, seq, w_in):
    G, R, dk = N_KV_GROUPS, HEADS_PER_GROUP, HEAD_DIM
    t, d = x.shape
    qd = N_HEADS * dk
    tm = min(ROW_TILE, seq)
    tq = min(Q_TILE, seq)
    tk = min(SEL_KEY_TILE, seq)
    nsl = seq // SEL_BLOCK
    ns = seq // tm
    assert tm % tq == 0 and tm % tk == 0
    cut = [qd + j * KV_WIDTH for j in range(7)]
    w = w_in.astype(BF16)
    w_q, w_kc, w_vc, w_ks, w_vs, w_kw, w_vw, w_gl = (
        w[:, :qd], w[:, cut[0]:cut[1]], w[:, cut[1]:cut[2]], w[:, cut[2]:cut[3]],
        w[:, cut[3]:cut[4]], w[:, cut[4]:cut[5]], w[:, cut[5]:cut[6]], w[:, cut[6]:])
    n_gl = w_gl.shape[1]
    wq_t = w_q.T
    wk = jnp.concatenate([w_ks, w_kw, w_kc], axis=1)
    wv_t = jnp.concatenate([w_vs, w_vw], axis=1).T
    wc = jnp.concatenate([w_vc, w_gl, jnp.zeros((d, LANES - n_gl), BF16)], axis=1)
    pos = jnp.arange(seq, dtype=F32)
    inv = ROPE_THETA ** (-jnp.arange(0, dk, 2, dtype=F32) / dk)
    ang = pos[:, None] * inv[None, :]
    cos, sin = jnp.cos(ang), jnp.sin(ang)
    cos_h = jnp.concatenate([cos, cos], axis=1)
    sin_h = jnp.concatenate([-sin, sin], axis=1)
    cq, sq = (cos_h * QK_SCALE).T, (sin_h * QK_SCALE).T
    ck, sk = jnp.tile(cos_h, (1, G)), jnp.tile(sin_h, (1, G))
    full = lambda a: pl.BlockSpec(a.shape, lambda i: (0,) * a.ndim)
    b_of = lambda i: i // ns
    s_of = lambda i: i % ns
    out_shape = [jax.ShapeDtypeStruct((bsz, G, seq // tq, dk, R * tq), BF16),
                 jax.ShapeDtypeStruct((bsz, G, seq, dk + nsl), BF16),
                 jax.ShapeDtypeStruct((bsz, G, seq, dk), BF16),
                 jax.ShapeDtypeStruct((t, KV_WIDTH), F32),
                 jax.ShapeDtypeStruct((t, KV_WIDTH + LANES), F32),
                 jax.ShapeDtypeStruct((bsz, G, seq // tk, dk + V_PAD, tk), BF16),
                 jax.ShapeDtypeStruct((bsz, G, seq // tq, dk + V_PAD, tq), BF16)]
    out_specs = [pl.BlockSpec((1, G, tm // tq, dk, R * tq), lambda i: (b_of(i), 0, s_of(i), 0, 0)),
                 pl.BlockSpec((1, G, tm, dk + nsl), lambda i: (b_of(i), 0, s_of(i), 0)),
                 pl.BlockSpec((1, G, tm, dk), lambda i: (b_of(i), 0, s_of(i), 0)),
                 pl.BlockSpec((tm, KV_WIDTH), lambda i: (i, 0)),
                 pl.BlockSpec((tm, KV_WIDTH + LANES), lambda i: (i, 0)),
                 pl.BlockSpec((1, G, tm // tk, dk + V_PAD, tk), lambda i: (b_of(i), 0, s_of(i), 0, 0)),
                 pl.BlockSpec((1, G, tm // tq, dk + V_PAD, tq), lambda i: (b_of(i), 0, s_of(i), 0, 0))]
    return pl.pallas_call(
        functools.partial(_nsa_proj_kernel, row_tiles_per_seq=ns),
        grid=(t // tm,),
        in_specs=[pl.BlockSpec((tm, d), lambda i: (i, 0)),
                  full(wq_t), full(wk), full(wv_t), full(wc),
                  pl.BlockSpec((dk, tm), lambda i: (0, s_of(i))),
                  pl.BlockSpec((dk, tm), lambda i: (0, s_of(i))),
                  pl.BlockSpec((tm, KV_WIDTH), lambda i: (s_of(i), 0)),
                  pl.BlockSpec((tm, KV_WIDTH), lambda i: (s_of(i), 0))],
        out_specs=out_specs, out_shape=out_shape,
        compiler_params=_params("arbitrary"),
    )(x, wq_t, wk, wv_t, wc, cq, sq, ck, sk)


def _nsa_mixer_heads(x, bsz, seq, w_in, cmp_k_pe, cmp_k_w1, cmp_k_w2, cmp_v_pe, cmp_v_w1, cmp_v_w2):
    G, R, dk = N_KV_GROUPS, HEADS_PER_GROUP, HEAD_DIM
    qd = N_HEADS * dk
    tq = min(Q_TILE, seq)
    nq = seq // tq
    n_gl = N_HEADS * 3
    q_t, ks_aug, kw, kc, vcg, vs_t, vw_t = _nsa_project(x, bsz, seq, w_in)

    def chunks(a):
        a = a.reshape(bsz, seq // CMP_STRIDE, CMP_STRIDE, G, dk).transpose(0, 3, 1, 2, 4)
        return a.reshape(bsz, G, seq // CMP_STRIDE, CMP_STRIDE * dk)

    gl_t = vcg[:, KV_WIDTH:KV_WIDTH + n_gl].reshape(bsz, nq, tq, G, R, 3).transpose(0, 3, 1, 5, 4, 2)
    gl_t = gl_t.reshape(bsz, G, nq, 3, R * tq)
    kcmp = _compress(chunks(kc), cmp_k_pe, cmp_k_w1, cmp_k_w2)
    vcmp_t = _compress(chunks(vcg[:, :KV_WIDTH]), cmp_v_pe, cmp_v_w1, cmp_v_w2).transpose(0, 1, 3, 2)
    o = _nsa_attention(q_t, kcmp, vcmp_t, ks_aug, vs_t, kw, vw_t, gl_t)
    return o.reshape(bsz * seq, qd)


def _even_mixer_cat(x, bsz, seq, w_in, conv_w, lru_conv_w, lru_conv_b, w_a, b_a, w_x, b_x, lam):
    wa_bd = jax.scipy.linalg.block_diag(*w_a).astype(BF16)
    wx_bd = jax.scipy.linalg.block_diag(*w_x).astype(BF16)
    return _l0_mixer(x, w_in.astype(BF16), bsz, seq, conv_w, lru_conv_w, lru_conv_b, wa_bd, b_a, wx_bd, b_x, lam)


def kernel(x, l0_w_in, l0_conv_w, l0_lru_conv_w, l0_lru_conv_b, l0_lru_w_a, l0_lru_b_a, l0_lru_w_x, l0_lru_b_x, l0_lru_lambda, l0_w_out, l0_ln1_g, l0_ln1_b, l0_router_group_w, l0_router_group_b, l0_router_expert_w, l0_router_expert_b, l0_expert_w_gate, l0_expert_w_up, l0_expert_w_down, l0_ln2_g, l0_ln2_b, l1_w_in, l1_cmp_k_pe, l1_cmp_k_w1, l1_cmp_k_w2, l1_cmp_v_pe, l1_cmp_v_w1, l1_cmp_v_w2, l1_w_out, l1_ln1_g, l1_ln1_b, l1_router_group_w, l1_router_group_b, l1_router_expert_w, l1_router_expert_b, l1_expert_w_gate, l1_expert_w_up, l1_expert_w_down, l1_ln2_g, l1_ln2_b):
    bsz, seq, d = x.shape
    h = x.reshape(bsz * seq, d)
    y = _even_mixer_cat(h, bsz, seq, l0_w_in, l0_conv_w, l0_lru_conv_w, l0_lru_conv_b, l0_lru_w_a,
                        l0_lru_b_a, l0_lru_w_x, l0_lru_b_x, l0_lru_lambda)
    h = _matmul_residual_ln(y, l0_w_out.astype(BF16), h, l0_ln1_g, l0_ln1_b)
    h = _hier_moe_ln(h, l0_router_group_w, l0_router_group_b, l0_router_expert_w, l0_router_expert_b,
                     l0_expert_w_gate, l0_expert_w_up, l0_expert_w_down, l0_ln2_g, l0_ln2_b)
    y = _nsa_mixer_heads(h, bsz, seq, l1_w_in, l1_cmp_k_pe, l1_cmp_k_w1, l1_cmp_k_w2, l1_cmp_v_pe,
                         l1_cmp_v_w1, l1_cmp_v_w2)
    h = _matmul_residual_ln(y, l1_w_out.astype(BF16), h, l1_ln1_g, l1_ln1_b)
    h = _hier_moe_ln(h, l1_router_group_w, l1_router_group_b, l1_router_expert_w, l1_router_expert_b,
                     l1_expert_w_gate, l1_expert_w_up, l1_expert_w_down, l1_ln2_g, l1_ln2_b)
    return h.reshape(bsz, seq, d)
```

```python
import functools

import jax
import jax.numpy as jnp
from jax import lax
from jax.experimental import pallas as pl
from jax.experimental.pallas import tpu as pltpu

F32 = jnp.float32
BF16 = jnp.bfloat16
I32 = jnp.int32

DEPTH = 2
DN_ALPHA = (2 * DEPTH) ** 0.25
LN_EPS = 1e-5
CONV_WIDTH = 512
CONV_TAPS = 3
LRU_WIDTH = 512
LRU_CONV_TAPS = 4
LRU_C = 8.0
N_HEADS = 16
HEAD_DIM = 64
N_KV_GROUPS = 4
HEADS_PER_GROUP = N_HEADS // N_KV_GROUPS
KV_WIDTH = N_KV_GROUPS * HEAD_DIM
CMP_STRIDE = 16
CMP_BLOCK = 32
SEL_BLOCK = 64
N_SEL = 16
WINDOW = 512
ROPE_THETA = 10000.0
FORCE = 1e4
NEG = -1e30
TINY = 1e-30
QK_SCALE = HEAD_DIM ** -0.5 * 1.4426950408889634
N_GROUPS = 4
EXPERTS_PER_GROUP = 8
N_EXPERTS = N_GROUPS * EXPERTS_PER_GROUP
TOP_K = 2
MOE_BLOCK = 256

LANES = 128
VMEM_LIMIT_BYTES = 48 * 1024 * 1024
ROW_TILE = 512
SCAN_CHUNK = 256
GATHER_TILE = 512
Q_TILE = 512
SEL_KEY_TILE = 512
V_PAD = 16


def _params(*sem):
    return pltpu.CompilerParams(dimension_semantics=sem, vmem_limit_bytes=VMEM_LIMIT_BYTES)


def _layernorm(v, g, b):
    mu = jnp.mean(v, axis=-1, keepdims=True)
    d = v - mu
    var = jnp.mean(d * d, axis=-1, keepdims=True)
    return d * lax.rsqrt(var + LN_EPS) * g + b


def _gelu_tanh(x):
    return 0.5 * x * (1.0 + jnp.tanh(0.7978845608028654 * (x + 0.044715 * (x * x * x))))


def _nt_dot(a, b):
    return lax.dot_general(a, b, (((1,), (1,)), ((), ())), preferred_element_type=F32)


def _router_operands(w_rg, b_rg, w_re, b_re):
    d = w_rg.shape[0]
    npad = LANES - N_GROUPS - N_EXPERTS
    w = jnp.concatenate([w_rg, w_re, jnp.zeros((d, npad), F32)], axis=1)
    wh = w.astype(BF16)
    wl = (w - wh.astype(F32)).astype(BF16)
    b = jnp.concatenate([b_rg, b_re, jnp.zeros((npad,), F32)]).reshape(1, LANES)
    return wh, wl, b


def _mm_res_ln_route_kernel(y_ref, w_ref, x_ref, g_ref, b_ref, wh_ref, wl_ref, rb_ref,
                            o_ref, info_ref, info_t_ref, cnt_ref):
    acc = jnp.dot(y_ref[...].astype(BF16), w_ref[...], preferred_element_type=F32)
    out = _layernorm(DN_ALPHA * x_ref[...] + acc, g_ref[...], b_ref[...])
    o_ref[...] = out
    _route_tile(out, wh_ref, wl_ref, rb_ref, info_ref, info_t_ref, cnt_ref)


def _matmul_residual_ln_route(y, w, x, g, b, router):
    m, k = y.shape
    d = w.shape[1]
    tm = min(ROW_TILE, m)
    wh, wl, rb = _router_operands(*router)
    const = lambda a: pl.BlockSpec(a.shape, lambda i: (0,) * a.ndim)
    return pl.pallas_call(
        _mm_res_ln_route_kernel,
        grid=(m // tm,),
        in_specs=[pl.BlockSpec((tm, k), lambda i: (i, 0)),
                  const(w),
                  pl.BlockSpec((tm, d), lambda i: (i, 0)),
                  pl.BlockSpec((1, d), lambda i: (0, 0)),
                  pl.BlockSpec((1, d), lambda i: (0, 0)),
                  const(wh), const(wl), const(rb)],
        out_specs=[pl.BlockSpec((tm, d), lambda i: (i, 0)),
                   pl.BlockSpec((tm, LANES), lambda i: (i, 0)),
                   pl.BlockSpec((8, tm), lambda i: (0, i)),
                   pl.BlockSpec((8, LANES), lambda i: (0, 0))],
        out_shape=[jax.ShapeDtypeStruct((m, d), F32), jax.ShapeDtypeStruct((m, LANES), F32),
                   jax.ShapeDtypeStruct((8, m), F32), jax.ShapeDtypeStruct((8, LANES), F32)],
        compiler_params=_params("arbitrary"),
    )(y, w, x, g.reshape(1, d), b.reshape(1, d), wh, wl, rb)


def _shift_rows(cur, tail, d, row8):
    rolled = pltpu.roll(cur, d, 0)
    head = jnp.where(row8 < d, pltpu.roll(tail, d, 0), rolled[:8])
    return jnp.concatenate([head, rolled[8:]], axis=0)


def _l0_mixer_kernel(x_ref, win_ref, cw_ref, lw_ref, lb_ref, wa_ref, ba_ref, wx_ref, bx_ref, lam_ref,
                     o_ref, tail_c, tail_x, h_state):
    W = CONV_WIDTH
    tc = x_ref.shape[0]
    z = jnp.dot(x_ref[...].astype(BF16), win_ref[...], preferred_element_type=F32)

    @pl.when(pl.program_id(1) == 0)
    def _():
        tail_c[...] = jnp.zeros_like(tail_c)
        tail_x[...] = jnp.zeros_like(tail_x)
        h_state[...] = jnp.zeros_like(h_state)

    row8 = lax.broadcasted_iota(I32, (8, W), 0)
    row = lax.broadcasted_iota(I32, (tc, W), 0)

    ch = z[:, W:2 * W] * z[:, 2 * W:3 * W]
    tc_prev = tail_c[...]
    conv = ch * cw_ref[CONV_TAPS - 1:CONV_TAPS, :]
    for d in range(1, CONV_TAPS):
        conv = conv + _shift_rows(ch, tc_prev, d, row8) * cw_ref[CONV_TAPS - 1 - d:CONV_TAPS - d, :]
    tail_c[...] = ch[tc - 8:, :]
    o_ref[:, :W] = (z[:, :W] * conv).astype(o_ref.dtype)

    xl = z[:, 4 * W:5 * W]
    tx_prev = tail_x[...]
    xc = xl * lw_ref[LRU_CONV_TAPS - 1:LRU_CONV_TAPS, :]
    for d in range(1, LRU_CONV_TAPS):
        xc = xc + _shift_rows(xl, tx_prev, d, row8) * lw_ref[LRU_CONV_TAPS - 1 - d:LRU_CONV_TAPS - d, :]
    xc = xc + lb_ref[...]
    tail_x[...] = xl[tc - 8:, :]

    xcb = xc.astype(BF16)
    r = jax.nn.sigmoid(jnp.dot(xcb, wa_ref[...], preferred_element_type=F32) + ba_ref[...])
    ig = jax.nn.sigmoid(jnp.dot(xcb, wx_ref[...], preferred_element_type=F32) + bx_ref[...])
    nl = -lam_ref[...]
    softplus = jnp.maximum(nl, 0.0) + jnp.log1p(jnp.exp(-jnp.abs(nl)))
    log_a = -LRU_C * r * softplus
    a = jnp.exp(log_a)
    mult = jnp.sqrt(jnp.maximum(1.0 - a * a, 0.0))
    u = mult * ig * xc

    d = 1
    while d < tc:
        keep = row >= d
        a_s = jnp.where(keep, pltpu.roll(a, d, 0), 1.0)
        u_s = jnp.where(keep, pltpu.roll(u, d, 0), 0.0)
        u = a * u_s + u
        a = a * a_s
        d *= 2
    h = a * h_state[0:1, :] + u
    h_state[...] = jnp.broadcast_to(h[tc - 1:tc, :], h_state.shape)
    o_ref[:, W:] = (h * _gelu_tanh(z[:, 3 * W:4 * W])).astype(o_ref.dtype)


def _l0_mixer(x, w_in, bsz, seq, conv_w, lru_conv_w, lru_conv_b, wa_bd, b_a, wx_bd, b_x, lam):
    W = CONV_WIDTH
    tc = min(SCAN_CHUNK, seq)
    nchunk = seq // tc
    vec = lambda i, j: (0, 0)
    return pl.pallas_call(
        _l0_mixer_kernel,
        grid=(bsz, nchunk),
        in_specs=[pl.BlockSpec((tc, x.shape[1]), lambda i, j: (i * nchunk + j, 0)),
                  pl.BlockSpec(w_in.shape, vec),
                  pl.BlockSpec((CONV_TAPS, W), vec),
                  pl.BlockSpec((LRU_CONV_TAPS, W), vec),
                  pl.BlockSpec((1, W), vec),
                  pl.BlockSpec((W, W), vec),
                  pl.BlockSpec((1, W), vec),
                  pl.BlockSpec((W, W), vec),
                  pl.BlockSpec((1, W), vec),
                  pl.BlockSpec((1, W), vec)],
        out_specs=pl.BlockSpec((tc, 2 * W), lambda i, j: (i * nchunk + j, 0)),
        out_shape=jax.ShapeDtypeStruct((bsz * seq, 2 * W), BF16),
        scratch_shapes=[pltpu.VMEM((8, W), F32), pltpu.VMEM((8, W), F32), pltpu.VMEM((8, W), F32)],
        compiler_params=_params("arbitrary", "arbitrary"),
    )(x, w_in, conv_w, lru_conv_w, lru_conv_b.reshape(1, W), wa_bd, b_a.reshape(1, W), wx_bd,
      b_x.reshape(1, W), lam.reshape(1, W))


def _split_bf16(v):
    hi = v.astype(BF16)
    lo = (v - hi.astype(F32)).astype(BF16)
    return hi, lo


def _route_tile(x, wh_ref, wl_ref, b_ref, info_ref, info_t_ref, cnt_ref):
    tm = x.shape[0]

    @pl.when(pl.program_id(0) == 0)
    def _():
        cnt_ref[...] = jnp.zeros_like(cnt_ref)

    xh, xl = _split_bf16(x)
    logits = (jnp.dot(xh, wh_ref[...], preferred_element_type=F32)
              + jnp.dot(xl, wh_ref[...], preferred_element_type=F32)
              + jnp.dot(xh, wl_ref[...], preferred_element_type=F32)) + b_ref[...]
    lane = lax.broadcasted_iota(I32, logits.shape, 1).astype(F32)
    ninf = -jnp.inf

    is_g = lane < N_GROUPS
    gmax = jnp.max(jnp.where(is_g, logits, ninf), axis=-1, keepdims=True)
    g_star = jnp.min(jnp.where(is_g & (logits == gmax), lane, float(LANES)), axis=-1, keepdims=True)
    gsum = jnp.sum(jnp.where(is_g, jnp.exp(logits - gmax), 0.0), axis=-1, keepdims=True)
    p_grp = 1.0 / gsum

    lo_lane = N_GROUPS + EXPERTS_PER_GROUP * g_star
    is_e = (lane >= lo_lane) & (lane < lo_lane + EXPERTS_PER_GROUP)
    emax = jnp.max(jnp.where(is_e, logits, ninf), axis=-1, keepdims=True)
    ex = jnp.where(is_e, jnp.exp(logits - emax), 0.0)
    pe = ex / jnp.sum(ex, axis=-1, keepdims=True)
    pe_m = jnp.where(is_e, pe, ninf)
    v1 = jnp.max(pe_m, axis=-1, keepdims=True)
    l1 = jnp.min(jnp.where(pe_m == v1, lane, float(LANES)), axis=-1, keepdims=True)
    pe_m2 = jnp.where(lane == l1, ninf, pe_m)
    v2 = jnp.max(pe_m2, axis=-1, keepdims=True)
    l2 = jnp.min(jnp.where(pe_m2 == v2, lane, float(LANES)), axis=-1, keepdims=True)
    vs = v1 + v2
    w1 = p_grp * v1 / vs
    w2 = p_grp * v2 / vs

    hit1 = lane == l1
    hit2 = lane == l2
    onehot = jnp.where(hit1 | hit2, 1.0, 0.0)
    r_i = lax.broadcasted_iota(I32, (tm, tm), 0)
    c_i = lax.broadcasted_iota(I32, (tm, tm), 1)
    tri = jnp.where(c_i < r_i, 1.0, 0.0).astype(BF16)
    before = jnp.dot(tri, onehot.astype(BF16), preferred_element_type=F32) + cnt_ref[0:1, :]
    rank1 = jnp.sum(jnp.where(hit1, before, 0.0), axis=-1, keepdims=True)
    rank2 = jnp.sum(jnp.where(hit2, before, 0.0), axis=-1, keepdims=True)
    cnt_ref[...] = cnt_ref[...] + jnp.sum(onehot, axis=0, keepdims=True)

    e1 = l1 - N_GROUPS
    e2 = l2 - N_GROUPS
    info = jnp.where(lane == 0, e1, jnp.where(lane == 1, e2, jnp.where(lane == 2, w1, jnp.where(
        lane == 3, w2, jnp.where(lane == 4, rank1, jnp.where(lane == 5, rank2, 0.0))))))
    info_ref[...] = info
    info_t_ref[...] = info.T[0:8, :]


def _row_copy(src_ref, src_row, dst_ref, dst_row, sem):
    return pltpu.make_async_copy(src_ref.at[pl.ds(src_row, 1), :], dst_ref.at[pl.ds(dst_row, 1), :], sem)


def _dispatch_kernel(meta_ref, dest_ref, x_ref, xs_ref, zeros, sem, zsem):
    tt = x_ref.shape[0]
    n_blocks = xs_ref.shape[0] // MOE_BLOCK

    def zero_block(row0):
        return pltpu.make_async_copy(zeros, xs_ref.at[pl.ds(row0, MOE_BLOCK), :], zsem)

    @pl.when(pl.program_id(0) == 0)
    def _():
        zeros[...] = jnp.zeros_like(zeros)
        n_used = meta_ref[N_EXPERTS]

        def over_fill_targets(fn):
            for e in range(N_EXPERTS):
                @pl.when(meta_ref[e] >= 0)
                def _():
                    fn(pl.multiple_of(meta_ref[e], MOE_BLOCK))

            def trailing(j, c):
                fn(pl.multiple_of(j * MOE_BLOCK, MOE_BLOCK))
                return c

            lax.fori_loop(n_used, n_blocks, trailing, 0)

        over_fill_targets(lambda row0: zero_block(row0).start())
        over_fill_targets(lambda row0: zero_block(row0).wait())

    def issue(i, c):
        for k in range(TOP_K):
            _row_copy(x_ref, i, xs_ref, dest_ref[0, 0, k * tt + i], sem).start(priority=k)
        return c

    lax.fori_loop(0, tt, issue, 0, unroll=8)
    for _ in range(TOP_K):
        pltpu.make_async_copy(x_ref, xs_ref.at[pl.ds(0, tt), :], sem).wait()


def _moe_dispatch(x, dest3, meta, n_rows):
    t, d = x.shape
    tt = dest3.shape[2] // TOP_K
    return pl.pallas_call(
        _dispatch_kernel,
        grid_spec=pltpu.PrefetchScalarGridSpec(
            num_scalar_prefetch=1, grid=(t // tt,),
            in_specs=[pl.BlockSpec((1, 1, TOP_K * tt), lambda i, m: (i, 0, 0), memory_space=pltpu.SMEM),
                      pl.BlockSpec((tt, d), lambda i, m: (i, 0))],
            out_specs=pl.BlockSpec(memory_space=pl.ANY),
            scratch_shapes=[pltpu.VMEM((MOE_BLOCK, d), F32), pltpu.SemaphoreType.DMA(()),
                            pltpu.SemaphoreType.DMA(())]),
        out_shape=jax.ShapeDtypeStruct((n_rows, d), F32),
        compiler_params=_params("arbitrary"),
    )(meta, dest3, x)


def _expert_kernel(be_ref, xs_ref, wg_ref, wu_ref, wd_ref, ys_ref, wg_bf, wu_bf, wd_bf):
    i = pl.program_id(0)

    @pl.when((i == 0) | (be_ref[i] != be_ref[jnp.maximum(i - 1, 0)]))
    def _():
        wg_bf[...] = wg_ref[0].astype(BF16)
        wu_bf[...] = wu_ref[0].astype(BF16)
        wd_bf[...] = wd_ref[0].astype(BF16)

    xb = xs_ref[...].astype(BF16)
    gate = jnp.dot(xb, wg_bf[...], preferred_element_type=F32)
    up = jnp.dot(xb, wu_bf[...], preferred_element_type=F32)
    h = gate * jax.nn.sigmoid(gate) * up
    ys_ref[...] = jnp.dot(h.astype(BF16), wd_bf[...], preferred_element_type=F32)


def _moe_experts(xs, block_e, wg, wu, wd):
    p, d = xs.shape
    ff = wg.shape[2]
    nb = p // MOE_BLOCK
    return pl.pallas_call(
        _expert_kernel,
        grid_spec=pltpu.PrefetchScalarGridSpec(
            num_scalar_prefetch=1, grid=(nb,),
            in_specs=[pl.BlockSpec((MOE_BLOCK, d), lambda i, be: (i, 0)),
                      pl.BlockSpec((1, d, ff), lambda i, be: (be[i], 0, 0)),
                      pl.BlockSpec((1, d, ff), lambda i, be: (be[i], 0, 0)),
                      pl.BlockSpec((1, ff, d), lambda i, be: (be[i], 0, 0))],
            out_specs=pl.BlockSpec((MOE_BLOCK, d), lambda i, be: (i, 0)),
            scratch_shapes=[pltpu.VMEM((d, ff), BF16), pltpu.VMEM((d, ff), BF16), pltpu.VMEM((ff, d), BF16)]),
        out_shape=jax.ShapeDtypeStruct((p, d), F32),
        compiler_params=_params("arbitrary"),
    )(block_e, xs, wg, wu, wd)


def _combine_ln_kernel(dest_ref, dest_next_ref, x_ref, info_ref, g_ref, b_ref, ys_ref, o_ref, buf, sem):
    tt = x_ref.shape[0]
    i = pl.program_id(0)
    slot = i % 2

    def issue_tile(d_ref, s):
        def issue(r, c):
            for k in range(TOP_K):
                _row_copy(ys_ref, d_ref[0, 0, k * tt + r], buf.at[s], k * tt + r, sem.at[s]).start(priority=k)
            return c

        lax.fori_loop(0, tt, issue, 0, unroll=8)

    @pl.when(i == 0)
    def _():
        issue_tile(dest_ref, 0)

    @pl.when(i + 1 < pl.num_programs(0))
    def _():
        issue_tile(dest_next_ref, 1 - slot)

    pltpu.make_async_copy(ys_ref.at[pl.ds(0, TOP_K * tt), :], buf.at[slot], sem.at[slot]).wait()
    rows = buf[slot]
    y = info_ref[:, 2:3] * rows[0:tt, :] + info_ref[:, 3:4] * rows[tt:2 * tt, :]
    o_ref[...] = _layernorm(DN_ALPHA * x_ref[...] + y, g_ref[...], b_ref[...])


def _moe_combine_ln(x, info, dest3, ys, g, b):
    t, d = x.shape
    tt = dest3.shape[2] // TOP_K
    n = t // tt
    return pl.pallas_call(
        _combine_ln_kernel,
        grid=(n,),
        in_specs=[pl.BlockSpec((1, 1, TOP_K * tt), lambda i: (i, 0, 0), memory_space=pltpu.SMEM),
                  pl.BlockSpec((1, 1, TOP_K * tt), lambda i: (jnp.minimum(i + 1, n - 1), 0, 0),
                               memory_space=pltpu.SMEM),
                  pl.BlockSpec((tt, d), lambda i: (i, 0)),
                  pl.BlockSpec((tt, LANES), lambda i: (i, 0)),
                  pl.BlockSpec((1, d), lambda i: (0, 0)),
                  pl.BlockSpec((1, d), lambda i: (0, 0)),
                  pl.BlockSpec(memory_space=pl.ANY)],
        out_specs=pl.BlockSpec((tt, d), lambda i: (i, 0)),
        out_shape=jax.ShapeDtypeStruct((t, d), F32),
        scratch_shapes=[pltpu.VMEM((2, TOP_K * tt, d), F32), pltpu.SemaphoreType.DMA((2,))],
        compiler_params=_params("arbitrary"),
    )(dest3, dest3, x, info, g.reshape(1, d), b.reshape(1, d), ys)


def _hier_moe_ln(x, info, info_t, cnt, w_gate, w_up, w_down, ln_g, ln_b):
    t, d = x.shape
    a_total = t * TOP_K
    counts = cnt[0, N_GROUPS:N_GROUPS + N_EXPERTS].astype(I32)
    padded = (counts + MOE_BLOCK - 1) // MOE_BLOCK * MOE_BLOCK
    pad_end = jnp.cumsum(padded)
    pad_start = pad_end - padded
    n_blocks = -(-a_total // MOE_BLOCK) + N_EXPERTS
    e = info_t[0:TOP_K].astype(I32)
    rank = info_t[4:4 + TOP_K].astype(I32)
    expert_ids = jnp.arange(N_EXPERTS, dtype=I32)[:, None, None]
    dest = rank + jnp.sum(jnp.where(e[None] == expert_ids, pad_start[:, None, None], 0), axis=0)
    blk_start = jnp.arange(n_blocks, dtype=I32) * MOE_BLOCK
    block_e = jnp.minimum(jnp.sum((pad_end[None, :] <= blk_start[:, None]).astype(I32), axis=1),
                          N_EXPERTS - 1).astype(I32)
    tt = min(GATHER_TILE, t)
    dest3 = dest.reshape(TOP_K, t // tt, tt).transpose(1, 0, 2).reshape(t // tt, 1, TOP_K * tt)
    last_blk = jnp.where(counts % MOE_BLOCK != 0, pad_end - MOE_BLOCK, -1)
    meta = jnp.concatenate([last_blk, pad_end[-1:] // MOE_BLOCK]).astype(I32)
    xs = _moe_dispatch(x, dest3, meta, n_blocks * MOE_BLOCK)
    ys = _moe_experts(xs, block_e, w_gate, w_up, w_down)
    return _moe_combine_ln(x, info, dest3, ys, ln_g, ln_b)


def _compress_kernel(x_ref, pe_ref, w1_ref, w2_ref, w2t_ref, o_ref, ot_ref):
    n = o_ref.shape[2]
    dk = HEAD_DIM

    def half(p0):
        acc = None
        for p in range(CMP_STRIDE):
            rows = x_ref[pl.ds(p, n, stride=CMP_STRIDE), :] + pe_ref[p0 + p:p0 + p + 1, :]
            part = jnp.dot(rows.astype(BF16), w1_ref[p0 + p], preferred_element_type=F32)
            acc = part if acc is None else acc + part
        return acc

    h = half(0) + pltpu.roll(half(CMP_STRIDE), n - 1, 0)
    hb = _gelu_tanh(h).astype(BF16)
    out = jnp.dot(hb, w2_ref[...], preferred_element_type=F32)
    out_t = _nt_dot(w2t_ref[...], hb)
    for j in range(2):
        o_ref[0, j] = out[:, j * dk:(j + 1) * dk].astype(o_ref.dtype)
        ot_ref[0, j] = out_t[j * dk:(j + 1) * dk].astype(ot_ref.dtype)


def _compress(x, bsz, seq, pe, w1, w2):
    G, dk = N_KV_GROUPS, HEAD_DIM
    n = seq // CMP_STRIDE
    hid = w1.shape[1]
    z = jnp.zeros((CMP_BLOCK, dk, hid), F32)
    w1p = w1.reshape(CMP_BLOCK, dk, hid)
    w1_pair = jnp.concatenate([jnp.concatenate([w1p, z], axis=2),
                               jnp.concatenate([z, w1p], axis=2)], axis=1).astype(BF16)
    z2 = jnp.zeros((hid, dk), F32)
    w2_pair = jnp.concatenate([jnp.concatenate([w2, z2], axis=1),
                               jnp.concatenate([z2, w2], axis=1)], axis=0).astype(BF16)
    pe_pair = jnp.tile(pe, (1, 2))
    const = lambda a: pl.BlockSpec(a.shape, lambda b, j: (0,) * a.ndim)
    return pl.pallas_call(
        _compress_kernel,
        grid=(bsz, G // 2),
        in_specs=[pl.BlockSpec((seq, 2 * dk), lambda b, j: (b, j)),
                  const(pe_pair), const(w1_pair), const(w2_pair), const(w2_pair.T)],
        out_specs=[pl.BlockSpec((1, 2, n, dk), lambda b, j: (b, j, 0, 0)),
                   pl.BlockSpec((1, 2, dk, n), lambda b, j: (b, j, 0, 0))],
        out_shape=[jax.ShapeDtypeStruct((bsz, G, n, dk), BF16),
                   jax.ShapeDtypeStruct((bsz, G, dk, n), BF16)],
        compiler_params=_params("arbitrary", "arbitrary"),
    )(x, pe_pair, w1_pair, w2_pair, w2_pair.T)


def _softmax_step(carry, s_t, v_t):
    m, acc = carry
    m_new = jnp.maximum(m, jnp.max(s_t, axis=0, keepdims=True))
    p = jnp.exp2(s_t - m_new).astype(BF16)
    acc = jnp.exp2(m - m_new) * acc + jnp.dot(v_t, p, preferred_element_type=F32)
    return m_new, acc


def _softmax_finish(acc):
    return acc[:HEAD_DIM] / jnp.maximum(acc[HEAD_DIM:HEAD_DIM + 1], TINY)


def _tile_heads(a):
    return jnp.concatenate([a] * HEADS_PER_GROUP, axis=1)


def _nsa_kernel(q_ref, kc_ref, vc_ref, ks_ref, vs_ref, kw_ref, vw_ref, gl_ref, ov_ref, o_ref):
    R = HEADS_PER_GROUP
    dk = HEAD_DIM
    tq = q_ref.shape[4] // R
    ncp = kc_ref.shape[2]
    nsl = ov_ref.shape[0]
    seq = kw_ref.shape[2]
    n_top = min(N_SEL, nsl)
    qi = pl.program_id(2)
    qs = qi * tq
    q_t = q_ref[0, 0, 0]
    t_lane = qs + lax.broadcasted_iota(I32, (1, tq), 1)

    init = (jnp.full((1, R * tq), NEG, F32), jnp.zeros((dk + V_PAD, R * tq), F32))

    c_row = lax.broadcasted_iota(I32, (ncp, tq), 0)
    bias_c = jnp.where(c_row * CMP_STRIDE + (CMP_BLOCK - 1) <= t_lane, 0.0, NEG)
    s_t = jnp.dot(kc_ref[0, 0], q_t, preferred_element_type=F32) + _tile_heads(bias_c)
    p = jnp.exp2(s_t - jnp.max(s_t, axis=0, keepdims=True))
    inv_l = 1.0 / jnp.maximum(jnp.sum(p, axis=0, keepdims=True), TINY)
    has_c = jnp.where(t_lane >= CMP_BLOCK - 1, 1.0, 0.0)
    inv_l = inv_l * _tile_heads(has_c)
    o_c = jnp.dot(vc_ref[0, 0], p.astype(BF16), preferred_element_type=F32) * inv_l
    pn = p * inv_l
    p_sum = pn[:, 0:tq]
    for r in range(1, R):
        p_sum = p_sum + pn[:, r * tq:(r + 1) * tq]

    ph, plo = _split_bf16(p_sum)
    imp = (jnp.dot(ov_ref[...], ph, preferred_element_type=F32)
           + jnp.dot(ov_ref[...], plo, preferred_element_type=F32))
    j_row = lax.broadcasted_iota(I32, (nsl, tq), 0).astype(F32)
    bt = (t_lane >> (SEL_BLOCK.bit_length() - 1)).astype(F32)
    forced = (j_row == 0) | (j_row == bt) | (j_row == bt - 1)
    future = j_row > bt
    work = jnp.where(forced, FORCE, jnp.where(future, -FORCE, imp))
    picked = jnp.zeros((nsl, tq), F32)
    for _ in range(n_top):
        mx = jnp.max(work, axis=0, keepdims=True)
        first = jnp.min(jnp.where(work == mx, j_row, float(nsl)), axis=0, keepdims=True)
        hit = j_row == first
        picked = jnp.where(hit, 1.0, picked)
        work = jnp.where(hit, -jnp.inf, work)
    sel_bias = jnp.where((picked > 0.5) & jnp.logical_not(future), 0.0, NEG).astype(BF16)
    q_aug = jnp.concatenate([q_t, _tile_heads(sel_bias)], axis=0)

    nw = WINDOW + tq
    w0 = pl.multiple_of(jnp.maximum(qs - WINDOW, 0), tq)
    wpos = w0 + lax.broadcasted_iota(I32, (nw, tq), 0)
    bias_w = jnp.where((wpos <= t_lane) & (wpos > t_lane - WINDOW), 0.0, NEG)
    s = jnp.dot(kw_ref[0, 0, pl.ds(w0, nw), :], q_t, preferred_element_type=F32) + _tile_heads(bias_w)
    c0 = w0 // tq
    vw_slab = jnp.concatenate([vw_ref[0, 0, c0 + i] for i in range(nw // tq)], axis=1)
    o_w = _softmax_finish(_softmax_step(init, s, vw_slab)[1])

    tk = min(SEL_KEY_TILE, seq)

    def sel_scores(kt):
        k0 = pl.multiple_of(kt * tk, tk)
        s = jnp.dot(ks_ref[0, 0, pl.ds(k0, tk), :], q_aug, preferred_element_type=F32)
        return s, vs_ref[0, 0, kt], k0

    def sel_step(kt, carry):
        s, v_t, _ = sel_scores(kt)
        return _softmax_step(carry, s, v_t)

    def sel_pair(i, carry):
        return sel_step(2 * i + 1, sel_step(2 * i, carry))

    n_full = qs // tk
    carry = lax.fori_loop(0, n_full // 2, sel_pair, init)
    carry = lax.cond(n_full % 2 == 1, lambda c: sel_step(n_full - 1, c), lambda c: c, carry)
    s, v_t, k0 = sel_scores(n_full)
    kpos = k0 + lax.broadcasted_iota(I32, (tk, tq), 0)
    s = s + _tile_heads(jnp.where(kpos <= t_lane, 0.0, NEG))
    o_s = _softmax_finish(_softmax_step(carry, s, v_t)[1])

    gates = jax.nn.sigmoid(gl_ref[0, 0, 0])
    o = gates[0:1, :] * o_c + gates[1:2, :] * o_s + gates[2:3, :] * o_w
    o_ref[0] = jnp.concatenate([o[:, r * tq:(r + 1) * tq].T for r in range(R)], axis=1).astype(o_ref.dtype)


def _nsa_attention(q_t, kcmp, vcmp_t, ks_aug, vs_t, kw, vw_t, gl_t):
    bsz, g, nq, dk, rq = q_t.shape
    seq = kw.shape[2]
    ncp = kcmp.shape[2]
    nsl = seq // SEL_BLOCK
    assert seq >= WINDOW + rq // HEADS_PER_GROUP and seq % min(SEL_KEY_TILE, seq) == 0
    c_start = jnp.arange(ncp) * CMP_STRIDE
    j_start = jnp.arange(nsl) * SEL_BLOCK
    overlap_t = ((c_start[None, :] < j_start[:, None] + SEL_BLOCK)
                 & (c_start[None, :] + CMP_BLOCK > j_start[:, None])).astype(BF16)
    per_bg = lambda shape: pl.BlockSpec((1, 1) + shape, lambda b, j, i: (b, j) + (0,) * len(shape))
    per_q = lambda shape: pl.BlockSpec((1, 1, 1) + shape, lambda b, j, i: (b, j, i, 0, 0))
    tk = min(SEL_KEY_TILE, seq)
    tq = rq // HEADS_PER_GROUP
    return pl.pallas_call(
        _nsa_kernel,
        grid=(bsz, g, nq),
        in_specs=[per_q((dk, rq)),
                  per_bg((ncp, dk)), per_bg((dk, ncp)),
                  per_bg((seq, dk + nsl)), per_bg((seq // tk, dk + V_PAD, tk)),
                  per_bg((seq, dk)), per_bg((seq // tq, dk + V_PAD, tq)),
                  per_q((3, rq)),
                  pl.BlockSpec((nsl, ncp), lambda b, j, i: (0, 0))],
        out_specs=pl.BlockSpec((1, tq, HEADS_PER_GROUP * dk), lambda b, j, i: (b, i, j)),
        out_shape=jax.ShapeDtypeStruct((bsz, seq, g * HEADS_PER_GROUP * dk), BF16),
        compiler_params=_params("arbitrary", "arbitrary", "arbitrary"),
    )(q_t, kcmp, vcmp_t, ks_aug, vs_t, kw, vw_t, gl_t, overlap_t)


def _rope_rows(v, cos, sin):
    w = v.shape[1]
    lane = lax.broadcasted_iota(I32, v.shape, 1)
    first_half = (lane & (HEAD_DIM // 2)) == 0
    partner = jnp.where(first_half, pltpu.roll(v, w - HEAD_DIM // 2, 1), pltpu.roll(v, HEAD_DIM // 2, 1))
    return v * cos + partner * sin


def _nsa_proj_kernel(x_ref, wq_ref, wk_ref, wv_ref, wc_ref, wg_ref, cq_ref, sq_ref, ck_ref, sk_ref,
                     q_ref, ks_ref, kw_ref, kc_ref, vc_ref, vs_ref, vw_ref, gl_ref, *, row_tiles_per_seq):
    G, R, dk = N_KV_GROUPS, HEADS_PER_GROUP, HEAD_DIM
    tm = x_ref.shape[0]
    tq = q_ref.shape[4] // R
    tk = vs_ref.shape[4]
    nsl = ks_ref.shape[3] - dk
    xb = x_ref[...].astype(BF16)

    q_t = _nt_dot(wq_ref[...], xb).reshape(N_HEADS, dk, tm)
    partner = jnp.concatenate([q_t[:, dk // 2:], q_t[:, :dk // 2]], axis=1)
    q_t = (q_t * cq_ref[...][None] + partner * sq_ref[...][None]).astype(BF16)
    for g in range(G):
        for h in range(tm // tq):
            q_ref[0, g, h] = jnp.concatenate(
                [q_t[g * R + r][:, h * tq:(h + 1) * tq] for r in range(R)], axis=1)

    kk = jnp.dot(xb, wk_ref[...], preferred_element_type=F32)
    cos_k, sin_k = ck_ref[...], sk_ref[...]
    k_sel = _rope_rows(kk[:, :KV_WIDTH], cos_k, sin_k)
    k_win = _rope_rows(kk[:, KV_WIDTH:2 * KV_WIDTH], cos_k, sin_k)
    kc_ref[...] = _rope_rows(kk[:, 2 * KV_WIDTH:], cos_k, sin_k)
    pos = (pl.program_id(0) % row_tiles_per_seq) * tm + lax.broadcasted_iota(I32, (tm, nsl), 0)
    block_onehot = jnp.where((pos >> (SEL_BLOCK.bit_length() - 1))
                             == lax.broadcasted_iota(I32, (tm, nsl), 1), 1.0, 0.0)
    for g in range(G):
        ks_ref[0, g] = jnp.concatenate([k_sel[:, g * dk:(g + 1) * dk], block_onehot],
                                       axis=1).astype(BF16)
        kw_ref[0, g] = k_win[:, g * dk:(g + 1) * dk].astype(BF16)

    v_t = _nt_dot(wv_ref[...], xb)
    extra = jnp.where(lax.broadcasted_iota(I32, (V_PAD, tm), 0) == 0, 1.0, 0.0)
    for g in range(G):
        vs = jnp.concatenate([v_t[g * dk:(g + 1) * dk], extra], axis=0).astype(BF16)
        for c in range(tm // tk):
            vs_ref[0, g, c] = vs[:, c * tk:(c + 1) * tk]
        vw = jnp.concatenate([v_t[KV_WIDTH + g * dk:KV_WIDTH + (g + 1) * dk], extra],
                             axis=0).astype(BF16)
        for h in range(tm // tq):
            vw_ref[0, g, h] = vw[:, h * tq:(h + 1) * tq]

    vc_ref[...] = jnp.dot(xb, wc_ref[...], preferred_element_type=F32)

    gl = _nt_dot(wg_ref[...], xb)
    for g in range(G):
        for h in range(tm // tq):
            gl_ref[0, g, h] = jnp.concatenate(
                [jnp.concatenate([gl[(g * 3 + i) * R + r:(g * 3 + i) * R + r + 1, h * tq:(h + 1) * tq]
                                  for r in range(R)], axis=1) for i in range(3)], axis=0)


def _nsa_project(x, bsz, seq, w_in):
    G, R, dk = N_KV_GROUPS, HEADS_PER_GROUP, HEAD_DIM
    t, d = x.shape
    qd = N_HEADS * dk
    tm = min(ROW_TILE, seq)
    tq = min(Q_TILE, seq)
    tk = min(SEL_KEY_TILE, seq)
    nsl = seq // SEL_BLOCK
    ns = seq // tm
    assert tm % tq == 0 and tm % tk == 0
    cut = [qd + j * KV_WIDTH for j in range(7)]
    w = w_in.astype(BF16)
    w_q, w_kc, w_vc, w_ks, w_vs, w_kw, w_vw, w_gl = (
        w[:, :qd], w[:, cut[0]:cut[1]], w[:, cut[1]:cut[2]], w[:, cut[2]:cut[3]],
        w[:, cut[3]:cut[4]], w[:, cut[4]:cut[5]], w[:, cut[5]:cut[6]], w[:, cut[6]:])
    wq_t = w_q.T
    wk = jnp.concatenate([w_ks, w_kw, w_kc], axis=1)
    wv_t = jnp.concatenate([w_vs, w_vw], axis=1).T
    wc = w_vc
    wg_t = w_gl.reshape(d, G, R, 3).transpose(1, 3, 2, 0).reshape(G * 3 * R, d)
    wg_t = jnp.concatenate([wg_t, jnp.zeros((LANES - G * 3 * R, d), BF16)], axis=0)
    pos = jnp.arange(seq, dtype=F32)
    inv = ROPE_THETA ** (-jnp.arange(0, dk, 2, dtype=F32) / dk)
    ang = pos[:, None] * inv[None, :]
    cos, sin = jnp.cos(ang), jnp.sin(ang)
    cos_h = jnp.concatenate([cos, cos], axis=1)
    sin_h = jnp.concatenate([-sin, sin], axis=1)
    cq, sq = (cos_h * QK_SCALE).T, (sin_h * QK_SCALE).T
    ck, sk = jnp.tile(cos_h, (1, G)), jnp.tile(sin_h, (1, G))
    full = lambda a: pl.BlockSpec(a.shape, lambda i: (0,) * a.ndim)
    b_of = lambda i: i // ns
    s_of = lambda i: i % ns
    out_shape = [jax.ShapeDtypeStruct((bsz, G, seq // tq, dk, R * tq), BF16),
                 jax.ShapeDtypeStruct((bsz, G, seq, dk + nsl), BF16),
                 jax.ShapeDtypeStruct((bsz, G, seq, dk), BF16),
                 jax.ShapeDtypeStruct((t, KV_WIDTH), F32),
                 jax.ShapeDtypeStruct((t, KV_WIDTH), F32),
                 jax.ShapeDtypeStruct((bsz, G, seq // tk, dk + V_PAD, tk), BF16),
                 jax.ShapeDtypeStruct((bsz, G, seq // tq, dk + V_PAD, tq), BF16),
                 jax.ShapeDtypeStruct((bsz, G, seq // tq, 3, R * tq), F32)]
    out_specs = [pl.BlockSpec((1, G, tm // tq, dk, R * tq), lambda i: (b_of(i), 0, s_of(i), 0, 0)),
                 pl.BlockSpec((1, G, tm, dk + nsl), lambda i: (b_of(i), 0, s_of(i), 0)),
                 pl.BlockSpec((1, G, tm, dk), lambda i: (b_of(i), 0, s_of(i), 0)),
                 pl.BlockSpec((tm, KV_WIDTH), lambda i: (i, 0)),
                 pl.BlockSpec((tm, KV_WIDTH), lambda i: (i, 0)),
                 pl.BlockSpec((1, G, tm // tk, dk + V_PAD, tk), lambda i: (b_of(i), 0, s_of(i), 0, 0)),
                 pl.BlockSpec((1, G, tm // tq, dk + V_PAD, tq), lambda i: (b_of(i), 0, s_of(i), 0, 0)),
                 pl.BlockSpec((1, G, tm // tq, 3, R * tq), lambda i: (b_of(i), 0, s_of(i), 0, 0))]
    return pl.pallas_call(
        functools.partial(_nsa_proj_kernel, row_tiles_per_seq=ns),
        grid=(t // tm,),
        in_specs=[pl.BlockSpec((tm, d), lambda i: (i, 0)),
                  full(wq_t), full(wk), full(wv_t), full(wc), full(wg_t),
                  pl.BlockSpec((dk, tm), lambda i: (0, s_of(i))),
                  pl.BlockSpec((dk, tm), lambda i: (0, s_of(i))),
                  pl.BlockSpec((tm, KV_WIDTH), lambda i: (s_of(i), 0)),
                  pl.BlockSpec((tm, KV_WIDTH), lambda i: (s_of(i), 0))],
        out_specs=out_specs, out_shape=out_shape,
        compiler_params=_params("arbitrary"),
    )(x, wq_t, wk, wv_t, wc, wg_t, cq, sq, ck, sk)


def _nsa_mixer_heads(x, bsz, seq, w_in, cmp_k_pe, cmp_k_w1, cmp_k_w2, cmp_v_pe, cmp_v_w1, cmp_v_w2):
    q_t, ks_aug, kw, kc, vc, vs_t, vw_t, gl_t = _nsa_project(x, bsz, seq, w_in)
    kcmp, _ = _compress(kc, bsz, seq, cmp_k_pe, cmp_k_w1, cmp_k_w2)
    _, vcmp_t = _compress(vc, bsz, seq, cmp_v_pe, cmp_v_w1, cmp_v_w2)
    o = _nsa_attention(q_t, kcmp, vcmp_t, ks_aug, vs_t, kw, vw_t, gl_t)
    return o.reshape(bsz * seq, N_HEADS * HEAD_DIM)


def _even_mixer_cat(x, bsz, seq, w_in, conv_w, lru_conv_w, lru_conv_b, w_a, b_a, w_x, b_x, lam):
    wa_bd = jax.scipy.linalg.block_diag(*w_a).astype(BF16)
    wx_bd = jax.scipy.linalg.block_diag(*w_x).astype(BF16)
    return _l0_mixer(x, w_in.astype(BF16), bsz, seq, conv_w, lru_conv_w, lru_conv_b, wa_bd, b_a, wx_bd, b_x, lam)


def kernel(x, l0_w_in, l0_conv_w, l0_lru_conv_w, l0_lru_conv_b, l0_lru_w_a, l0_lru_b_a, l0_lru_w_x, l0_lru_b_x, l0_lru_lambda, l0_w_out, l0_ln1_g, l0_ln1_b, l0_router_group_w, l0_router_group_b, l0_router_expert_w, l0_router_expert_b, l0_expert_w_gate, l0_expert_w_up, l0_expert_w_down, l0_ln2_g, l0_ln2_b, l1_w_in, l1_cmp_k_pe, l1_cmp_k_w1, l1_cmp_k_w2, l1_cmp_v_pe, l1_cmp_v_w1, l1_cmp_v_w2, l1_w_out, l1_ln1_g, l1_ln1_b, l1_router_group_w, l1_router_group_b, l1_router_expert_w, l1_router_expert_b, l1_expert_w_gate, l1_expert_w_up, l1_expert_w_down, l1_ln2_g, l1_ln2_b):
    bsz, seq, d = x.shape
    h = x.reshape(bsz * seq, d)
    y = _even_mixer_cat(h, bsz, seq, l0_w_in, l0_conv_w, l0_lru_conv_w, l0_lru_conv_b, l0_lru_w_a,
                        l0_lru_b_a, l0_lru_w_x, l0_lru_b_x, l0_lru_lambda)
    h, *routing = _matmul_residual_ln_route(
        y, l0_w_out.astype(BF16), h, l0_ln1_g, l0_ln1_b,
        (l0_router_group_w, l0_router_group_b, l0_router_expert_w, l0_router_expert_b))
    h = _hier_moe_ln(h, *routing, l0_expert_w_gate, l0_expert_w_up, l0_expert_w_down, l0_ln2_g, l0_ln2_b)
    y = _nsa_mixer_heads(h, bsz, seq, l1_w_in, l1_cmp_k_pe, l1_cmp_k_w1, l1_cmp_k_w2, l1_cmp_v_pe,
                         l1_cmp_v_w1, l1_cmp_v_w2)
    h, *routing = _matmul_residual_ln_route(
        y, l1_w_out.astype(BF16), h, l1_ln1_g, l1_ln1_b,
        (l1_router_group_w, l1_router_group_b, l1_router_expert_w, l1_router_expert_b))
    h = _hier_moe_ln(h, *routing, l1_expert_w_gate, l1_expert_w_up, l1_expert_w_down, l1_ln2_g, l1_ln2_b)
    return h.reshape(bsz, seq, d)
```

```python
import functools

import jax
import jax.numpy as jnp
from jax import lax
from jax.experimental import pallas as pl
from jax.experimental.pallas import tpu as pltpu

F32 = jnp.float32
BF16 = jnp.bfloat16
I32 = jnp.int32

DEPTH = 2
DN_ALPHA = (2 * DEPTH) ** 0.25
LN_EPS = 1e-5
CONV_WIDTH = 512
CONV_TAPS = 3
LRU_WIDTH = 512
LRU_CONV_TAPS = 4
LRU_C = 8.0
N_HEADS = 16
HEAD_DIM = 64
N_KV_GROUPS = 4
HEADS_PER_GROUP = N_HEADS // N_KV_GROUPS
KV_WIDTH = N_KV_GROUPS * HEAD_DIM
CMP_STRIDE = 16
CMP_BLOCK = 32
SEL_BLOCK = 64
N_SEL = 16
WINDOW = 512
ROPE_THETA = 10000.0
FORCE = 1e4
NEG = -1e30
TINY = 1e-30
QK_SCALE = HEAD_DIM ** -0.5 * 1.4426950408889634
N_GROUPS = 4
EXPERTS_PER_GROUP = 8
N_EXPERTS = N_GROUPS * EXPERTS_PER_GROUP
TOP_K = 2
MOE_BLOCK = 256

LANES = 128
VMEM_LIMIT_BYTES = 48 * 1024 * 1024
ROW_TILE = 512
SCAN_CHUNK = 256
MIXER_SUBCHUNKS = 2
GATHER_TILE = 512
Q_TILE = 512
WIN_Q_TILE = 256
SEL_KEY_TILE = 512
V_PAD = 16


def _params(*sem):
    return pltpu.CompilerParams(dimension_semantics=sem, vmem_limit_bytes=VMEM_LIMIT_BYTES)


def _layernorm(v, g, b):
    mu = jnp.mean(v, axis=-1, keepdims=True)
    d = v - mu
    var = jnp.mean(d * d, axis=-1, keepdims=True)
    return d * lax.rsqrt(var + LN_EPS) * g + b


def _gelu_tanh(x):
    return 0.5 * x * (1.0 + jnp.tanh(0.7978845608028654 * (x + 0.044715 * (x * x * x))))


def _nt_dot(a, b):
    return lax.dot_general(a, b, (((1,), (1,)), ((), ())), preferred_element_type=F32)


def _router_operands(w_rg, b_rg, w_re, b_re):
    d = w_rg.shape[0]
    npad = LANES - N_GROUPS - N_EXPERTS
    w = jnp.concatenate([w_rg, w_re, jnp.zeros((d, npad), F32)], axis=1)
    wh = w.astype(BF16)
    wl = (w - wh.astype(F32)).astype(BF16)
    b = jnp.concatenate([b_rg, b_re, jnp.zeros((npad,), F32)]).reshape(1, LANES)
    return wh, wl, b


def _mm_res_ln_route_kernel(y_ref, w_ref, x_ref, g_ref, b_ref, wh_ref, wl_ref, rb_ref,
                            o_ref, info_ref, info_t_ref, cnt_ref):
    acc = jnp.dot(y_ref[...].astype(BF16), w_ref[...], preferred_element_type=F32)
    out = _layernorm(DN_ALPHA * x_ref[...] + acc, g_ref[...], b_ref[...])
    o_ref[...] = out
    _route_tile(out, wh_ref, wl_ref, rb_ref, info_ref, info_t_ref, cnt_ref)


def _matmul_residual_ln_route(y, w, x, g, b, router):
    m, k = y.shape
    d = w.shape[1]
    tm = min(ROW_TILE, m)
    wh, wl, rb = _router_operands(*router)
    const = lambda a: pl.BlockSpec(a.shape, lambda i: (0,) * a.ndim)
    return pl.pallas_call(
        _mm_res_ln_route_kernel,
        grid=(m // tm,),
        in_specs=[pl.BlockSpec((tm, k), lambda i: (i, 0)),
                  const(w),
                  pl.BlockSpec((tm, d), lambda i: (i, 0)),
                  pl.BlockSpec((1, d), lambda i: (0, 0)),
                  pl.BlockSpec((1, d), lambda i: (0, 0)),
                  const(wh), const(wl), const(rb)],
        out_specs=[pl.BlockSpec((tm, d), lambda i: (i, 0)),
                   pl.BlockSpec((tm, LANES), lambda i: (i, 0)),
                   pl.BlockSpec((8, tm), lambda i: (0, i)),
                   pl.BlockSpec((8, LANES), lambda i: (0, 0))],
        out_shape=[jax.ShapeDtypeStruct((m, d), F32), jax.ShapeDtypeStruct((m, LANES), F32),
                   jax.ShapeDtypeStruct((8, m), F32), jax.ShapeDtypeStruct((8, LANES), F32)],
        compiler_params=_params("arbitrary"),
    )(y, w, x, g.reshape(1, d), b.reshape(1, d), wh, wl, rb)


def _shift_rows(cur, tail, d, row8):
    rolled = pltpu.roll(cur, d, 0)
    head = jnp.where(row8 < d, pltpu.roll(tail, d, 0), rolled[:8])
    return jnp.concatenate([head, rolled[8:]], axis=0)


def _l0_mixer_kernel(x_ref, win_ref, cw_ref, lw_ref, lb_ref, wa_ref, ba_ref, wx_ref, bx_ref, lam_ref,
                     o_ref, tail_c, tail_x, h_state):
    W = CONV_WIDTH
    tc = x_ref.shape[0] // MIXER_SUBCHUNKS

    @pl.when(pl.program_id(1) == 0)
    def _():
        tail_c[...] = jnp.zeros_like(tail_c)
        tail_x[...] = jnp.zeros_like(tail_x)
        h_state[...] = jnp.zeros_like(h_state)

    row8 = lax.broadcasted_iota(I32, (8, W), 0)
    row = lax.broadcasted_iota(I32, (tc, W), 0)
    nl = -lam_ref[...]
    softplus = jnp.maximum(nl, 0.0) + jnp.log1p(jnp.exp(-jnp.abs(nl)))
    zs = [jnp.dot(x_ref[k * tc:(k + 1) * tc, :].astype(BF16), win_ref[...], preferred_element_type=F32)
          for k in range(MIXER_SUBCHUNKS)]
    tc_prev, tx_prev, h_prev = tail_c[...], tail_x[...], h_state[0:1, :]

    for k, z in enumerate(zs):
        rows = slice(k * tc, (k + 1) * tc)
        ch = z[:, W:2 * W] * z[:, 2 * W:3 * W]
        conv = ch * cw_ref[CONV_TAPS - 1:CONV_TAPS, :]
        for d in range(1, CONV_TAPS):
            conv = conv + _shift_rows(ch, tc_prev, d, row8) * cw_ref[CONV_TAPS - 1 - d:CONV_TAPS - d, :]
        tc_prev = ch[tc - 8:, :]
        o_ref[rows, :W] = (z[:, :W] * conv).astype(o_ref.dtype)

        xl = z[:, 4 * W:5 * W]
        xc = xl * lw_ref[LRU_CONV_TAPS - 1:LRU_CONV_TAPS, :]
        for d in range(1, LRU_CONV_TAPS):
            xc = xc + _shift_rows(xl, tx_prev, d, row8) * lw_ref[LRU_CONV_TAPS - 1 - d:LRU_CONV_TAPS - d, :]
        xc = xc + lb_ref[...]
        tx_prev = xl[tc - 8:, :]

        xcb = xc.astype(BF16)
        r = jax.nn.sigmoid(jnp.dot(xcb, wa_ref[...], preferred_element_type=F32) + ba_ref[...])
        ig = jax.nn.sigmoid(jnp.dot(xcb, wx_ref[...], preferred_element_type=F32) + bx_ref[...])
        log_a = -LRU_C * r * softplus
        a = jnp.exp(log_a)
        mult = jnp.sqrt(jnp.maximum(1.0 - a * a, 0.0))
        u = mult * ig * xc

        d = 1
        while d < tc:
            keep = row >= d
            a_s = jnp.where(keep, pltpu.roll(a, d, 0), 1.0)
            u_s = jnp.where(keep, pltpu.roll(u, d, 0), 0.0)
            u = a * u_s + u
            a = a * a_s
            d *= 2
        h = a * h_prev + u
        h_prev = h[tc - 1:tc, :]
        o_ref[rows, W:] = (h * _gelu_tanh(z[:, 3 * W:4 * W])).astype(o_ref.dtype)

    tail_c[...] = tc_prev
    tail_x[...] = tx_prev
    h_state[...] = jnp.broadcast_to(h_prev, h_state.shape)


def _l0_mixer(x, w_in, bsz, seq, conv_w, lru_conv_w, lru_conv_b, wa_bd, b_a, wx_bd, b_x, lam):
    W = CONV_WIDTH
    tc = min(SCAN_CHUNK * MIXER_SUBCHUNKS, seq)
    nchunk = seq // tc
    vec = lambda i, j: (0, 0)
    return pl.pallas_call(
        _l0_mixer_kernel,
        grid=(bsz, nchunk),
        in_specs=[pl.BlockSpec((tc, x.shape[1]), lambda i, j: (i * nchunk + j, 0)),
                  pl.BlockSpec(w_in.shape, vec),
                  pl.BlockSpec((CONV_TAPS, W), vec),
                  pl.BlockSpec((LRU_CONV_TAPS, W), vec),
                  pl.BlockSpec((1, W), vec),
                  pl.BlockSpec((W, W), vec),
                  pl.BlockSpec((1, W), vec),
                  pl.BlockSpec((W, W), vec),
                  pl.BlockSpec((1, W), vec),
                  pl.BlockSpec((1, W), vec)],
        out_specs=pl.BlockSpec((tc, 2 * W), lambda i, j: (i * nchunk + j, 0)),
        out_shape=jax.ShapeDtypeStruct((bsz * seq, 2 * W), BF16),
        scratch_shapes=[pltpu.VMEM((8, W), F32), pltpu.VMEM((8, W), F32), pltpu.VMEM((8, W), F32)],
        compiler_params=_params("arbitrary", "arbitrary"),
    )(x, w_in, conv_w, lru_conv_w, lru_conv_b.reshape(1, W), wa_bd, b_a.reshape(1, W), wx_bd,
      b_x.reshape(1, W), lam.reshape(1, W))


def _split_bf16(v):
    hi = v.astype(BF16)
    lo = (v - hi.astype(F32)).astype(BF16)
    return hi, lo


def _route_tile(x, wh_ref, wl_ref, b_ref, info_ref, info_t_ref, cnt_ref):
    tm = x.shape[0]

    @pl.when(pl.program_id(0) == 0)
    def _():
        cnt_ref[...] = jnp.zeros_like(cnt_ref)

    xh, xl = _split_bf16(x)
    logits = (jnp.dot(xh, wh_ref[...], preferred_element_type=F32)
              + jnp.dot(xl, wh_ref[...], preferred_element_type=F32)
              + jnp.dot(xh, wl_ref[...], preferred_element_type=F32)) + b_ref[...]
    lane = lax.broadcasted_iota(I32, logits.shape, 1).astype(F32)
    ninf = -jnp.inf

    is_g = lane < N_GROUPS
    gmax = jnp.max(jnp.where(is_g, logits, ninf), axis=-1, keepdims=True)
    g_star = jnp.min(jnp.where(is_g & (logits == gmax), lane, float(LANES)), axis=-1, keepdims=True)
    gsum = jnp.sum(jnp.where(is_g, jnp.exp(logits - gmax), 0.0), axis=-1, keepdims=True)
    p_grp = 1.0 / gsum

    lo_lane = N_GROUPS + EXPERTS_PER_GROUP * g_star
    is_e = (lane >= lo_lane) & (lane < lo_lane + EXPERTS_PER_GROUP)
    emax = jnp.max(jnp.where(is_e, logits, ninf), axis=-1, keepdims=True)
    ex = jnp.where(is_e, jnp.exp(logits - emax), 0.0)
    pe = ex / jnp.sum(ex, axis=-1, keepdims=True)
    pe_m = jnp.where(is_e, pe, ninf)
    v1 = jnp.max(pe_m, axis=-1, keepdims=True)
    l1 = jnp.min(jnp.where(pe_m == v1, lane, float(LANES)), axis=-1, keepdims=True)
    pe_m2 = jnp.where(lane == l1, ninf, pe_m)
    v2 = jnp.max(pe_m2, axis=-1, keepdims=True)
    l2 = jnp.min(jnp.where(pe_m2 == v2, lane, float(LANES)), axis=-1, keepdims=True)
    vs = v1 + v2
    w1 = p_grp * v1 / vs
    w2 = p_grp * v2 / vs

    hit1 = lane == l1
    hit2 = lane == l2
    onehot = jnp.where(hit1 | hit2, 1.0, 0.0)
    r_i = lax.broadcasted_iota(I32, (tm, tm), 0)
    c_i = lax.broadcasted_iota(I32, (tm, tm), 1)
    tri = jnp.where(c_i < r_i, 1.0, 0.0).astype(BF16)
    before = jnp.dot(tri, onehot.astype(BF16), preferred_element_type=F32) + cnt_ref[0:1, :]
    rank1 = jnp.sum(jnp.where(hit1, before, 0.0), axis=-1, keepdims=True)
    rank2 = jnp.sum(jnp.where(hit2, before, 0.0), axis=-1, keepdims=True)
    cnt_ref[...] = cnt_ref[...] + jnp.sum(onehot, axis=0, keepdims=True)

    e1 = l1 - N_GROUPS
    e2 = l2 - N_GROUPS
    info = jnp.where(lane == 0, e1, jnp.where(lane == 1, e2, jnp.where(lane == 2, w1, jnp.where(
        lane == 3, w2, jnp.where(lane == 4, rank1, jnp.where(lane == 5, rank2, 0.0))))))
    info_ref[...] = info
    info_t_ref[...] = info.T[0:8, :]


def _row_copy(src_ref, src_row, dst_ref, dst_row, sem):
    return pltpu.make_async_copy(src_ref.at[pl.ds(src_row, 1), :], dst_ref.at[pl.ds(dst_row, 1), :], sem)


def _dispatch_kernel(meta_ref, dest_ref, x_ref, xs_ref, zeros, sem, zsem):
    tt = x_ref.shape[0]
    n_blocks = xs_ref.shape[0] // MOE_BLOCK

    def zero_block(row0):
        return pltpu.make_async_copy(zeros, xs_ref.at[pl.ds(row0, MOE_BLOCK), :], zsem)

    @pl.when(pl.program_id(0) == 0)
    def _():
        zeros[...] = jnp.zeros_like(zeros)
        n_used = meta_ref[N_EXPERTS]

        def over_fill_targets(fn):
            for e in range(N_EXPERTS):
                @pl.when(meta_ref[e] >= 0)
                def _():
                    fn(pl.multiple_of(meta_ref[e], MOE_BLOCK))

            def trailing(j, c):
                fn(pl.multiple_of(j * MOE_BLOCK, MOE_BLOCK))
                return c

            lax.fori_loop(n_used, n_blocks, trailing, 0)

        over_fill_targets(lambda row0: zero_block(row0).start())
        over_fill_targets(lambda row0: zero_block(row0).wait())

    def issue(i, c):
        for k in range(TOP_K):
            _row_copy(x_ref, i, xs_ref, dest_ref[0, 0, k * tt + i], sem).start(priority=k)
        return c

    lax.fori_loop(0, tt, issue, 0, unroll=8)
    for _ in range(TOP_K):
        pltpu.make_async_copy(x_ref, xs_ref.at[pl.ds(0, tt), :], sem).wait()


def _moe_dispatch(x, dest3, meta, n_rows):
    t, d = x.shape
    tt = dest3.shape[2] // TOP_K
    return pl.pallas_call(
        _dispatch_kernel,
        grid_spec=pltpu.PrefetchScalarGridSpec(
            num_scalar_prefetch=1, grid=(t // tt,),
            in_specs=[pl.BlockSpec((1, 1, TOP_K * tt), lambda i, m: (i, 0, 0), memory_space=pltpu.SMEM),
                      pl.BlockSpec((tt, d), lambda i, m: (i, 0))],
            out_specs=pl.BlockSpec(memory_space=pl.ANY),
            scratch_shapes=[pltpu.VMEM((MOE_BLOCK, d), F32), pltpu.SemaphoreType.DMA(()),
                            pltpu.SemaphoreType.DMA(())]),
        out_shape=jax.ShapeDtypeStruct((n_rows, d), F32),
        compiler_params=_params("arbitrary"),
    )(meta, dest3, x)


def _expert_kernel(be_ref, xs_ref, wg_ref, wu_ref, wd_ref, ys_ref, wg_bf, wu_bf, wd_bf):
    i = pl.program_id(0)
    n_used = be_ref[pl.num_programs(0)]

    @pl.when((i == 0) | (be_ref[i] != be_ref[jnp.maximum(i - 1, 0)]))
    def _():
        wg_bf[...] = wg_ref[0].astype(BF16)
        wu_bf[...] = wu_ref[0].astype(BF16)
        wd_bf[...] = wd_ref[0].astype(BF16)

    @pl.when(i < n_used)
    def _():
        xb = xs_ref[...].astype(BF16)
        gate = jnp.dot(xb, wg_bf[...], preferred_element_type=F32)
        up = jnp.dot(xb, wu_bf[...], preferred_element_type=F32)
        h = gate * jax.nn.sigmoid(gate) * up
        ys_ref[...] = jnp.dot(h.astype(BF16), wd_bf[...], preferred_element_type=F32)

    @pl.when(i >= n_used)
    def _():
        ys_ref[...] = jnp.zeros_like(ys_ref)


def _moe_experts(xs, block_e, wg, wu, wd):
    p, d = xs.shape
    ff = wg.shape[2]
    nb = p // MOE_BLOCK
    return pl.pallas_call(
        _expert_kernel,
        grid_spec=pltpu.PrefetchScalarGridSpec(
            num_scalar_prefetch=1, grid=(nb,),
            in_specs=[pl.BlockSpec((MOE_BLOCK, d), lambda i, be: (i, 0)),
                      pl.BlockSpec((1, d, ff), lambda i, be: (be[i], 0, 0)),
                      pl.BlockSpec((1, d, ff), lambda i, be: (be[i], 0, 0)),
                      pl.BlockSpec((1, ff, d), lambda i, be: (be[i], 0, 0))],
            out_specs=pl.BlockSpec((MOE_BLOCK, d), lambda i, be: (i, 0)),
            scratch_shapes=[pltpu.VMEM((d, ff), BF16), pltpu.VMEM((d, ff), BF16), pltpu.VMEM((ff, d), BF16)]),
        out_shape=jax.ShapeDtypeStruct((p, d), F32),
        compiler_params=_params("arbitrary"),
    )(block_e, xs, wg, wu, wd)


def _combine_ln_kernel(dest_ref, dest_next_ref, x_ref, info_ref, g_ref, b_ref, ys_ref, o_ref, buf, sem):
    tt = x_ref.shape[0]
    i = pl.program_id(0)
    slot = i % 2

    def issue_tile(d_ref, s):
        def issue(r, c):
            for k in range(TOP_K):
                _row_copy(ys_ref, d_ref[0, 0, k * tt + r], buf.at[s], k * tt + r, sem.at[s]).start(priority=k)
            return c

        lax.fori_loop(0, tt, issue, 0, unroll=8)

    @pl.when(i == 0)
    def _():
        issue_tile(dest_ref, 0)

    @pl.when(i + 1 < pl.num_programs(0))
    def _():
        issue_tile(dest_next_ref, 1 - slot)

    pltpu.make_async_copy(ys_ref.at[pl.ds(0, TOP_K * tt), :], buf.at[slot], sem.at[slot]).wait()
    rows = buf[slot]
    y = info_ref[:, 2:3] * rows[0:tt, :] + info_ref[:, 3:4] * rows[tt:2 * tt, :]
    o_ref[...] = _layernorm(DN_ALPHA * x_ref[...] + y, g_ref[...], b_ref[...])


def _moe_combine_ln(x, info, dest3, ys, g, b):
    t, d = x.shape
    tt = dest3.shape[2] // TOP_K
    n = t // tt
    return pl.pallas_call(
        _combine_ln_kernel,
        grid=(n,),
        in_specs=[pl.BlockSpec((1, 1, TOP_K * tt), lambda i: (i, 0, 0), memory_space=pltpu.SMEM),
                  pl.BlockSpec((1, 1, TOP_K * tt), lambda i: (jnp.minimum(i + 1, n - 1), 0, 0),
                               memory_space=pltpu.SMEM),
                  pl.BlockSpec((tt, d), lambda i: (i, 0)),
                  pl.BlockSpec((tt, LANES), lambda i: (i, 0)),
                  pl.BlockSpec((1, d), lambda i: (0, 0)),
                  pl.BlockSpec((1, d), lambda i: (0, 0)),
                  pl.BlockSpec(memory_space=pl.ANY)],
        out_specs=pl.BlockSpec((tt, d), lambda i: (i, 0)),
        out_shape=jax.ShapeDtypeStruct((t, d), F32),
        scratch_shapes=[pltpu.VMEM((2, TOP_K * tt, d), F32), pltpu.SemaphoreType.DMA((2,))],
        compiler_params=_params("arbitrary"),
    )(dest3, dest3, x, info, g.reshape(1, d), b.reshape(1, d), ys)


def _hier_moe_ln(x, info, info_t, cnt, w_gate, w_up, w_down, ln_g, ln_b):
    t, d = x.shape
    a_total = t * TOP_K
    counts = cnt[0, N_GROUPS:N_GROUPS + N_EXPERTS].astype(I32)
    padded = (counts + MOE_BLOCK - 1) // MOE_BLOCK * MOE_BLOCK
    pad_end = jnp.cumsum(padded)
    pad_start = pad_end - padded
    n_blocks = -(-a_total // MOE_BLOCK) + N_EXPERTS
    e = info_t[0:TOP_K].astype(I32)
    rank = info_t[4:4 + TOP_K].astype(I32)
    expert_ids = jnp.arange(N_EXPERTS, dtype=I32)[:, None, None]
    dest = rank + jnp.sum(jnp.where(e[None] == expert_ids, pad_start[:, None, None], 0), axis=0)
    blk_start = jnp.arange(n_blocks, dtype=I32) * MOE_BLOCK
    block_e = jnp.minimum(jnp.sum((pad_end[None, :] <= blk_start[:, None]).astype(I32), axis=1),
                          N_EXPERTS - 1).astype(I32)
    tt = min(GATHER_TILE, t)
    dest3 = dest.reshape(TOP_K, t // tt, tt).transpose(1, 0, 2).reshape(t // tt, 1, TOP_K * tt)
    last_blk = jnp.where(counts % MOE_BLOCK != 0, pad_end - MOE_BLOCK, -1)
    meta = jnp.concatenate([last_blk, pad_end[-1:] // MOE_BLOCK]).astype(I32)
    xs = _moe_dispatch(x, dest3, meta, n_blocks * MOE_BLOCK)
    ys = _moe_experts(xs, jnp.concatenate([block_e, meta[N_EXPERTS:]]), w_gate, w_up, w_down)
    return _moe_combine_ln(x, info, dest3, ys, ln_g, ln_b)


def _compress_kernel(x_ref, pe_ref, w1_ref, w2_ref, w2t_ref, o_ref, ot_ref):
    n = o_ref.shape[2]
    dk = HEAD_DIM

    def half(p0):
        acc = None
        for p in range(CMP_STRIDE):
            rows = x_ref[pl.ds(p, n, stride=CMP_STRIDE), :] + pe_ref[p0 + p:p0 + p + 1, :]
            part = jnp.dot(rows.astype(BF16), w1_ref[p0 + p], preferred_element_type=F32)
            acc = part if acc is None else acc + part
        return acc

    h = half(0) + pltpu.roll(half(CMP_STRIDE), n - 1, 0)
    hb = _gelu_tanh(h).astype(BF16)
    out = jnp.dot(hb, w2_ref[...], preferred_element_type=F32)
    out_t = _nt_dot(w2t_ref[...], hb)
    for j in range(2):
        o_ref[0, j] = out[:, j * dk:(j + 1) * dk].astype(o_ref.dtype)
        ot_ref[0, j] = out_t[j * dk:(j + 1) * dk].astype(ot_ref.dtype)


def _compress(x, bsz, seq, pe, w1, w2):
    G, dk = N_KV_GROUPS, HEAD_DIM
    n = seq // CMP_STRIDE
    hid = w1.shape[1]
    z = jnp.zeros((CMP_BLOCK, dk, hid), F32)
    w1p = w1.reshape(CMP_BLOCK, dk, hid)
    w1_pair = jnp.concatenate([jnp.concatenate([w1p, z], axis=2),
                               jnp.concatenate([z, w1p], axis=2)], axis=1).astype(BF16)
    z2 = jnp.zeros((hid, dk), F32)
    w2_pair = jnp.concatenate([jnp.concatenate([w2, z2], axis=1),
                               jnp.concatenate([z2, w2], axis=1)], axis=0).astype(BF16)
    pe_pair = jnp.tile(pe, (1, 2))
    const = lambda a: pl.BlockSpec(a.shape, lambda b, j: (0,) * a.ndim)
    return pl.pallas_call(
        _compress_kernel,
        grid=(bsz, G // 2),
        in_specs=[pl.BlockSpec((seq, 2 * dk), lambda b, j: (b, j)),
                  const(pe_pair), const(w1_pair), const(w2_pair), const(w2_pair.T)],
        out_specs=[pl.BlockSpec((1, 2, n, dk), lambda b, j: (b, j, 0, 0)),
                   pl.BlockSpec((1, 2, dk, n), lambda b, j: (b, j, 0, 0))],
        out_shape=[jax.ShapeDtypeStruct((bsz, G, n, dk), BF16),
                   jax.ShapeDtypeStruct((bsz, G, dk, n), BF16)],
        compiler_params=_params("arbitrary", "arbitrary"),
    )(x, pe_pair, w1_pair, w2_pair, w2_pair.T)


def _softmax_step(carry, s_t, v_t):
    m, acc = carry
    m_new = jnp.maximum(m, jnp.max(s_t, axis=0, keepdims=True))
    p = jnp.exp2(s_t - m_new).astype(BF16)
    acc = jnp.exp2(m - m_new) * acc + jnp.dot(v_t, p, preferred_element_type=F32)
    return m_new, acc


def _softmax_finish(acc):
    return acc[:HEAD_DIM] / jnp.maximum(acc[HEAD_DIM:HEAD_DIM + 1], TINY)


def _tile_heads(a):
    return jnp.concatenate([a] * HEADS_PER_GROUP, axis=1)


def _nsa_kernel(q_ref, kc_ref, vc_ref, ks_ref, vs_ref, kw_ref, vw_ref, gl_ref, ov_ref, o_ref):
    R = HEADS_PER_GROUP
    dk = HEAD_DIM
    tq = q_ref.shape[4] // R
    ncp = kc_ref.shape[2]
    nsl = ov_ref.shape[0]
    seq = kw_ref.shape[2]
    n_top = min(N_SEL, nsl)
    qi = pl.program_id(2)
    qs = qi * tq
    q_t = q_ref[0, 0, 0]
    t_lane = qs + lax.broadcasted_iota(I32, (1, tq), 1)

    init = (jnp.full((1, R * tq), NEG, F32), jnp.zeros((dk + V_PAD, R * tq), F32))

    c_row = lax.broadcasted_iota(I32, (ncp, tq), 0)
    bias_c = jnp.where(c_row * CMP_STRIDE + (CMP_BLOCK - 1) <= t_lane, 0.0, NEG)
    s_t = jnp.dot(kc_ref[0, 0], q_t, preferred_element_type=F32) + _tile_heads(bias_c)
    p = jnp.exp2(s_t - jnp.max(s_t, axis=0, keepdims=True))
    inv_l = 1.0 / jnp.maximum(jnp.sum(p, axis=0, keepdims=True), TINY)
    has_c = jnp.where(t_lane >= CMP_BLOCK - 1, 1.0, 0.0)
    inv_l = inv_l * _tile_heads(has_c)
    o_c = jnp.dot(vc_ref[0, 0], p.astype(BF16), preferred_element_type=F32) * inv_l
    pn = p * inv_l
    p_sum = pn[:, 0:tq]
    for r in range(1, R):
        p_sum = p_sum + pn[:, r * tq:(r + 1) * tq]

    ph, plo = _split_bf16(p_sum)
    imp = (jnp.dot(ov_ref[...], ph, preferred_element_type=F32)
           + jnp.dot(ov_ref[...], plo, preferred_element_type=F32))
    j_row = lax.broadcasted_iota(I32, (nsl, tq), 0).astype(F32)
    bt = (t_lane >> (SEL_BLOCK.bit_length() - 1)).astype(F32)
    forced = (j_row == 0) | (j_row == bt) | (j_row == bt - 1)
    future = j_row > bt
    work = jnp.where(forced, FORCE, jnp.where(future, -FORCE, imp))
    picked = jnp.zeros((nsl, tq), F32)
    for _ in range(n_top):
        mx = jnp.max(work, axis=0, keepdims=True)
        first = jnp.min(jnp.where(work == mx, j_row, float(nsl)), axis=0, keepdims=True)
        hit = j_row == first
        picked = jnp.where(hit, 1.0, picked)
        work = jnp.where(hit, -jnp.inf, work)
    sel_bias = jnp.where((picked > 0.5) & jnp.logical_not(future), 0.0, NEG).astype(BF16)
    q_aug = jnp.concatenate([q_t, _tile_heads(sel_bias)], axis=0)

    tw = vw_ref.shape[4]
    nw = WINDOW + tw
    o_w_parts = []
    for h in range(tq // tw):
        cols = [slice(r * tq + h * tw, r * tq + (h + 1) * tw) for r in range(R)]
        q_h = jnp.concatenate([q_t[:, c] for c in cols], axis=1)
        t_h = t_lane[:, h * tw:(h + 1) * tw]
        w0 = pl.multiple_of(jnp.maximum(qs + h * tw - WINDOW, 0), tw)
        wpos = w0 + lax.broadcasted_iota(I32, (nw, tw), 0)
        bias_w = jnp.where((wpos <= t_h) & (wpos > t_h - WINDOW), 0.0, NEG)
        s = jnp.dot(kw_ref[0, 0, pl.ds(w0, nw), :], q_h, preferred_element_type=F32) + _tile_heads(bias_w)
        c0 = w0 // tw
        vw_slab = jnp.concatenate([vw_ref[0, 0, c0 + i] for i in range(nw // tw)], axis=1)
        init_h = (jnp.full((1, R * tw), NEG, F32), jnp.zeros((dk + V_PAD, R * tw), F32))
        o_w_parts.append(_softmax_finish(_softmax_step(init_h, s, vw_slab)[1]))
    o_w = jnp.concatenate([o_w_parts[h][:, r * tw:(r + 1) * tw]
                           for r in range(R) for h in range(tq // tw)], axis=1)

    tk = min(SEL_KEY_TILE, seq)

    def sel_scores(kt):
        k0 = pl.multiple_of(kt * tk, tk)
        s = jnp.dot(ks_ref[0, 0, pl.ds(k0, tk), :], q_aug, preferred_element_type=F32)
        return s, vs_ref[0, 0, kt], k0

    def sel_step(kt, carry):
        s, v_t, _ = sel_scores(kt)
        return _softmax_step(carry, s, v_t)

    def sel_pair(i, carry):
        return sel_step(2 * i + 1, sel_step(2 * i, carry))

    n_full = qs // tk
    carry = lax.fori_loop(0, n_full // 2, sel_pair, init)
    carry = lax.cond(n_full % 2 == 1, lambda c: sel_step(n_full - 1, c), lambda c: c, carry)
    s, v_t, k0 = sel_scores(n_full)
    kpos = k0 + lax.broadcasted_iota(I32, (tk, tq), 0)
    s = s + _tile_heads(jnp.where(kpos <= t_lane, 0.0, NEG))
    o_s = _softmax_finish(_softmax_step(carry, s, v_t)[1])

    gates = jax.nn.sigmoid(gl_ref[0, 0, 0])
    o = gates[0:1, :] * o_c + gates[1:2, :] * o_s + gates[2:3, :] * o_w
    o_ref[0] = jnp.concatenate([o[:, r * tq:(r + 1) * tq].T for r in range(R)], axis=1).astype(o_ref.dtype)


def _nsa_attention(q_t, kcmp, vcmp_t, ks_aug, vs_t, kw, vw_t, gl_t):
    bsz, g, nq, dk, rq = q_t.shape
    seq = kw.shape[2]
    ncp = kcmp.shape[2]
    nsl = seq // SEL_BLOCK
    assert seq >= WINDOW + rq // HEADS_PER_GROUP and seq % min(SEL_KEY_TILE, seq) == 0
    c_start = jnp.arange(ncp) * CMP_STRIDE
    j_start = jnp.arange(nsl) * SEL_BLOCK
    overlap_t = ((c_start[None, :] < j_start[:, None] + SEL_BLOCK)
                 & (c_start[None, :] + CMP_BLOCK > j_start[:, None])).astype(BF16)
    per_bg = lambda shape: pl.BlockSpec((1, 1) + shape, lambda b, j, i: (b, j) + (0,) * len(shape))
    per_q = lambda shape: pl.BlockSpec((1, 1, 1) + shape, lambda b, j, i: (b, j, i, 0, 0))
    tk = min(SEL_KEY_TILE, seq)
    tq = rq // HEADS_PER_GROUP
    return pl.pallas_call(
        _nsa_kernel,
        grid=(bsz, g, nq),
        in_specs=[per_q((dk, rq)),
                  per_bg((ncp, dk)), per_bg((dk, ncp)),
                  per_bg((seq, dk + nsl)), per_bg((seq // tk, dk + V_PAD, tk)),
                  per_bg((seq, dk)), per_bg(vw_t.shape[2:]),
                  per_q((3, rq)),
                  pl.BlockSpec((nsl, ncp), lambda b, j, i: (0, 0))],
        out_specs=pl.BlockSpec((1, tq, HEADS_PER_GROUP * dk), lambda b, j, i: (b, i, j)),
        out_shape=jax.ShapeDtypeStruct((bsz, seq, g * HEADS_PER_GROUP * dk), BF16),
        compiler_params=_params("arbitrary", "arbitrary", "arbitrary"),
    )(q_t, kcmp, vcmp_t, ks_aug, vs_t, kw, vw_t, gl_t, overlap_t)


def _rope_rows(v, cos, sin):
    w = v.shape[1]
    lane = lax.broadcasted_iota(I32, v.shape, 1)
    first_half = (lane & (HEAD_DIM // 2)) == 0
    partner = jnp.where(first_half, pltpu.roll(v, w - HEAD_DIM // 2, 1), pltpu.roll(v, HEAD_DIM // 2, 1))
    return v * cos + partner * sin


def _nsa_proj_kernel(x_ref, wq_ref, wk_ref, wv_ref, wc_ref, wg_ref, cq_ref, sq_ref, ck_ref, sk_ref,
                     q_ref, ks_ref, kw_ref, kc_ref, vc_ref, vs_ref, vw_ref, gl_ref, *, row_tiles_per_seq):
    G, R, dk = N_KV_GROUPS, HEADS_PER_GROUP, HEAD_DIM
    tm = x_ref.shape[0]
    tq = q_ref.shape[4] // R
    tk = vs_ref.shape[4]
    nsl = ks_ref.shape[3] - dk
    xb = x_ref[...].astype(BF16)

    q_t = _nt_dot(wq_ref[...], xb).reshape(N_HEADS, dk, tm)
    partner = jnp.concatenate([q_t[:, dk // 2:], q_t[:, :dk // 2]], axis=1)
    q_t = (q_t * cq_ref[...][None] + partner * sq_ref[...][None]).astype(BF16)
    for g in range(G):
        for h in range(tm // tq):
            q_ref[0, g, h] = jnp.concatenate(
                [q_t[g * R + r][:, h * tq:(h + 1) * tq] for r in range(R)], axis=1)

    kk = jnp.dot(xb, wk_ref[...], preferred_element_type=F32)
    cos_k, sin_k = ck_ref[...], sk_ref[...]
    k_sel = _rope_rows(kk[:, :KV_WIDTH], cos_k, sin_k)
    k_win = _rope_rows(kk[:, KV_WIDTH:2 * KV_WIDTH], cos_k, sin_k)
    kc_ref[...] = _rope_rows(kk[:, 2 * KV_WIDTH:], cos_k, sin_k)
    pos = (pl.program_id(0) % row_tiles_per_seq) * tm + lax.broadcasted_iota(I32, (tm, nsl), 0)
    block_onehot = jnp.where((pos >> (SEL_BLOCK.bit_length() - 1))
                             == lax.broadcasted_iota(I32, (tm, nsl), 1), 1.0, 0.0)
    for g in range(G):
        ks_ref[0, g] = jnp.concatenate([k_sel[:, g * dk:(g + 1) * dk], block_onehot],
                                       axis=1).astype(BF16)
        kw_ref[0, g] = k_win[:, g * dk:(g + 1) * dk].astype(BF16)

    v_t = _nt_dot(wv_ref[...], xb)
    extra = jnp.where(lax.broadcasted_iota(I32, (V_PAD, tm), 0) == 0, 1.0, 0.0)
    for g in range(G):
        vs = jnp.concatenate([v_t[g * dk:(g + 1) * dk], extra], axis=0).astype(BF16)
        for c in range(tm // tk):
            vs_ref[0, g, c] = vs[:, c * tk:(c + 1) * tk]
        vw = jnp.concatenate([v_t[KV_WIDTH + g * dk:KV_WIDTH + (g + 1) * dk], extra],
                             axis=0).astype(BF16)
        tw = vw_ref.shape[4]
        for h in range(tm // tw):
            vw_ref[0, g, h] = vw[:, h * tw:(h + 1) * tw]

    vc_ref[...] = jnp.dot(xb, wc_ref[...], preferred_element_type=F32)

    gl = _nt_dot(wg_ref[...], xb)
    for g in range(G):
        for h in range(tm // tq):
            gl_ref[0, g, h] = jnp.concatenate(
                [jnp.concatenate([gl[(g * 3 + i) * R + r:(g * 3 + i) * R + r + 1, h * tq:(h + 1) * tq]
                                  for r in range(R)], axis=1) for i in range(3)], axis=0)


def _nsa_project(x, bsz, seq, w_in):
    G, R, dk = N_KV_GROUPS, HEADS_PER_GROUP, HEAD_DIM
    t, d = x.shape
    qd = N_HEADS * dk
    tm = min(ROW_TILE, seq)
    tq = min(Q_TILE, seq)
    tk = min(SEL_KEY_TILE, seq)
    tw = min(WIN_Q_TILE, tq)
    nsl = seq // SEL_BLOCK
    ns = seq // tm
    assert tm % tq == 0 and tm % tk == 0 and tq % tw == 0
    cut = [qd + j * KV_WIDTH for j in range(7)]
    w = w_in.astype(BF16)
    w_q, w_kc, w_vc, w_ks, w_vs, w_kw, w_vw, w_gl = (
        w[:, :qd], w[:, cut[0]:cut[1]], w[:, cut[1]:cut[2]], w[:, cut[2]:cut[3]],
        w[:, cut[3]:cut[4]], w[:, cut[4]:cut[5]], w[:, cut[5]:cut[6]], w[:, cut[6]:])
    wq_t = w_q.T
    wk = jnp.concatenate([w_ks, w_kw, w_kc], axis=1)
    wv_t = jnp.concatenate([w_vs, w_vw], axis=1).T
    wc = w_vc
    wg_t = w_gl.reshape(d, G, R, 3).transpose(1, 3, 2, 0).reshape(G * 3 * R, d)
    wg_t = jnp.concatenate([wg_t, jnp.zeros((LANES - G * 3 * R, d), BF16)], axis=0)
    pos = jnp.arange(seq, dtype=F32)
    inv = ROPE_THETA ** (-jnp.arange(0, dk, 2, dtype=F32) / dk)
    ang = pos[:, None] * inv[None, :]
    cos, sin = jnp.cos(ang), jnp.sin(ang)
    cos_h = jnp.concatenate([cos, cos], axis=1)
    sin_h = jnp.concatenate([-sin, sin], axis=1)
    cq, sq = (cos_h * QK_SCALE).T, (sin_h * QK_SCALE).T
    ck, sk = jnp.tile(cos_h, (1, G)), jnp.tile(sin_h, (1, G))
    full = lambda a: pl.BlockSpec(a.shape, lambda i: (0,) * a.ndim)
    b_of = lambda i: i // ns
    s_of = lambda i: i % ns
    out_shape = [jax.ShapeDtypeStruct((bsz, G, seq // tq, dk, R * tq), BF16),
                 jax.ShapeDtypeStruct((bsz, G, seq, dk + nsl), BF16),
                 jax.ShapeDtypeStruct((bsz, G, seq, dk), BF16),
                 jax.ShapeDtypeStruct((t, KV_WIDTH), F32),
                 jax.ShapeDtypeStruct((t, KV_WIDTH), F32),
                 jax.ShapeDtypeStruct((bsz, G, seq // tk, dk + V_PAD, tk), BF16),
                 jax.ShapeDtypeStruct((bsz, G, seq // tw, dk + V_PAD, tw), BF16),
                 jax.ShapeDtypeStruct((bsz, G, seq // tq, 3, R * tq), F32)]
    out_specs = [pl.BlockSpec((1, G, tm // tq, dk, R * tq), lambda i: (b_of(i), 0, s_of(i), 0, 0)),
                 pl.BlockSpec((1, G, tm, dk + nsl), lambda i: (b_of(i), 0, s_of(i), 0)),
                 pl.BlockSpec((1, G, tm, dk), lambda i: (b_of(i), 0, s_of(i), 0)),
                 pl.BlockSpec((tm, KV_WIDTH), lambda i: (i, 0)),
                 pl.BlockSpec((tm, KV_WIDTH), lambda i: (i, 0)),
                 pl.BlockSpec((1, G, tm // tk, dk + V_PAD, tk), lambda i: (b_of(i), 0, s_of(i), 0, 0)),
                 pl.BlockSpec((1, G, tm // tw, dk + V_PAD, tw), lambda i: (b_of(i), 0, s_of(i), 0, 0)),
                 pl.BlockSpec((1, G, tm // tq, 3, R * tq), lambda i: (b_of(i), 0, s_of(i), 0, 0))]
    return pl.pallas_call(
        functools.partial(_nsa_proj_kernel, row_tiles_per_seq=ns),
        grid=(t // tm,),
        in_specs=[pl.BlockSpec((tm, d), lambda i: (i, 0)),
                  full(wq_t), full(wk), full(wv_t), full(wc), full(wg_t),
                  pl.BlockSpec((dk, tm), lambda i: (0, s_of(i))),
                  pl.BlockSpec((dk, tm), lambda i: (0, s_of(i))),
                  pl.BlockSpec((tm, KV_WIDTH), lambda i: (s_of(i), 0)),
                  pl.BlockSpec((tm, KV_WIDTH), lambda i: (s_of(i), 0))],
        out_specs=out_specs, out_shape=out_shape,
        compiler_params=_params("arbitrary"),
    )(x, wq_t, wk, wv_t, wc, wg_t, cq, sq, ck, sk)


def _nsa_mixer_heads(x, bsz, seq, w_in, cmp_k_pe, cmp_k_w1, cmp_k_w2, cmp_v_pe, cmp_v_w1, cmp_v_w2):
    q_t, ks_aug, kw, kc, vc, vs_t, vw_t, gl_t = _nsa_project(x, bsz, seq, w_in)
    kcmp, _ = _compress(kc, bsz, seq, cmp_k_pe, cmp_k_w1, cmp_k_w2)
    _, vcmp_t = _compress(vc, bsz, seq, cmp_v_pe, cmp_v_w1, cmp_v_w2)
    o = _nsa_attention(q_t, kcmp, vcmp_t, ks_aug, vs_t, kw, vw_t, gl_t)
    return o.reshape(bsz * seq, N_HEADS * HEAD_DIM)


def _even_mixer_cat(x, bsz, seq, w_in, conv_w, lru_conv_w, lru_conv_b, w_a, b_a, w_x, b_x, lam):
    wa_bd = jax.scipy.linalg.block_diag(*w_a).astype(BF16)
    wx_bd = jax.scipy.linalg.block_diag(*w_x).astype(BF16)
    return _l0_mixer(x, w_in.astype(BF16), bsz, seq, conv_w, lru_conv_w, lru_conv_b, wa_bd, b_a, wx_bd, b_x, lam)


def kernel(x, l0_w_in, l0_conv_w, l0_lru_conv_w, l0_lru_conv_b, l0_lru_w_a, l0_lru_b_a, l0_lru_w_x, l0_lru_b_x, l0_lru_lambda, l0_w_out, l0_ln1_g, l0_ln1_b, l0_router_group_w, l0_router_group_b, l0_router_expert_w, l0_router_expert_b, l0_expert_w_gate, l0_expert_w_up, l0_expert_w_down, l0_ln2_g, l0_ln2_b, l1_w_in, l1_cmp_k_pe, l1_cmp_k_w1, l1_cmp_k_w2, l1_cmp_v_pe, l1_cmp_v_w1, l1_cmp_v_w2, l1_w_out, l1_ln1_g, l1_ln1_b, l1_router_group_w, l1_router_group_b, l1_router_expert_w, l1_router_expert_b, l1_expert_w_gate, l1_expert_w_up, l1_expert_w_down, l1_ln2_g, l1_ln2_b):
    bsz, seq, d = x.shape
    h = x.reshape(bsz * seq, d)
    y = _even_mixer_cat(h, bsz, seq, l0_w_in, l0_conv_w, l0_lru_conv_w, l0_lru_conv_b, l0_lru_w_a,
                        l0_lru_b_a, l0_lru_w_x, l0_lru_b_x, l0_lru_lambda)
    h, *routing = _matmul_residual_ln_route(
        y, l0_w_out.astype(BF16), h, l0_ln1_g, l0_ln1_b,
        (l0_router_group_w, l0_router_group_b, l0_router_expert_w, l0_router_expert_b))
    h = _hier_moe_ln(h, *routing, l0_expert_w_gate, l0_expert_w_up, l0_expert_w_down, l0_ln2_g, l0_ln2_b)
    y = _nsa_mixer_heads(h, bsz, seq, l1_w_in, l1_cmp_k_pe, l1_cmp_k_w1, l1_cmp_k_w2, l1_cmp_v_pe,
                         l1_cmp_v_w1, l1_cmp_v_w2)
    h, *routing = _matmul_residual_ln_route(
        y, l1_w_out.astype(BF16), h, l1_ln1_g, l1_ln1_b,
        (l1_router_group_w, l1_router_group_b, l1_router_expert_w, l1_router_expert_b))
    h = _hier_moe_ln(h, *routing, l1_expert_w_gate, l1_expert_w_up, l1_expert_w_down, l1_ln2_g, l1_ln2_b)
    return h.reshape(bsz, seq, d)
```

```python
import functools

import jax
import jax.numpy as jnp
from jax import lax
from jax.experimental import pallas as pl
from jax.experimental.pallas import tpu as pltpu

F32 = jnp.float32
BF16 = jnp.bfloat16
I32 = jnp.int32

DEPTH = 2
DN_ALPHA = (2 * DEPTH) ** 0.25
LN_EPS = 1e-5
CONV_WIDTH = 512
CONV_TAPS = 3
LRU_WIDTH = 512
LRU_CONV_TAPS = 4
LRU_C = 8.0
N_HEADS = 16
HEAD_DIM = 64
N_KV_GROUPS = 4
HEADS_PER_GROUP = N_HEADS // N_KV_GROUPS
KV_WIDTH = N_KV_GROUPS * HEAD_DIM
CMP_STRIDE = 16
CMP_BLOCK = 32
SEL_BLOCK = 64
N_SEL = 16
WINDOW = 512
ROPE_THETA = 10000.0
FORCE = 1e4
NEG = -1e30
TINY = 1e-30
QK_SCALE = HEAD_DIM ** -0.5 * 1.4426950408889634
N_GROUPS = 4
EXPERTS_PER_GROUP = 8
N_EXPERTS = N_GROUPS * EXPERTS_PER_GROUP
TOP_K = 2

LANES = 128
VMEM_LIMIT_BYTES = 48 * 1024 * 1024
ROW_TILE = 512
SCAN_CHUNK = 256
MIXER_SUBCHUNKS = 2
GATHER_TILE = 512
MOE_BLOCK = 512
Q_TILE = 512
WIN_Q_TILE = 256
SEL_KEY_TILE = 512
V_PAD = 16


def _params(*sem):
    return pltpu.CompilerParams(dimension_semantics=sem, vmem_limit_bytes=VMEM_LIMIT_BYTES)


def _layernorm(v, g, b):
    mu = jnp.mean(v, axis=-1, keepdims=True)
    d = v - mu
    var = jnp.mean(d * d, axis=-1, keepdims=True)
    return d * lax.rsqrt(var + LN_EPS) * g + b


def _gelu_tanh(x):
    return 0.5 * x * (1.0 + jnp.tanh(0.7978845608028654 * (x + 0.044715 * (x * x * x))))


def _nt_dot(a, b):
    return lax.dot_general(a, b, (((1,), (1,)), ((), ())), preferred_element_type=F32)


def _router_operands(w_rg, b_rg, w_re, b_re):
    d = w_rg.shape[0]
    npad = LANES - N_GROUPS - N_EXPERTS
    w = jnp.concatenate([w_rg, w_re, jnp.zeros((d, npad), F32)], axis=1)
    wh = w.astype(BF16)
    wl = (w - wh.astype(F32)).astype(BF16)
    b = jnp.concatenate([b_rg, b_re, jnp.zeros((npad,), F32)]).reshape(1, LANES)
    return wh, wl, b


def _mm_res_ln_route_kernel(y_ref, w_ref, x_ref, g_ref, b_ref, wh_ref, wl_ref, rb_ref,
                            o_ref, info_ref, info_t_ref, cnt_ref):
    acc = jnp.dot(y_ref[...].astype(BF16), w_ref[...], preferred_element_type=F32)
    out = _layernorm(DN_ALPHA * x_ref[...] + acc, g_ref[...], b_ref[...])
    o_ref[...] = out
    _route_tile(out, wh_ref, wl_ref, rb_ref, info_ref, info_t_ref, cnt_ref)


def _matmul_residual_ln_route(y, w, x, g, b, router):
    m, k = y.shape
    d = w.shape[1]
    tm = min(ROW_TILE, m)
    wh, wl, rb = _router_operands(*router)
    const = lambda a: pl.BlockSpec(a.shape, lambda i: (0,) * a.ndim)
    return pl.pallas_call(
        _mm_res_ln_route_kernel,
        grid=(m // tm,),
        in_specs=[pl.BlockSpec((tm, k), lambda i: (i, 0)),
                  const(w),
                  pl.BlockSpec((tm, d), lambda i: (i, 0)),
                  pl.BlockSpec((1, d), lambda i: (0, 0)),
                  pl.BlockSpec((1, d), lambda i: (0, 0)),
                  const(wh), const(wl), const(rb)],
        out_specs=[pl.BlockSpec((tm, d), lambda i: (i, 0)),
                   pl.BlockSpec((tm, LANES), lambda i: (i, 0)),
                   pl.BlockSpec((8, tm), lambda i: (0, i)),
                   pl.BlockSpec((8, LANES), lambda i: (0, 0))],
        out_shape=[jax.ShapeDtypeStruct((m, d), F32), jax.ShapeDtypeStruct((m, LANES), F32),
                   jax.ShapeDtypeStruct((8, m), F32), jax.ShapeDtypeStruct((8, LANES), F32)],
        compiler_params=_params("arbitrary"),
    )(y, w, x, g.reshape(1, d), b.reshape(1, d), wh, wl, rb)


def _shift_rows(cur, tail, d, row8):
    rolled = pltpu.roll(cur, d, 0)
    head = jnp.where(row8 < d, pltpu.roll(tail, d, 0), rolled[:8])
    return jnp.concatenate([head, rolled[8:]], axis=0)


def _l0_mixer_kernel(x_ref, win_ref, cw_ref, lw_ref, lb_ref, wa_ref, ba_ref, wx_ref, bx_ref, lam_ref,
                     o_ref, tail_c, tail_x, h_state):
    W = CONV_WIDTH
    tc = x_ref.shape[0] // MIXER_SUBCHUNKS

    @pl.when(pl.program_id(1) == 0)
    def _():
        tail_c[...] = jnp.zeros_like(tail_c)
        tail_x[...] = jnp.zeros_like(tail_x)
        h_state[...] = jnp.zeros_like(h_state)

    row8 = lax.broadcasted_iota(I32, (8, W), 0)
    row = lax.broadcasted_iota(I32, (tc, W), 0)
    nl = -lam_ref[...]
    softplus = jnp.maximum(nl, 0.0) + jnp.log1p(jnp.exp(-jnp.abs(nl)))
    zs = [jnp.dot(x_ref[k * tc:(k + 1) * tc, :].astype(BF16), win_ref[...], preferred_element_type=F32)
          for k in range(MIXER_SUBCHUNKS)]
    tc_prev, tx_prev, h_prev = tail_c[...], tail_x[...], h_state[0:1, :]

    for k, z in enumerate(zs):
        rows = slice(k * tc, (k + 1) * tc)
        ch = z[:, W:2 * W] * z[:, 2 * W:3 * W]
        conv = ch * cw_ref[CONV_TAPS - 1:CONV_TAPS, :]
        for d in range(1, CONV_TAPS):
            conv = conv + _shift_rows(ch, tc_prev, d, row8) * cw_ref[CONV_TAPS - 1 - d:CONV_TAPS - d, :]
        tc_prev = ch[tc - 8:, :]
        o_ref[rows, :W] = (z[:, :W] * conv).astype(o_ref.dtype)

        xl = z[:, 4 * W:5 * W]
        xc = xl * lw_ref[LRU_CONV_TAPS - 1:LRU_CONV_TAPS, :]
        for d in range(1, LRU_CONV_TAPS):
            xc = xc + _shift_rows(xl, tx_prev, d, row8) * lw_ref[LRU_CONV_TAPS - 1 - d:LRU_CONV_TAPS - d, :]
        xc = xc + lb_ref[...]
        tx_prev = xl[tc - 8:, :]

        xcb = xc.astype(BF16)
        r = jax.nn.sigmoid(jnp.dot(xcb, wa_ref[...], preferred_element_type=F32) + ba_ref[...])
        ig = jax.nn.sigmoid(jnp.dot(xcb, wx_ref[...], preferred_element_type=F32) + bx_ref[...])
        log_a = -LRU_C * r * softplus
        a = jnp.exp(log_a)
        mult = jnp.sqrt(jnp.maximum(1.0 - a * a, 0.0))
        u = mult * ig * xc

        d = 1
        while d < tc:
            keep = row >= d
            a_s = jnp.where(keep, pltpu.roll(a, d, 0), 1.0)
            u_s = jnp.where(keep, pltpu.roll(u, d, 0), 0.0)
            u = a * u_s + u
            a = a * a_s
            d *= 2
        h = a * h_prev + u
        h_prev = h[tc - 1:tc, :]
        o_ref[rows, W:] = (h * _gelu_tanh(z[:, 3 * W:4 * W])).astype(o_ref.dtype)

    tail_c[...] = tc_prev
    tail_x[...] = tx_prev
    h_state[...] = jnp.broadcast_to(h_prev, h_state.shape)


def _l0_mixer(x, w_in, bsz, seq, conv_w, lru_conv_w, lru_conv_b, wa_bd, b_a, wx_bd, b_x, lam):
    W = CONV_WIDTH
    tc = min(SCAN_CHUNK * MIXER_SUBCHUNKS, seq)
    nchunk = seq // tc
    vec = lambda i, j: (0, 0)
    return pl.pallas_call(
        _l0_mixer_kernel,
        grid=(bsz, nchunk),
        in_specs=[pl.BlockSpec((tc, x.shape[1]), lambda i, j: (i * nchunk + j, 0)),
                  pl.BlockSpec(w_in.shape, vec),
                  pl.BlockSpec((CONV_TAPS, W), vec),
                  pl.BlockSpec((LRU_CONV_TAPS, W), vec),
                  pl.BlockSpec((1, W), vec),
                  pl.BlockSpec((W, W), vec),
                  pl.BlockSpec((1, W), vec),
                  pl.BlockSpec((W, W), vec),
                  pl.BlockSpec((1, W), vec),
                  pl.BlockSpec((1, W), vec)],
        out_specs=pl.BlockSpec((tc, 2 * W), lambda i, j: (i * nchunk + j, 0)),
        out_shape=jax.ShapeDtypeStruct((bsz * seq, 2 * W), BF16),
        scratch_shapes=[pltpu.VMEM((8, W), F32), pltpu.VMEM((8, W), F32), pltpu.VMEM((8, W), F32)],
        compiler_params=_params("arbitrary", "arbitrary"),
    )(x, w_in, conv_w, lru_conv_w, lru_conv_b.reshape(1, W), wa_bd, b_a.reshape(1, W), wx_bd,
      b_x.reshape(1, W), lam.reshape(1, W))


def _split_bf16(v):
    hi = v.astype(BF16)
    lo = (v - hi.astype(F32)).astype(BF16)
    return hi, lo


def _route_tile(x, wh_ref, wl_ref, b_ref, info_ref, info_t_ref, cnt_ref):
    tm = x.shape[0]

    @pl.when(pl.program_id(0) == 0)
    def _():
        cnt_ref[...] = jnp.zeros_like(cnt_ref)

    xh, xl = _split_bf16(x)
    logits = (jnp.dot(xh, wh_ref[...], preferred_element_type=F32)
              + jnp.dot(xl, wh_ref[...], preferred_element_type=F32)
              + jnp.dot(xh, wl_ref[...], preferred_element_type=F32)) + b_ref[...]
    lane = lax.broadcasted_iota(I32, logits.shape, 1).astype(F32)
    ninf = -jnp.inf

    is_g = lane < N_GROUPS
    gmax = jnp.max(jnp.where(is_g, logits, ninf), axis=-1, keepdims=True)
    g_star = jnp.min(jnp.where(is_g & (logits == gmax), lane, float(LANES)), axis=-1, keepdims=True)
    gsum = jnp.sum(jnp.where(is_g, jnp.exp(logits - gmax), 0.0), axis=-1, keepdims=True)
    p_grp = 1.0 / gsum

    lo_lane = N_GROUPS + EXPERTS_PER_GROUP * g_star
    is_e = (lane >= lo_lane) & (lane < lo_lane + EXPERTS_PER_GROUP)
    emax = jnp.max(jnp.where(is_e, logits, ninf), axis=-1, keepdims=True)
    ex = jnp.where(is_e, jnp.exp(logits - emax), 0.0)
    pe = ex / jnp.sum(ex, axis=-1, keepdims=True)
    pe_m = jnp.where(is_e, pe, ninf)
    v1 = jnp.max(pe_m, axis=-1, keepdims=True)
    l1 = jnp.min(jnp.where(pe_m == v1, lane, float(LANES)), axis=-1, keepdims=True)
    pe_m2 = jnp.where(lane == l1, ninf, pe_m)
    v2 = jnp.max(pe_m2, axis=-1, keepdims=True)
    l2 = jnp.min(jnp.where(pe_m2 == v2, lane, float(LANES)), axis=-1, keepdims=True)
    vs = v1 + v2
    w1 = p_grp * v1 / vs
    w2 = p_grp * v2 / vs

    hit1 = lane == l1
    hit2 = lane == l2
    onehot = jnp.where(hit1 | hit2, 1.0, 0.0)
    r_i = lax.broadcasted_iota(I32, (tm, tm), 0)
    c_i = lax.broadcasted_iota(I32, (tm, tm), 1)
    tri = jnp.where(c_i < r_i, 1.0, 0.0).astype(BF16)
    before = jnp.dot(tri, onehot.astype(BF16), preferred_element_type=F32) + cnt_ref[0:1, :]
    rank1 = jnp.sum(jnp.where(hit1, before, 0.0), axis=-1, keepdims=True)
    rank2 = jnp.sum(jnp.where(hit2, before, 0.0), axis=-1, keepdims=True)
    cnt_ref[...] = cnt_ref[...] + jnp.sum(onehot, axis=0, keepdims=True)

    e1 = l1 - N_GROUPS
    e2 = l2 - N_GROUPS
    info = jnp.where(lane == 0, e1, jnp.where(lane == 1, e2, jnp.where(lane == 2, w1, jnp.where(
        lane == 3, w2, jnp.where(lane == 4, rank1, jnp.where(lane == 5, rank2, 0.0))))))
    info_ref[...] = info
    info_t_ref[...] = info.T[0:8, :]


def _row_copy(src_ref, src_row, dst_ref, dst_row, sem):
    return pltpu.make_async_copy(src_ref.at[pl.ds(src_row, 1), :], dst_ref.at[pl.ds(dst_row, 1), :], sem)


def _dispatch_kernel(meta_ref, dest_ref, x_ref, xs_ref, zeros, sem, zsem):
    tt = x_ref.shape[0]
    n_blocks = xs_ref.shape[0] // MOE_BLOCK

    def zero_block(row0):
        return pltpu.make_async_copy(zeros, xs_ref.at[pl.ds(row0, MOE_BLOCK), :], zsem)

    @pl.when(pl.program_id(0) == 0)
    def _():
        zeros[...] = jnp.zeros_like(zeros)
        n_used = meta_ref[N_EXPERTS]

        def over_fill_targets(fn):
            for e in range(N_EXPERTS):
                @pl.when(meta_ref[e] >= 0)
                def _():
                    fn(pl.multiple_of(meta_ref[e], MOE_BLOCK))

            def trailing(j, c):
                fn(pl.multiple_of(j * MOE_BLOCK, MOE_BLOCK))
                return c

            lax.fori_loop(n_used, n_blocks, trailing, 0)

        over_fill_targets(lambda row0: zero_block(row0).start())
        over_fill_targets(lambda row0: zero_block(row0).wait())

    def issue(i, c):
        for k in range(TOP_K):
            _row_copy(x_ref, i, xs_ref, dest_ref[0, 0, k * tt + i], sem).start(priority=k)
        return c

    lax.fori_loop(0, tt, issue, 0, unroll=8)
    for _ in range(TOP_K):
        pltpu.make_async_copy(x_ref, xs_ref.at[pl.ds(0, tt), :], sem).wait()


def _moe_dispatch(x, dest3, meta, n_rows):
    t, d = x.shape
    tt = dest3.shape[2] // TOP_K
    return pl.pallas_call(
        _dispatch_kernel,
        grid_spec=pltpu.PrefetchScalarGridSpec(
            num_scalar_prefetch=1, grid=(t // tt,),
            in_specs=[pl.BlockSpec((1, 1, TOP_K * tt), lambda i, m: (i, 0, 0), memory_space=pltpu.SMEM),
                      pl.BlockSpec((tt, d), lambda i, m: (i, 0))],
            out_specs=pl.BlockSpec(memory_space=pl.ANY),
            scratch_shapes=[pltpu.VMEM((MOE_BLOCK, d), F32), pltpu.SemaphoreType.DMA(()),
                            pltpu.SemaphoreType.DMA(())]),
        out_shape=jax.ShapeDtypeStruct((n_rows, d), F32),
        compiler_params=_params("arbitrary"),
    )(meta, dest3, x)


def _expert_kernel(be_ref, xs_ref, wg_ref, wu_ref, wd_ref, ys_ref, wg_bf, wu_bf, wd_bf):
    i = pl.program_id(0)
    n_used = be_ref[pl.num_programs(0)]

    @pl.when((i == 0) | (be_ref[i] != be_ref[jnp.maximum(i - 1, 0)]))
    def _():
        wg_bf[...] = wg_ref[0].astype(BF16)
        wu_bf[...] = wu_ref[0].astype(BF16)
        wd_bf[...] = wd_ref[0].astype(BF16)

    @pl.when(i < n_used)
    def _():
        xb = xs_ref[...].astype(BF16)
        gate = jnp.dot(xb, wg_bf[...], preferred_element_type=F32)
        up = jnp.dot(xb, wu_bf[...], preferred_element_type=F32)
        h = gate * jax.nn.sigmoid(gate) * up
        ys_ref[...] = jnp.dot(h.astype(BF16), wd_bf[...], preferred_element_type=F32)

    @pl.when(i >= n_used)
    def _():
        ys_ref[...] = jnp.zeros_like(ys_ref)


def _moe_experts(xs, block_e, wg, wu, wd):
    p, d = xs.shape
    ff = wg.shape[2]
    nb = p // MOE_BLOCK
    return pl.pallas_call(
        _expert_kernel,
        grid_spec=pltpu.PrefetchScalarGridSpec(
            num_scalar_prefetch=1, grid=(nb,),
            in_specs=[pl.BlockSpec((MOE_BLOCK, d), lambda i, be: (i, 0)),
                      pl.BlockSpec((1, d, ff), lambda i, be: (be[i], 0, 0)),
                      pl.BlockSpec((1, d, ff), lambda i, be: (be[i], 0, 0)),
                      pl.BlockSpec((1, ff, d), lambda i, be: (be[i], 0, 0))],
            out_specs=pl.BlockSpec((MOE_BLOCK, d), lambda i, be: (i, 0)),
            scratch_shapes=[pltpu.VMEM((d, ff), BF16), pltpu.VMEM((d, ff), BF16), pltpu.VMEM((ff, d), BF16)]),
        out_shape=jax.ShapeDtypeStruct((p, d), F32),
        compiler_params=_params("arbitrary"),
    )(block_e, xs, wg, wu, wd)


def _combine_ln_kernel(dest_ref, dest_next_ref, x_ref, info_ref, g_ref, b_ref, ys_ref, o_ref, buf, sem):
    tt = x_ref.shape[0]
    i = pl.program_id(0)
    slot = i % 2

    def issue_tile(d_ref, s):
        def issue(r, c):
            for k in range(TOP_K):
                _row_copy(ys_ref, d_ref[0, 0, k * tt + r], buf.at[s], k * tt + r, sem.at[s]).start(priority=k)
            return c

        lax.fori_loop(0, tt, issue, 0, unroll=8)

    @pl.when(i == 0)
    def _():
        issue_tile(dest_ref, 0)

    @pl.when(i + 1 < pl.num_programs(0))
    def _():
        issue_tile(dest_next_ref, 1 - slot)

    pltpu.make_async_copy(ys_ref.at[pl.ds(0, TOP_K * tt), :], buf.at[slot], sem.at[slot]).wait()
    rows = buf[slot]
    y = info_ref[:, 2:3] * rows[0:tt, :] + info_ref[:, 3:4] * rows[tt:2 * tt, :]
    o_ref[...] = _layernorm(DN_ALPHA * x_ref[...] + y, g_ref[...], b_ref[...])


def _moe_combine_ln(x, info, dest3, ys, g, b):
    t, d = x.shape
    tt = dest3.shape[2] // TOP_K
    n = t // tt
    return pl.pallas_call(
        _combine_ln_kernel,
        grid=(n,),
        in_specs=[pl.BlockSpec((1, 1, TOP_K * tt), lambda i: (i, 0, 0), memory_space=pltpu.SMEM),
                  pl.BlockSpec((1, 1, TOP_K * tt), lambda i: (jnp.minimum(i + 1, n - 1), 0, 0),
                               memory_space=pltpu.SMEM),
                  pl.BlockSpec((tt, d), lambda i: (i, 0)),
                  pl.BlockSpec((tt, LANES), lambda i: (i, 0)),
                  pl.BlockSpec((1, d), lambda i: (0, 0)),
                  pl.BlockSpec((1, d), lambda i: (0, 0)),
                  pl.BlockSpec(memory_space=pl.ANY)],
        out_specs=pl.BlockSpec((tt, d), lambda i: (i, 0)),
        out_shape=jax.ShapeDtypeStruct((t, d), F32),
        scratch_shapes=[pltpu.VMEM((2, TOP_K * tt, d), F32), pltpu.SemaphoreType.DMA((2,))],
        compiler_params=_params("arbitrary"),
    )(dest3, dest3, x, info, g.reshape(1, d), b.reshape(1, d), ys)


def _hier_moe_ln(x, info, info_t, cnt, w_gate, w_up, w_down, ln_g, ln_b):
    t, d = x.shape
    a_total = t * TOP_K
    counts = cnt[0, N_GROUPS:N_GROUPS + N_EXPERTS].astype(I32)
    padded = (counts + MOE_BLOCK - 1) // MOE_BLOCK * MOE_BLOCK
    pad_end = jnp.cumsum(padded)
    pad_start = pad_end - padded
    n_blocks = -(-a_total // MOE_BLOCK) + N_EXPERTS
    e = info_t[0:TOP_K].astype(I32)
    rank = info_t[4:4 + TOP_K].astype(I32)
    expert_ids = jnp.arange(N_EXPERTS, dtype=I32)[:, None, None]
    dest = rank + jnp.sum(jnp.where(e[None] == expert_ids, pad_start[:, None, None], 0), axis=0)
    blk_start = jnp.arange(n_blocks, dtype=I32) * MOE_BLOCK
    block_e = jnp.minimum(jnp.sum((pad_end[None, :] <= blk_start[:, None]).astype(I32), axis=1),
                          N_EXPERTS - 1).astype(I32)
    tt = min(GATHER_TILE, t)
    dest3 = dest.reshape(TOP_K, t // tt, tt).transpose(1, 0, 2).reshape(t // tt, 1, TOP_K * tt)
    last_blk = jnp.where(counts % MOE_BLOCK != 0, pad_end - MOE_BLOCK, -1)
    meta = jnp.concatenate([last_blk, pad_end[-1:] // MOE_BLOCK]).astype(I32)
    xs = _moe_dispatch(x, dest3, meta, n_blocks * MOE_BLOCK)
    ys = _moe_experts(xs, jnp.concatenate([block_e, meta[N_EXPERTS:]]), w_gate, w_up, w_down)
    return _moe_combine_ln(x, info, dest3, ys, ln_g, ln_b)


def _compress_kernel(x_ref, pe_ref, w1_ref, w2_ref, w2t_ref, o_ref, ot_ref):
    n = o_ref.shape[2]
    dk = HEAD_DIM

    def half(p0):
        acc = None
        for p in range(CMP_STRIDE):
            rows = x_ref[pl.ds(p, n, stride=CMP_STRIDE), :] + pe_ref[p0 + p:p0 + p + 1, :]
            part = jnp.dot(rows.astype(BF16), w1_ref[p0 + p], preferred_element_type=F32)
            acc = part if acc is None else acc + part
        return acc

    h = half(0) + pltpu.roll(half(CMP_STRIDE), n - 1, 0)
    hb = _gelu_tanh(h).astype(BF16)
    out = jnp.dot(hb, w2_ref[...], preferred_element_type=F32)
    out_t = _nt_dot(w2t_ref[...], hb)
    extra = jnp.where(lax.broadcasted_iota(I32, (V_PAD, n), 0) == 0, 1.0, 0.0)
    for j in range(2):
        o_ref[0, j] = out[:, j * dk:(j + 1) * dk].astype(o_ref.dtype)
        ot_ref[0, j] = jnp.concatenate([out_t[j * dk:(j + 1) * dk], extra], axis=0).astype(ot_ref.dtype)


def _compress(x, bsz, seq, pe, w1, w2):
    G, dk = N_KV_GROUPS, HEAD_DIM
    n = seq // CMP_STRIDE
    hid = w1.shape[1]
    z = jnp.zeros((CMP_BLOCK, dk, hid), F32)
    w1p = w1.reshape(CMP_BLOCK, dk, hid)
    w1_pair = jnp.concatenate([jnp.concatenate([w1p, z], axis=2),
                               jnp.concatenate([z, w1p], axis=2)], axis=1).astype(BF16)
    z2 = jnp.zeros((hid, dk), F32)
    w2_pair = jnp.concatenate([jnp.concatenate([w2, z2], axis=1),
                               jnp.concatenate([z2, w2], axis=1)], axis=0).astype(BF16)
    pe_pair = jnp.tile(pe, (1, 2))
    const = lambda a: pl.BlockSpec(a.shape, lambda b, j: (0,) * a.ndim)
    return pl.pallas_call(
        _compress_kernel,
        grid=(bsz, G // 2),
        in_specs=[pl.BlockSpec((seq, 2 * dk), lambda b, j: (b, j)),
                  const(pe_pair), const(w1_pair), const(w2_pair), const(w2_pair.T)],
        out_specs=[pl.BlockSpec((1, 2, n, dk), lambda b, j: (b, j, 0, 0)),
                   pl.BlockSpec((1, 2, dk + V_PAD, n), lambda b, j: (b, j, 0, 0))],
        out_shape=[jax.ShapeDtypeStruct((bsz, G, n, dk), BF16),
                   jax.ShapeDtypeStruct((bsz, G, dk + V_PAD, n), BF16)],
        compiler_params=_params("arbitrary", "arbitrary"),
    )(x, pe_pair, w1_pair, w2_pair, w2_pair.T)


def _softmax_step(carry, s_t, v_t):
    m, acc = carry
    m_new = jnp.maximum(m, jnp.max(s_t, axis=0, keepdims=True))
    p = jnp.exp2(s_t - m_new).astype(BF16)
    acc = jnp.exp2(m - m_new) * acc + jnp.dot(v_t, p, preferred_element_type=F32)
    return m_new, acc


def _softmax_finish(acc):
    return acc[:HEAD_DIM] / jnp.maximum(acc[HEAD_DIM:HEAD_DIM + 1], TINY)


def _tile_heads(a):
    return jnp.concatenate([a] * HEADS_PER_GROUP, axis=1)


def _nsa_kernel(q_ref, kc_ref, vc_ref, ks_ref, vs_ref, kw_ref, vw_ref, gl_ref, ov_ref, o_ref):
    R = HEADS_PER_GROUP
    dk = HEAD_DIM
    tq = q_ref.shape[4] // R
    ncp = kc_ref.shape[2]
    nsl = ov_ref.shape[0]
    seq = kw_ref.shape[2]
    n_top = min(N_SEL, nsl)
    qi = pl.program_id(2)
    qs = qi * tq
    q_t = q_ref[0, 0, 0]
    t_lane = qs + lax.broadcasted_iota(I32, (1, tq), 1)

    init = (jnp.full((1, R * tq), NEG, F32), jnp.zeros((dk + V_PAD, R * tq), F32))

    c_row = lax.broadcasted_iota(I32, (ncp, tq), 0)
    bias_c = jnp.where(c_row * CMP_STRIDE + (CMP_BLOCK - 1) <= t_lane, 0.0, NEG)
    s_t = jnp.dot(kc_ref[0, 0], q_t, preferred_element_type=F32) + _tile_heads(bias_c)
    p = jnp.exp2(s_t - jnp.max(s_t, axis=0, keepdims=True)).astype(BF16)
    acc = jnp.dot(jnp.concatenate([vc_ref[0, 0], ov_ref[...]], axis=0), p, preferred_element_type=F32)
    has_c = jnp.where(t_lane >= CMP_BLOCK - 1, 1.0, 0.0)
    inv_l = _tile_heads(has_c) / jnp.maximum(acc[dk:dk + 1], TINY)
    o_c = acc[:dk] * inv_l

    imp_heads = acc[dk + V_PAD:] * inv_l
    imp = imp_heads[:, 0:tq]
    for r in range(1, R):
        imp = imp + imp_heads[:, r * tq:(r + 1) * tq]
    j_row = lax.broadcasted_iota(I32, (nsl, tq), 0).astype(F32)
    bt = (t_lane >> (SEL_BLOCK.bit_length() - 1)).astype(F32)
    forced = (j_row == 0) | (j_row == bt) | (j_row == bt - 1)
    future = j_row > bt
    work = jnp.where(forced, FORCE, jnp.where(future, -FORCE, imp))
    for _ in range(n_top):
        mx = jnp.max(work, axis=0, keepdims=True)
        cand = jnp.where(work == mx, j_row, float(nsl))
        work = jnp.where(cand == jnp.min(cand, axis=0, keepdims=True), -jnp.inf, work)
    sel_bias = jnp.where((work == -jnp.inf) & jnp.logical_not(future), 0.0, NEG).astype(BF16)
    q_aug = jnp.concatenate([q_t, _tile_heads(sel_bias)], axis=0)

    tw = vw_ref.shape[4]
    nw = WINDOW + tw
    o_w_parts = []
    for h in range(tq // tw):
        cols = [slice(r * tq + h * tw, r * tq + (h + 1) * tw) for r in range(R)]
        q_h = jnp.concatenate([q_t[:, c] for c in cols], axis=1)
        t_h = t_lane[:, h * tw:(h + 1) * tw]
        w0 = pl.multiple_of(jnp.maximum(qs + h * tw - WINDOW, 0), tw)
        wpos = w0 + lax.broadcasted_iota(I32, (nw, tw), 0)
        bias_w = jnp.where((wpos <= t_h) & (wpos > t_h - WINDOW), 0.0, NEG)
        s = jnp.dot(kw_ref[0, 0, pl.ds(w0, nw), :], q_h, preferred_element_type=F32) + _tile_heads(bias_w)
        c0 = w0 // tw
        vw_slab = jnp.concatenate([vw_ref[0, 0, c0 + i] for i in range(nw // tw)], axis=1)
        init_h = (jnp.full((1, R * tw), NEG, F32), jnp.zeros((dk + V_PAD, R * tw), F32))
        o_w_parts.append(_softmax_finish(_softmax_step(init_h, s, vw_slab)[1]))
    o_w = jnp.concatenate([o_w_parts[h][:, r * tw:(r + 1) * tw]
                           for r in range(R) for h in range(tq // tw)], axis=1)

    tk = min(SEL_KEY_TILE, seq)

    def sel_scores(kt):
        k0 = pl.multiple_of(kt * tk, tk)
        s = jnp.dot(ks_ref[0, 0, pl.ds(k0, tk), :], q_aug, preferred_element_type=F32)
        return s, vs_ref[0, 0, kt], k0

    def sel_step(kt, carry):
        s, v_t, _ = sel_scores(kt)
        return _softmax_step(carry, s, v_t)

    def sel_pair(i, carry):
        return sel_step(2 * i + 1, sel_step(2 * i, carry))

    n_full = qs // tk
    carry = lax.fori_loop(0, n_full // 2, sel_pair, init)
    carry = lax.cond(n_full % 2 == 1, lambda c: sel_step(n_full - 1, c), lambda c: c, carry)
    s, v_t, k0 = sel_scores(n_full)
    kpos = k0 + lax.broadcasted_iota(I32, (tk, tq), 0)
    s = s + _tile_heads(jnp.where(kpos <= t_lane, 0.0, NEG))
    o_s = _softmax_finish(_softmax_step(carry, s, v_t)[1])

    gates = jax.nn.sigmoid(gl_ref[0, 0, 0])
    o = gates[0:1, :] * o_c + gates[1:2, :] * o_s + gates[2:3, :] * o_w
    o_ref[0] = jnp.concatenate([o[:, r * tq:(r + 1) * tq].T for r in range(R)], axis=1).astype(o_ref.dtype)


def _nsa_attention(q_t, kcmp, vcmp_t, ks_aug, vs_t, kw, vw_t, gl_t):
    bsz, g, nq, dk, rq = q_t.shape
    seq = kw.shape[2]
    ncp = kcmp.shape[2]
    nsl = seq // SEL_BLOCK
    assert seq >= WINDOW + rq // HEADS_PER_GROUP and seq % min(SEL_KEY_TILE, seq) == 0
    c_start = jnp.arange(ncp) * CMP_STRIDE
    j_start = jnp.arange(nsl) * SEL_BLOCK
    overlap_t = ((c_start[None, :] < j_start[:, None] + SEL_BLOCK)
                 & (c_start[None, :] + CMP_BLOCK > j_start[:, None])).astype(BF16)
    per_bg = lambda shape: pl.BlockSpec((1, 1) + shape, lambda b, j, i: (b, j) + (0,) * len(shape))
    per_q = lambda shape: pl.BlockSpec((1, 1, 1) + shape, lambda b, j, i: (b, j, i, 0, 0))
    tk = min(SEL_KEY_TILE, seq)
    tq = rq // HEADS_PER_GROUP
    return pl.pallas_call(
        _nsa_kernel,
        grid=(bsz, g, nq),
        in_specs=[per_q((dk, rq)),
                  per_bg((ncp, dk)), per_bg((dk + V_PAD, ncp)),
                  per_bg((seq, dk + nsl)), per_bg((seq // tk, dk + V_PAD, tk)),
                  per_bg((seq, dk)), per_bg(vw_t.shape[2:]),
                  per_q((3, rq)),
                  pl.BlockSpec((nsl, ncp), lambda b, j, i: (0, 0))],
        out_specs=pl.BlockSpec((1, tq, HEADS_PER_GROUP * dk), lambda b, j, i: (b, i, j)),
        out_shape=jax.ShapeDtypeStruct((bsz, seq, g * HEADS_PER_GROUP * dk), BF16),
        compiler_params=_params("arbitrary", "arbitrary", "arbitrary"),
    )(q_t, kcmp, vcmp_t, ks_aug, vs_t, kw, vw_t, gl_t, overlap_t)


def _rope_rows(v, cos, sin):
    w = v.shape[1]
    lane = lax.broadcasted_iota(I32, v.shape, 1)
    first_half = (lane & (HEAD_DIM // 2)) == 0
    partner = jnp.where(first_half, pltpu.roll(v, w - HEAD_DIM // 2, 1), pltpu.roll(v, HEAD_DIM // 2, 1))
    return v * cos + partner * sin


def _nsa_proj_kernel(x_ref, wq_ref, wk_ref, wv_ref, wc_ref, wg_ref, cq_ref, sq_ref, ck_ref, sk_ref,
                     q_ref, ks_ref, kw_ref, kc_ref, vc_ref, vs_ref, vw_ref, gl_ref, *, row_tiles_per_seq):
    G, R, dk = N_KV_GROUPS, HEADS_PER_GROUP, HEAD_DIM
    tm = x_ref.shape[0]
    tq = q_ref.shape[4] // R
    tk = vs_ref.shape[4]
    nsl = ks_ref.shape[3] - dk
    xb = x_ref[...].astype(BF16)

    q_t = _nt_dot(wq_ref[...], xb).reshape(N_HEADS, dk, tm)
    partner = jnp.concatenate([q_t[:, dk // 2:], q_t[:, :dk // 2]], axis=1)
    q_t = (q_t * cq_ref[...][None] + partner * sq_ref[...][None]).astype(BF16)
    for g in range(G):
        for h in range(tm // tq):
            q_ref[0, g, h] = jnp.concatenate(
                [q_t[g * R + r][:, h * tq:(h + 1) * tq] for r in range(R)], axis=1)

    kk = jnp.dot(xb, wk_ref[...], preferred_element_type=F32)
    cos_k, sin_k = ck_ref[...], sk_ref[...]
    k_sel = _rope_rows(kk[:, :KV_WIDTH], cos_k, sin_k)
    k_win = _rope_rows(kk[:, KV_WIDTH:2 * KV_WIDTH], cos_k, sin_k)
    kc_ref[...] = _rope_rows(kk[:, 2 * KV_WIDTH:], cos_k, sin_k)
    pos = (pl.program_id(0) % row_tiles_per_seq) * tm + lax.broadcasted_iota(I32, (tm, nsl), 0)
    block_onehot = jnp.where((pos >> (SEL_BLOCK.bit_length() - 1))
                             == lax.broadcasted_iota(I32, (tm, nsl), 1), 1.0, 0.0)
    for g in range(G):
        ks_ref[0, g] = jnp.concatenate([k_sel[:, g * dk:(g + 1) * dk], block_onehot],
                                       axis=1).astype(BF16)
        kw_ref[0, g] = k_win[:, g * dk:(g + 1) * dk].astype(BF16)

    v_t = _nt_dot(wv_ref[...], xb)
    extra = jnp.where(lax.broadcasted_iota(I32, (V_PAD, tm), 0) == 0, 1.0, 0.0)
    for g in range(G):
        vs = jnp.concatenate([v_t[g * dk:(g + 1) * dk], extra], axis=0).astype(BF16)
        for c in range(tm // tk):
            vs_ref[0, g, c] = vs[:, c * tk:(c + 1) * tk]
        vw = jnp.concatenate([v_t[KV_WIDTH + g * dk:KV_WIDTH + (g + 1) * dk], extra],
                             axis=0).astype(BF16)
        tw = vw_ref.shape[4]
        for h in range(tm // tw):
            vw_ref[0, g, h] = vw[:, h * tw:(h + 1) * tw]

    vc_ref[...] = jnp.dot(xb, wc_ref[...], preferred_element_type=F32)

    gl = _nt_dot(wg_ref[...], xb)
    for g in range(G):
        for h in range(tm // tq):
            gl_ref[0, g, h] = jnp.concatenate(
                [jnp.concatenate([gl[(g * 3 + i) * R + r:(g * 3 + i) * R + r + 1, h * tq:(h + 1) * tq]
                                  for r in range(R)], axis=1) for i in range(3)], axis=0)


def _nsa_project(x, bsz, seq, w_in):
    G, R, dk = N_KV_GROUPS, HEADS_PER_GROUP, HEAD_DIM
    t, d = x.shape
    qd = N_HEADS * dk
    tm = min(ROW_TILE, seq)
    tq = min(Q_TILE, seq)
    tk = min(SEL_KEY_TILE, seq)
    tw = min(WIN_Q_TILE, tq)
    nsl = seq // SEL_BLOCK
    ns = seq // tm
    assert tm % tq == 0 and tm % tk == 0 and tq % tw == 0
    cut = [qd + j * KV_WIDTH for j in range(7)]
    w = w_in.astype(BF16)
    w_q, w_kc, w_vc, w_ks, w_vs, w_kw, w_vw, w_gl = (
        w[:, :qd], w[:, cut[0]:cut[1]], w[:, cut[1]:cut[2]], w[:, cut[2]:cut[3]],
        w[:, cut[3]:cut[4]], w[:, cut[4]:cut[5]], w[:, cut[5]:cut[6]], w[:, cut[6]:])
    wq_t = w_q.T
    wk = jnp.concatenate([w_ks, w_kw, w_kc], axis=1)
    wv_t = jnp.concatenate([w_vs, w_vw], axis=1).T
    wc = w_vc
    wg_t = w_gl.reshape(d, G, R, 3).transpose(1, 3, 2, 0).reshape(G * 3 * R, d)
    wg_t = jnp.concatenate([wg_t, jnp.zeros((LANES - G * 3 * R, d), BF16)], axis=0)
    pos = jnp.arange(seq, dtype=F32)
    inv = ROPE_THETA ** (-jnp.arange(0, dk, 2, dtype=F32) / dk)
    ang = pos[:, None] * inv[None, :]
    cos, sin = jnp.cos(ang), jnp.sin(ang)
    cos_h = jnp.concatenate([cos, cos], axis=1)
    sin_h = jnp.concatenate([-sin, sin], axis=1)
    cq, sq = (cos_h * QK_SCALE).T, (sin_h * QK_SCALE).T
    ck, sk = jnp.tile(cos_h, (1, G)), jnp.tile(sin_h, (1, G))
    full = lambda a: pl.BlockSpec(a.shape, lambda i: (0,) * a.ndim)
    b_of = lambda i: i // ns
    s_of = lambda i: i % ns
    out_shape = [jax.ShapeDtypeStruct((bsz, G, seq // tq, dk, R * tq), BF16),
                 jax.ShapeDtypeStruct((bsz, G, seq, dk + nsl), BF16),
                 jax.ShapeDtypeStruct((bsz, G, seq, dk), BF16),
                 jax.ShapeDtypeStruct((t, KV_WIDTH), F32),
                 jax.ShapeDtypeStruct((t, KV_WIDTH), F32),
                 jax.ShapeDtypeStruct((bsz, G, seq // tk, dk + V_PAD, tk), BF16),
                 jax.ShapeDtypeStruct((bsz, G, seq // tw, dk + V_PAD, tw), BF16),
                 jax.ShapeDtypeStruct((bsz, G, seq // tq, 3, R * tq), F32)]
    out_specs = [pl.BlockSpec((1, G, tm // tq, dk, R * tq), lambda i: (b_of(i), 0, s_of(i), 0, 0)),
                 pl.BlockSpec((1, G, tm, dk + nsl), lambda i: (b_of(i), 0, s_of(i), 0)),
                 pl.BlockSpec((1, G, tm, dk), lambda i: (b_of(i), 0, s_of(i), 0)),
                 pl.BlockSpec((tm, KV_WIDTH), lambda i: (i, 0)),
                 pl.BlockSpec((tm, KV_WIDTH), lambda i: (i, 0)),
                 pl.BlockSpec((1, G, tm // tk, dk + V_PAD, tk), lambda i: (b_of(i), 0, s_of(i), 0, 0)),
                 pl.BlockSpec((1, G, tm // tw, dk + V_PAD, tw), lambda i: (b_of(i), 0, s_of(i), 0, 0)),
                 pl.BlockSpec((1, G, tm // tq, 3, R * tq), lambda i: (b_of(i), 0, s_of(i), 0, 0))]
    return pl.pallas_call(
        functools.partial(_nsa_proj_kernel, row_tiles_per_seq=ns),
        grid=(t // tm,),
        in_specs=[pl.BlockSpec((tm, d), lambda i: (i, 0)),
                  full(wq_t), full(wk), full(wv_t), full(wc), full(wg_t),
                  pl.BlockSpec((dk, tm), lambda i: (0, s_of(i))),
                  pl.BlockSpec((dk, tm), lambda i: (0, s_of(i))),
                  pl.BlockSpec((tm, KV_WIDTH), lambda i: (s_of(i), 0)),
                  pl.BlockSpec((tm, KV_WIDTH), lambda i: (s_of(i), 0))],
        out_specs=out_specs, out_shape=out_shape,
        compiler_params=_params("arbitrary"),
    )(x, wq_t, wk, wv_t, wc, wg_t, cq, sq, ck, sk)


def _nsa_mixer_heads(x, bsz, seq, w_in, cmp_k_pe, cmp_k_w1, cmp_k_w2, cmp_v_pe, cmp_v_w1, cmp_v_w2):
    q_t, ks_aug, kw, kc, vc, vs_t, vw_t, gl_t = _nsa_project(x, bsz, seq, w_in)
    kcmp, _ = _compress(kc, bsz, seq, cmp_k_pe, cmp_k_w1, cmp_k_w2)
    _, vcmp_t = _compress(vc, bsz, seq, cmp_v_pe, cmp_v_w1, cmp_v_w2)
    o = _nsa_attention(q_t, kcmp, vcmp_t, ks_aug, vs_t, kw, vw_t, gl_t)
    return o.reshape(bsz * seq, N_HEADS * HEAD_DIM)


def _even_mixer_cat(x, bsz, seq, w_in, conv_w, lru_conv_w, lru_conv_b, w_a, b_a, w_x, b_x, lam):
    wa_bd = jax.scipy.linalg.block_diag(*w_a).astype(BF16)
    wx_bd = jax.scipy.linalg.block_diag(*w_x).astype(BF16)
    return _l0_mixer(x, w_in.astype(BF16), bsz, seq, conv_w, lru_conv_w, lru_conv_b, wa_bd, b_a, wx_bd, b_x, lam)


def kernel(x, l0_w_in, l0_conv_w, l0_lru_conv_w, l0_lru_conv_b, l0_lru_w_a, l0_lru_b_a, l0_lru_w_x, l0_lru_b_x, l0_lru_lambda, l0_w_out, l0_ln1_g, l0_ln1_b, l0_router_group_w, l0_router_group_b, l0_router_expert_w, l0_router_expert_b, l0_expert_w_gate, l0_expert_w_up, l0_expert_w_down, l0_ln2_g, l0_ln2_b, l1_w_in, l1_cmp_k_pe, l1_cmp_k_w1, l1_cmp_k_w2, l1_cmp_v_pe, l1_cmp_v_w1, l1_cmp_v_w2, l1_w_out, l1_ln1_g, l1_ln1_b, l1_router_group_w, l1_router_group_b, l1_router_expert_w, l1_router_expert_b, l1_expert_w_gate, l1_expert_w_up, l1_expert_w_down, l1_ln2_g, l1_ln2_b):
    bsz, seq, d = x.shape
    h = x.reshape(bsz * seq, d)
    y = _even_mixer_cat(h, bsz, seq, l0_w_in, l0_conv_w, l0_lru_conv_w, l0_lru_conv_b, l0_lru_w_a,
                        l0_lru_b_a, l0_lru_w_x, l0_lru_b_x, l0_lru_lambda)
    h, *routing = _matmul_residual_ln_route(
        y, l0_w_out.astype(BF16), h, l0_ln1_g, l0_ln1_b,
        (l0_router_group_w, l0_router_group_b, l0_router_expert_w, l0_router_expert_b))
    h = _hier_moe_ln(h, *routing, l0_expert_w_gate, l0_expert_w_up, l0_expert_w_down, l0_ln2_g, l0_ln2_b)
    y = _nsa_mixer_heads(h, bsz, seq, l1_w_in, l1_cmp_k_pe, l1_cmp_k_w1, l1_cmp_k_w2, l1_cmp_v_pe,
                         l1_cmp_v_w1, l1_cmp_v_w2)
    h, *routing = _matmul_residual_ln_route(
        y, l1_w_out.astype(BF16), h, l1_ln1_g, l1_ln1_b,
        (l1_router_group_w, l1_router_group_b, l1_router_expert_w, l1_router_expert_b))
    h = _hier_moe_ln(h, *routing, l1_expert_w_gate, l1_expert_w_up, l1_expert_w_down, l1_ln2_g, l1_ln2_b)
    return h.reshape(bsz, seq, d)
```

```python
import functools

import jax
import jax.numpy as jnp
from jax import lax
from jax.experimental import pallas as pl
from jax.experimental.pallas import tpu as pltpu

F32 = jnp.float32
BF16 = jnp.bfloat16
I32 = jnp.int32

DEPTH = 2
DN_ALPHA = (2 * DEPTH) ** 0.25
LN_EPS = 1e-5
CONV_WIDTH = 512
CONV_TAPS = 3
LRU_WIDTH = 512
LRU_CONV_TAPS = 4
LRU_C = 8.0
N_HEADS = 16
HEAD_DIM = 64
N_KV_GROUPS = 4
HEADS_PER_GROUP = N_HEADS // N_KV_GROUPS
KV_WIDTH = N_KV_GROUPS * HEAD_DIM
CMP_STRIDE = 16
CMP_BLOCK = 32
SEL_BLOCK = 64
N_SEL = 16
WINDOW = 512
ROPE_THETA = 10000.0
FORCE = 1e4
NEG = -1e30
TINY = 1e-30
QK_SCALE = HEAD_DIM ** -0.5 * 1.4426950408889634
N_GROUPS = 4
EXPERTS_PER_GROUP = 8
N_EXPERTS = N_GROUPS * EXPERTS_PER_GROUP
TOP_K = 2

LANES = 128
VMEM_LIMIT_BYTES = 48 * 1024 * 1024
ROW_TILE = 512
SCAN_CHUNK = 128
MIXER_SUBCHUNKS = 4
GATHER_TILE = 512
MOE_BLOCK = 512
Q_TILE = 512
WIN_Q_TILE = 256
DIAG_Q_TILE = 256
SEL_KEY_TILE = 512
V_PAD = 16


def _params(*sem):
    return pltpu.CompilerParams(dimension_semantics=sem, vmem_limit_bytes=VMEM_LIMIT_BYTES)


def _layernorm(v, g, b):
    mu = jnp.mean(v, axis=-1, keepdims=True)
    d = v - mu
    var = jnp.mean(d * d, axis=-1, keepdims=True)
    return d * lax.rsqrt(var + LN_EPS) * g + b


def _gelu_tanh(x):
    return 0.5 * x * (1.0 + jnp.tanh(0.7978845608028654 * (x + 0.044715 * (x * x * x))))


def _nt_dot(a, b):
    return lax.dot_general(a, b, (((1,), (1,)), ((), ())), preferred_element_type=F32)


def _router_operands(w_rg, b_rg, w_re, b_re):
    d = w_rg.shape[0]
    npad = LANES - N_GROUPS - N_EXPERTS
    w = jnp.concatenate([w_rg, w_re, jnp.zeros((d, npad), F32)], axis=1)
    wh = w.astype(BF16)
    wl = (w - wh.astype(F32)).astype(BF16)
    b = jnp.concatenate([b_rg, b_re, jnp.zeros((npad,), F32)]).reshape(1, LANES)
    return wh, wl, b


def _mm_res_ln_route_kernel(y_ref, w_ref, x_ref, g_ref, b_ref, wh_ref, wl_ref, rb_ref,
                            o_ref, info_ref, info_t_ref, cnt_ref):
    acc = jnp.dot(y_ref[...].astype(BF16), w_ref[...], preferred_element_type=F32)
    out = _layernorm(DN_ALPHA * x_ref[...] + acc, g_ref[...], b_ref[...])
    o_ref[...] = out
    _route_tile(out, wh_ref, wl_ref, rb_ref, info_ref, info_t_ref, cnt_ref)


def _matmul_residual_ln_route(y, w, x, g, b, router):
    m, k = y.shape
    d = w.shape[1]
    tm = min(ROW_TILE, m)
    wh, wl, rb = _router_operands(*router)
    const = lambda a: pl.BlockSpec(a.shape, lambda i: (0,) * a.ndim)
    return pl.pallas_call(
        _mm_res_ln_route_kernel,
        grid=(m // tm,),
        in_specs=[pl.BlockSpec((tm, k), lambda i: (i, 0)),
                  const(w),
                  pl.BlockSpec((tm, d), lambda i: (i, 0)),
                  pl.BlockSpec((1, d), lambda i: (0, 0)),
                  pl.BlockSpec((1, d), lambda i: (0, 0)),
                  const(wh), const(wl), const(rb)],
        out_specs=[pl.BlockSpec((tm, d), lambda i: (i, 0)),
                   pl.BlockSpec((tm, LANES), lambda i: (i, 0)),
                   pl.BlockSpec((8, tm), lambda i: (0, i)),
                   pl.BlockSpec((8, LANES), lambda i: (0, 0))],
        out_shape=[jax.ShapeDtypeStruct((m, d), F32), jax.ShapeDtypeStruct((m, LANES), F32),
                   jax.ShapeDtypeStruct((8, m), F32), jax.ShapeDtypeStruct((8, LANES), F32)],
        compiler_params=_params("arbitrary"),
    )(y, w, x, g.reshape(1, d), b.reshape(1, d), wh, wl, rb)


def _shift_rows(cur, tail, d, row8):
    rolled = pltpu.roll(cur, d, 0)
    head = jnp.where(row8 < d, pltpu.roll(tail, d, 0), rolled[:8])
    return jnp.concatenate([head, rolled[8:]], axis=0)


def _l0_mixer_kernel(x_ref, win_ref, cw_ref, lw_ref, lb_ref, wa_ref, ba_ref, wx_ref, bx_ref, lam_ref,
                     o_ref, tail_c, tail_x, h_state):
    W = CONV_WIDTH
    tc = x_ref.shape[0] // MIXER_SUBCHUNKS

    @pl.when(pl.program_id(1) == 0)
    def _():
        tail_c[...] = jnp.zeros_like(tail_c)
        tail_x[...] = jnp.zeros_like(tail_x)
        h_state[...] = jnp.zeros_like(h_state)

    row8 = lax.broadcasted_iota(I32, (8, W), 0)
    row = lax.broadcasted_iota(I32, (tc, W), 0)
    nl = -lam_ref[...]
    softplus = jnp.maximum(nl, 0.0) + jnp.log1p(jnp.exp(-jnp.abs(nl)))
    zs = [jnp.dot(x_ref[k * tc:(k + 1) * tc, :].astype(BF16), win_ref[...], preferred_element_type=F32)
          for k in range(MIXER_SUBCHUNKS)]
    tc_prev, tx_prev, h_prev = tail_c[...], tail_x[...], h_state[0:1, :]

    for k, z in enumerate(zs):
        rows = slice(k * tc, (k + 1) * tc)
        ch = z[:, W:2 * W] * z[:, 2 * W:3 * W]
        conv = ch * cw_ref[CONV_TAPS - 1:CONV_TAPS, :]
        for d in range(1, CONV_TAPS):
            conv = conv + _shift_rows(ch, tc_prev, d, row8) * cw_ref[CONV_TAPS - 1 - d:CONV_TAPS - d, :]
        tc_prev = ch[tc - 8:, :]
        o_ref[rows, :W] = (z[:, :W] * conv).astype(o_ref.dtype)

        xl = z[:, 4 * W:5 * W]
        xc = xl * lw_ref[LRU_CONV_TAPS - 1:LRU_CONV_TAPS, :]
        for d in range(1, LRU_CONV_TAPS):
            xc = xc + _shift_rows(xl, tx_prev, d, row8) * lw_ref[LRU_CONV_TAPS - 1 - d:LRU_CONV_TAPS - d, :]
        xc = xc + lb_ref[...]
        tx_prev = xl[tc - 8:, :]

        xcb = xc.astype(BF16)
        r = jax.nn.sigmoid(jnp.dot(xcb, wa_ref[...], preferred_element_type=F32) + ba_ref[...])
        ig = jax.nn.sigmoid(jnp.dot(xcb, wx_ref[...], preferred_element_type=F32) + bx_ref[...])
        log_a = -LRU_C * r * softplus
        a = jnp.exp(log_a)
        mult = jnp.sqrt(jnp.maximum(1.0 - a * a, 0.0))
        u = mult * ig * xc

        d = 1
        while d < tc:
            keep = row >= d
            a_s = jnp.where(keep, pltpu.roll(a, d, 0), 1.0)
            u_s = jnp.where(keep, pltpu.roll(u, d, 0), 0.0)
            u = a * u_s + u
            a = a * a_s
            d *= 2
        h = a * h_prev + u
        h_prev = h[tc - 1:tc, :]
        o_ref[rows, W:] = (h * _gelu_tanh(z[:, 3 * W:4 * W])).astype(o_ref.dtype)

    tail_c[...] = tc_prev
    tail_x[...] = tx_prev
    h_state[...] = jnp.broadcast_to(h_prev, h_state.shape)


def _l0_mixer(x, w_in, bsz, seq, conv_w, lru_conv_w, lru_conv_b, wa_bd, b_a, wx_bd, b_x, lam):
    W = CONV_WIDTH
    tc = min(SCAN_CHUNK * MIXER_SUBCHUNKS, seq)
    nchunk = seq // tc
    vec = lambda i, j: (0, 0)
    return pl.pallas_call(
        _l0_mixer_kernel,
        grid=(bsz, nchunk),
        in_specs=[pl.BlockSpec((tc, x.shape[1]), lambda i, j: (i * nchunk + j, 0)),
                  pl.BlockSpec(w_in.shape, vec),
                  pl.BlockSpec((CONV_TAPS, W), vec),
                  pl.BlockSpec((LRU_CONV_TAPS, W), vec),
                  pl.BlockSpec((1, W), vec),
                  pl.BlockSpec((W, W), vec),
                  pl.BlockSpec((1, W), vec),
                  pl.BlockSpec((W, W), vec),
                  pl.BlockSpec((1, W), vec),
                  pl.BlockSpec((1, W), vec)],
        out_specs=pl.BlockSpec((tc, 2 * W), lambda i, j: (i * nchunk + j, 0)),
        out_shape=jax.ShapeDtypeStruct((bsz * seq, 2 * W), BF16),
        scratch_shapes=[pltpu.VMEM((8, W), F32), pltpu.VMEM((8, W), F32), pltpu.VMEM((8, W), F32)],
        compiler_params=_params("arbitrary", "arbitrary"),
    )(x, w_in, conv_w, lru_conv_w, lru_conv_b.reshape(1, W), wa_bd, b_a.reshape(1, W), wx_bd,
      b_x.reshape(1, W), lam.reshape(1, W))


def _split_bf16(v):
    hi = v.astype(BF16)
    lo = (v - hi.astype(F32)).astype(BF16)
    return hi, lo


def _route_tile(x, wh_ref, wl_ref, b_ref, info_ref, info_t_ref, cnt_ref):
    tm = x.shape[0]

    @pl.when(pl.program_id(0) == 0)
    def _():
        cnt_ref[...] = jnp.zeros_like(cnt_ref)

    xh, xl = _split_bf16(x)
    logits = (jnp.dot(xh, wh_ref[...], preferred_element_type=F32)
              + jnp.dot(xl, wh_ref[...], preferred_element_type=F32)
              + jnp.dot(xh, wl_ref[...], preferred_element_type=F32)) + b_ref[...]
    lane = lax.broadcasted_iota(I32, logits.shape, 1).astype(F32)
    ninf = -jnp.inf

    is_g = lane < N_GROUPS
    gmax = jnp.max(jnp.where(is_g, logits, ninf), axis=-1, keepdims=True)
    g_star = jnp.min(jnp.where(is_g & (logits == gmax), lane, float(LANES)), axis=-1, keepdims=True)
    gsum = jnp.sum(jnp.where(is_g, jnp.exp(logits - gmax), 0.0), axis=-1, keepdims=True)
    p_grp = 1.0 / gsum

    lo_lane = N_GROUPS + EXPERTS_PER_GROUP * g_star
    is_e = (lane >= lo_lane) & (lane < lo_lane + EXPERTS_PER_GROUP)
    emax = jnp.max(jnp.where(is_e, logits, ninf), axis=-1, keepdims=True)
    ex = jnp.where(is_e, jnp.exp(logits - emax), 0.0)
    pe = ex / jnp.sum(ex, axis=-1, keepdims=True)
    pe_m = jnp.where(is_e, pe, ninf)
    v1 = jnp.max(pe_m, axis=-1, keepdims=True)
    l1 = jnp.min(jnp.where(pe_m == v1, lane, float(LANES)), axis=-1, keepdims=True)
    pe_m2 = jnp.where(lane == l1, ninf, pe_m)
    v2 = jnp.max(pe_m2, axis=-1, keepdims=True)
    l2 = jnp.min(jnp.where(pe_m2 == v2, lane, float(LANES)), axis=-1, keepdims=True)
    vs = v1 + v2
    w1 = p_grp * v1 / vs
    w2 = p_grp * v2 / vs

    hit1 = lane == l1
    hit2 = lane == l2
    onehot = jnp.where(hit1 | hit2, 1.0, 0.0)
    r_i = lax.broadcasted_iota(I32, (tm, tm), 0)
    c_i = lax.broadcasted_iota(I32, (tm, tm), 1)
    tri = jnp.where(c_i < r_i, 1.0, 0.0).astype(BF16)
    before = jnp.dot(tri, onehot.astype(BF16), preferred_element_type=F32) + cnt_ref[0:1, :]
    rank1 = jnp.sum(jnp.where(hit1, before, 0.0), axis=-1, keepdims=True)
    rank2 = jnp.sum(jnp.where(hit2, before, 0.0), axis=-1, keepdims=True)
    cnt_ref[...] = cnt_ref[...] + jnp.sum(onehot, axis=0, keepdims=True)

    e1 = l1 - N_GROUPS
    e2 = l2 - N_GROUPS
    info = jnp.where(lane == 0, e1, jnp.where(lane == 1, e2, jnp.where(lane == 2, w1, jnp.where(
        lane == 3, w2, jnp.where(lane == 4, rank1, jnp.where(lane == 5, rank2, 0.0))))))
    info_ref[...] = info
    info_t_ref[...] = info.T[0:8, :]


def _row_copy(src_ref, src_row, dst_ref, dst_row, sem):
    return pltpu.make_async_copy(src_ref.at[pl.ds(src_row, 1), :], dst_ref.at[pl.ds(dst_row, 1), :], sem)


def _dispatch_kernel(meta_ref, dest_ref, x_ref, xs_ref, zeros, sem, zsem):
    tt = x_ref.shape[0]
    n_blocks = xs_ref.shape[0] // MOE_BLOCK

    def zero_block(row0):
        return pltpu.make_async_copy(zeros, xs_ref.at[pl.ds(row0, MOE_BLOCK), :], zsem)

    @pl.when(pl.program_id(0) == 0)
    def _():
        zeros[...] = jnp.zeros_like(zeros)
        n_used = meta_ref[N_EXPERTS]

        def over_fill_targets(fn):
            for e in range(N_EXPERTS):
                @pl.when(meta_ref[e] >= 0)
                def _():
                    fn(pl.multiple_of(meta_ref[e], MOE_BLOCK))

            def trailing(j, c):
                fn(pl.multiple_of(j * MOE_BLOCK, MOE_BLOCK))
                return c

            lax.fori_loop(n_used, n_blocks, trailing, 0)

        over_fill_targets(lambda row0: zero_block(row0).start())
        over_fill_targets(lambda row0: zero_block(row0).wait())

    def issue(i, c):
        for k in range(TOP_K):
            _row_copy(x_ref, i, xs_ref, dest_ref[0, 0, k * tt + i], sem).start(priority=k)
        return c

    lax.fori_loop(0, tt, issue, 0, unroll=8)
    for _ in range(TOP_K):
        pltpu.make_async_copy(x_ref, xs_ref.at[pl.ds(0, tt), :], sem).wait()


def _moe_dispatch(x, dest3, meta, n_rows):
    t, d = x.shape
    tt = dest3.shape[2] // TOP_K
    return pl.pallas_call(
        _dispatch_kernel,
        grid_spec=pltpu.PrefetchScalarGridSpec(
            num_scalar_prefetch=1, grid=(t // tt,),
            in_specs=[pl.BlockSpec((1, 1, TOP_K * tt), lambda i, m: (i, 0, 0), memory_space=pltpu.SMEM),
                      pl.BlockSpec((tt, d), lambda i, m: (i, 0))],
            out_specs=pl.BlockSpec(memory_space=pl.ANY),
            scratch_shapes=[pltpu.VMEM((MOE_BLOCK, d), F32), pltpu.SemaphoreType.DMA(()),
                            pltpu.SemaphoreType.DMA(())]),
        out_shape=jax.ShapeDtypeStruct((n_rows, d), F32),
        compiler_params=_params("arbitrary"),
    )(meta, dest3, x)


def _expert_kernel(be_ref, xs_ref, wg_ref, wu_ref, wd_ref, ys_ref, wg_bf, wu_bf, wd_bf):
    i = pl.program_id(0)
    n_used = be_ref[pl.num_programs(0)]

    @pl.when((i == 0) | (be_ref[i] != be_ref[jnp.maximum(i - 1, 0)]))
    def _():
        wg_bf[...] = wg_ref[0].astype(BF16)
        wu_bf[...] = wu_ref[0].astype(BF16)
        wd_bf[...] = wd_ref[0].astype(BF16)

    @pl.when(i < n_used)
    def _():
        xb = xs_ref[...].astype(BF16)
        gate = jnp.dot(xb, wg_bf[...], preferred_element_type=F32)
        up = jnp.dot(xb, wu_bf[...], preferred_element_type=F32)
        h = gate * jax.nn.sigmoid(gate) * up
        ys_ref[...] = jnp.dot(h.astype(BF16), wd_bf[...], preferred_element_type=F32)

    @pl.when(i >= n_used)
    def _():
        ys_ref[...] = jnp.zeros_like(ys_ref)


def _moe_experts(xs, block_e, wg, wu, wd):
    p, d = xs.shape
    ff = wg.shape[2]
    nb = p // MOE_BLOCK
    return pl.pallas_call(
        _expert_kernel,
        grid_spec=pltpu.PrefetchScalarGridSpec(
            num_scalar_prefetch=1, grid=(nb,),
            in_specs=[pl.BlockSpec((MOE_BLOCK, d), lambda i, be: (i, 0)),
                      pl.BlockSpec((1, d, ff), lambda i, be: (be[i], 0, 0)),
                      pl.BlockSpec((1, d, ff), lambda i, be: (be[i], 0, 0)),
                      pl.BlockSpec((1, ff, d), lambda i, be: (be[i], 0, 0))],
            out_specs=pl.BlockSpec((MOE_BLOCK, d), lambda i, be: (i, 0)),
            scratch_shapes=[pltpu.VMEM((d, ff), BF16), pltpu.VMEM((d, ff), BF16), pltpu.VMEM((ff, d), BF16)]),
        out_shape=jax.ShapeDtypeStruct((p, d), F32),
        compiler_params=_params("arbitrary"),
    )(block_e, xs, wg, wu, wd)


def _combine_ln_kernel(dest_ref, dest_next_ref, x_ref, info_ref, g_ref, b_ref, ys_ref, o_ref, buf, sem):
    tt = x_ref.shape[0]
    i = pl.program_id(0)
    slot = i % 2

    def issue_tile(d_ref, s):
        def issue(r, c):
            for k in range(TOP_K):
                _row_copy(ys_ref, d_ref[0, 0, k * tt + r], buf.at[s], k * tt + r, sem.at[s]).start(priority=k)
            return c

        lax.fori_loop(0, tt, issue, 0, unroll=8)

    @pl.when(i == 0)
    def _():
        issue_tile(dest_ref, 0)

    @pl.when(i + 1 < pl.num_programs(0))
    def _():
        issue_tile(dest_next_ref, 1 - slot)

    pltpu.make_async_copy(ys_ref.at[pl.ds(0, TOP_K * tt), :], buf.at[slot], sem.at[slot]).wait()
    rows = buf[slot]
    y = info_ref[:, 2:3] * rows[0:tt, :] + info_ref[:, 3:4] * rows[tt:2 * tt, :]
    o_ref[...] = _layernorm(DN_ALPHA * x_ref[...] + y, g_ref[...], b_ref[...])


def _moe_combine_ln(x, info, dest3, ys, g, b):
    t, d = x.shape
    tt = dest3.shape[2] // TOP_K
    n = t // tt
    return pl.pallas_call(
        _combine_ln_kernel,
        grid=(n,),
        in_specs=[pl.BlockSpec((1, 1, TOP_K * tt), lambda i: (i, 0, 0), memory_space=pltpu.SMEM),
                  pl.BlockSpec((1, 1, TOP_K * tt), lambda i: (jnp.minimum(i + 1, n - 1), 0, 0),
                               memory_space=pltpu.SMEM),
                  pl.BlockSpec((tt, d), lambda i: (i, 0)),
                  pl.BlockSpec((tt, LANES), lambda i: (i, 0)),
                  pl.BlockSpec((1, d), lambda i: (0, 0)),
                  pl.BlockSpec((1, d), lambda i: (0, 0)),
                  pl.BlockSpec(memory_space=pl.ANY)],
        out_specs=pl.BlockSpec((tt, d), lambda i: (i, 0)),
        out_shape=jax.ShapeDtypeStruct((t, d), F32),
        scratch_shapes=[pltpu.VMEM((2, TOP_K * tt, d), F32), pltpu.SemaphoreType.DMA((2,))],
        compiler_params=_params("arbitrary"),
    )(dest3, dest3, x, info, g.reshape(1, d), b.reshape(1, d), ys)


def _hier_moe_ln(x, info, info_t, cnt, w_gate, w_up, w_down, ln_g, ln_b):
    t, d = x.shape
    a_total = t * TOP_K
    counts = cnt[0, N_GROUPS:N_GROUPS + N_EXPERTS].astype(I32)
    padded = (counts + MOE_BLOCK - 1) // MOE_BLOCK * MOE_BLOCK
    pad_end = jnp.cumsum(padded)
    pad_start = pad_end - padded
    n_blocks = -(-a_total // MOE_BLOCK) + N_EXPERTS
    e = info_t[0:TOP_K].astype(I32)
    rank = info_t[4:4 + TOP_K].astype(I32)
    expert_ids = jnp.arange(N_EXPERTS, dtype=I32)[:, None, None]
    dest = rank + jnp.sum(jnp.where(e[None] == expert_ids, pad_start[:, None, None], 0), axis=0)
    blk_start = jnp.arange(n_blocks, dtype=I32) * MOE_BLOCK
    block_e = jnp.minimum(jnp.sum((pad_end[None, :] <= blk_start[:, None]).astype(I32), axis=1),
                          N_EXPERTS - 1).astype(I32)
    tt = min(GATHER_TILE, t)
    dest3 = dest.reshape(TOP_K, t // tt, tt).transpose(1, 0, 2).reshape(t // tt, 1, TOP_K * tt)
    last_blk = jnp.where(counts % MOE_BLOCK != 0, pad_end - MOE_BLOCK, -1)
    meta = jnp.concatenate([last_blk, pad_end[-1:] // MOE_BLOCK]).astype(I32)
    xs = _moe_dispatch(x, dest3, meta, n_blocks * MOE_BLOCK)
    ys = _moe_experts(xs, jnp.concatenate([block_e, meta[N_EXPERTS:]]), w_gate, w_up, w_down)
    return _moe_combine_ln(x, info, dest3, ys, ln_g, ln_b)


def _compress_kernel(x_ref, pe_ref, w1_ref, w2_ref, w2t_ref, o_ref, ot_ref):
    n = o_ref.shape[2]
    dk = HEAD_DIM

    def half(p0):
        acc = None
        for p in range(CMP_STRIDE):
            rows = x_ref[pl.ds(p, n, stride=CMP_STRIDE), :] + pe_ref[p0 + p:p0 + p + 1, :]
            part = jnp.dot(rows.astype(BF16), w1_ref[p0 + p], preferred_element_type=F32)
            acc = part if acc is None else acc + part
        return acc

    h = half(0) + pltpu.roll(half(CMP_STRIDE), n - 1, 0)
    hb = _gelu_tanh(h).astype(BF16)
    out = jnp.dot(hb, w2_ref[...], preferred_element_type=F32)
    out_t = _nt_dot(w2t_ref[...], hb)
    extra = jnp.where(lax.broadcasted_iota(I32, (V_PAD, n), 0) == 0, 1.0, 0.0)
    for j in range(2):
        o_ref[0, j] = out[:, j * dk:(j + 1) * dk].astype(o_ref.dtype)
        ot_ref[0, j] = jnp.concatenate([out_t[j * dk:(j + 1) * dk], extra], axis=0).astype(ot_ref.dtype)


def _compress(x, bsz, seq, pe, w1, w2):
    G, dk = N_KV_GROUPS, HEAD_DIM
    n = seq // CMP_STRIDE
    hid = w1.shape[1]
    z = jnp.zeros((CMP_BLOCK, dk, hid), F32)
    w1p = w1.reshape(CMP_BLOCK, dk, hid)
    w1_pair = jnp.concatenate([jnp.concatenate([w1p, z], axis=2),
                               jnp.concatenate([z, w1p], axis=2)], axis=1).astype(BF16)
    z2 = jnp.zeros((hid, dk), F32)
    w2_pair = jnp.concatenate([jnp.concatenate([w2, z2], axis=1),
                               jnp.concatenate([z2, w2], axis=1)], axis=0).astype(BF16)
    pe_pair = jnp.tile(pe, (1, 2))
    const = lambda a: pl.BlockSpec(a.shape, lambda b, j: (0,) * a.ndim)
    return pl.pallas_call(
        _compress_kernel,
        grid=(bsz, G // 2),
        in_specs=[pl.BlockSpec((seq, 2 * dk), lambda b, j: (b, j)),
                  const(pe_pair), const(w1_pair), const(w2_pair), const(w2_pair.T)],
        out_specs=[pl.BlockSpec((1, 2, n, dk), lambda b, j: (b, j, 0, 0)),
                   pl.BlockSpec((1, 2, dk + V_PAD, n), lambda b, j: (b, j, 0, 0))],
        out_shape=[jax.ShapeDtypeStruct((bsz, G, n, dk), BF16),
                   jax.ShapeDtypeStruct((bsz, G, dk + V_PAD, n), BF16)],
        compiler_params=_params("arbitrary", "arbitrary"),
    )(x, pe_pair, w1_pair, w2_pair, w2_pair.T)


def _softmax_step(carry, s_t, v_t):
    m, acc = carry
    m_new = jnp.maximum(m, jnp.max(s_t, axis=0, keepdims=True))
    p = jnp.exp2(s_t - m_new).astype(BF16)
    acc = jnp.exp2(m - m_new) * acc + jnp.dot(v_t, p, preferred_element_type=F32)
    return m_new, acc


def _softmax_finish(acc):
    return acc[:HEAD_DIM] / jnp.maximum(acc[HEAD_DIM:HEAD_DIM + 1], TINY)


def _tile_heads(a):
    return jnp.concatenate([a] * HEADS_PER_GROUP, axis=1)


def _nsa_kernel(q_ref, kc_ref, vc_ref, ks_ref, vs_ref, kw_ref, vw_ref, gl_ref, ov_ref, o_ref):
    R = HEADS_PER_GROUP
    dk = HEAD_DIM
    tq = q_ref.shape[4] // R
    ncp = kc_ref.shape[2]
    nsl = ov_ref.shape[0]
    seq = kw_ref.shape[2]
    n_top = min(N_SEL, nsl)
    qi = pl.program_id(2)
    qs = qi * tq
    q_t = q_ref[0, 0, 0]
    t_lane = qs + lax.broadcasted_iota(I32, (1, tq), 1)

    init = (jnp.full((1, R * tq), NEG, F32), jnp.zeros((dk + V_PAD, R * tq), F32))

    c_row = lax.broadcasted_iota(I32, (ncp, tq), 0)
    bias_c = jnp.where(c_row * CMP_STRIDE + (CMP_BLOCK - 1) <= t_lane, 0.0, NEG)
    s_t = jnp.dot(kc_ref[0, 0], q_t, preferred_element_type=F32) + _tile_heads(bias_c)
    p = jnp.exp2(s_t - jnp.max(s_t, axis=0, keepdims=True)).astype(BF16)
    acc = jnp.dot(jnp.concatenate([vc_ref[0, 0], ov_ref[...]], axis=0), p, preferred_element_type=F32)
    has_c = jnp.where(t_lane >= CMP_BLOCK - 1, 1.0, 0.0)
    inv_l = _tile_heads(has_c) / jnp.maximum(acc[dk:dk + 1], TINY)
    o_c = acc[:dk] * inv_l

    imp_heads = acc[dk + V_PAD:] * inv_l
    imp = imp_heads[:, 0:tq]
    for r in range(1, R):
        imp = imp + imp_heads[:, r * tq:(r + 1) * tq]
    j_row = lax.broadcasted_iota(I32, (nsl, tq), 0).astype(F32)
    bt = (t_lane >> (SEL_BLOCK.bit_length() - 1)).astype(F32)
    forced = (j_row == 0) | (j_row == bt) | (j_row == bt - 1)
    future = j_row > bt
    work = jnp.where(forced, FORCE, jnp.where(future, -FORCE, imp))
    for _ in range(n_top):
        mx = jnp.max(work, axis=0, keepdims=True)
        cand = jnp.where(work == mx, j_row, float(nsl))
        work = jnp.where(cand == jnp.min(cand, axis=0, keepdims=True), -jnp.inf, work)
    sel_bias = jnp.where((work == -jnp.inf) & jnp.logical_not(future), 0.0, NEG).astype(BF16)
    q_aug = jnp.concatenate([q_t, _tile_heads(sel_bias)], axis=0)

    tw = vw_ref.shape[4]
    nw = WINDOW + tw
    o_w_parts = []
    for h in range(tq // tw):
        cols = [slice(r * tq + h * tw, r * tq + (h + 1) * tw) for r in range(R)]
        q_h = jnp.concatenate([q_t[:, c] for c in cols], axis=1)
        t_h = t_lane[:, h * tw:(h + 1) * tw]
        w0 = pl.multiple_of(jnp.maximum(qs + h * tw - WINDOW, 0), tw)
        wpos = w0 + lax.broadcasted_iota(I32, (nw, tw), 0)
        bias_w = jnp.where((wpos <= t_h) & (wpos > t_h - WINDOW), 0.0, NEG)
        s = jnp.dot(kw_ref[0, 0, pl.ds(w0, nw), :], q_h, preferred_element_type=F32) + _tile_heads(bias_w)
        c0 = w0 // tw
        vw_slab = jnp.concatenate([vw_ref[0, 0, c0 + i] for i in range(nw // tw)], axis=1)
        init_h = (jnp.full((1, R * tw), NEG, F32), jnp.zeros((dk + V_PAD, R * tw), F32))
        o_w_parts.append(_softmax_finish(_softmax_step(init_h, s, vw_slab)[1]))
    o_w = jnp.concatenate([o_w_parts[h][:, r * tw:(r + 1) * tw]
                           for r in range(R) for h in range(tq // tw)], axis=1)

    tk = min(SEL_KEY_TILE, seq)

    def sel_scores(kt):
        k0 = pl.multiple_of(kt * tk, tk)
        s = jnp.dot(ks_ref[0, 0, pl.ds(k0, tk), :], q_aug, preferred_element_type=F32)
        return s, vs_ref[0, 0, kt], k0

    def sel_step(kt, carry):
        s, v_t, _ = sel_scores(kt)
        return _softmax_step(carry, s, v_t)

    def sel_pair(i, carry):
        return sel_step(2 * i + 1, sel_step(2 * i, carry))

    n_full = qs // tk
    carry = lax.fori_loop(0, n_full // 2, sel_pair, init)
    carry = lax.cond(n_full % 2 == 1, lambda c: sel_step(n_full - 1, c), lambda c: c, carry)
    k0 = pl.multiple_of(n_full * tk, tk)
    n_sub = tq // DIAG_Q_TILE if (tq % tk == 0 and tq % DIAG_Q_TILE == 0) else 1
    tsub = tq // n_sub
    o_s_parts = []
    for h in range(n_sub):
        cols = [slice(r * tq + h * tsub, r * tq + (h + 1) * tsub) for r in range(R)]
        pick = lambda a: a if n_sub == 1 else jnp.concatenate([a[:, c] for c in cols], axis=1)
        nk = tk if n_sub == 1 else (h + 1) * tsub
        t_h = t_lane[:, h * tsub:(h + 1) * tsub]
        kpos = k0 + lax.broadcasted_iota(I32, (nk, tsub), 0)
        s = (jnp.dot(ks_ref[0, 0, pl.ds(k0, nk), :], pick(q_aug), preferred_element_type=F32)
             + _tile_heads(jnp.where(kpos <= t_h, 0.0, NEG)))
        sub_carry = (pick(carry[0]), pick(carry[1]))
        o_s_parts.append(_softmax_finish(_softmax_step(sub_carry, s, vs_ref[0, 0, n_full][:, :nk])[1]))
    o_s = jnp.concatenate([o_s_parts[h][:, r * tsub:(r + 1) * tsub]
                           for r in range(R) for h in range(n_sub)], axis=1)

    gates = jax.nn.sigmoid(gl_ref[0, 0, 0])
    o = gates[0:1, :] * o_c + gates[1:2, :] * o_s + gates[2:3, :] * o_w
    o_ref[0] = jnp.concatenate([o[:, r * tq:(r + 1) * tq].T for r in range(R)], axis=1).astype(o_ref.dtype)


def _nsa_attention(q_t, kcmp, vcmp_t, ks_aug, vs_t, kw, vw_t, gl_t):
    bsz, g, nq, dk, rq = q_t.shape
    seq = kw.shape[2]
    ncp = kcmp.shape[2]
    nsl = seq // SEL_BLOCK
    assert seq >= WINDOW + rq // HEADS_PER_GROUP and seq % min(SEL_KEY_TILE, seq) == 0
    assert min(SEL_KEY_TILE, seq) % (rq // HEADS_PER_GROUP) == 0
    c_start = jnp.arange(ncp) * CMP_STRIDE
    j_start = jnp.arange(nsl) * SEL_BLOCK
    overlap_t = ((c_start[None, :] < j_start[:, None] + SEL_BLOCK)
                 & (c_start[None, :] + CMP_BLOCK > j_start[:, None])).astype(BF16)
    per_bg = lambda shape: pl.BlockSpec((1, 1) + shape, lambda b, j, i: (b, j) + (0,) * len(shape))
    per_q = lambda shape: pl.BlockSpec((1, 1, 1) + shape, lambda b, j, i: (b, j, i, 0, 0))
    tk = min(SEL_KEY_TILE, seq)
    tq = rq // HEADS_PER_GROUP
    return pl.pallas_call(
        _nsa_kernel,
        grid=(bsz, g, nq),
        in_specs=[per_q((dk, rq)),
                  per_bg((ncp, dk)), per_bg((dk + V_PAD, ncp)),
                  per_bg((seq, dk + nsl)), per_bg((seq // tk, dk + V_PAD, tk)),
                  per_bg((seq, dk)), per_bg(vw_t.shape[2:]),
                  per_q((3, rq)),
                  pl.BlockSpec((nsl, ncp), lambda b, j, i: (0, 0))],
        out_specs=pl.BlockSpec((1, tq, HEADS_PER_GROUP * dk), lambda b, j, i: (b, i, j)),
        out_shape=jax.ShapeDtypeStruct((bsz, seq, g * HEADS_PER_GROUP * dk), BF16),
        compiler_params=_params("arbitrary", "arbitrary", "arbitrary"),
    )(q_t, kcmp, vcmp_t, ks_aug, vs_t, kw, vw_t, gl_t, overlap_t)


def _rope_rows(v, cos, sin):
    w = v.shape[1]
    lane = lax.broadcasted_iota(I32, v.shape, 1)
    first_half = (lane & (HEAD_DIM // 2)) == 0
    partner = jnp.where(first_half, pltpu.roll(v, w - HEAD_DIM // 2, 1), pltpu.roll(v, HEAD_DIM // 2, 1))
    return v * cos + partner * sin


def _nsa_proj_kernel(x_ref, wq_ref, wk_ref, wv_ref, wc_ref, wg_ref, cq_ref, sq_ref, ck_ref, sk_ref,
                     q_ref, ks_ref, kw_ref, kc_ref, vc_ref, vs_ref, vw_ref, gl_ref, *, row_tiles_per_seq):
    G, R, dk = N_KV_GROUPS, HEADS_PER_GROUP, HEAD_DIM
    tm = x_ref.shape[0]
    tq = q_ref.shape[4] // R
    tk = vs_ref.shape[4]
    nsl = ks_ref.shape[3] - dk
    xb = x_ref[...].astype(BF16)

    q_t = _nt_dot(wq_ref[...], xb).reshape(N_HEADS, dk, tm)
    partner = jnp.concatenate([q_t[:, dk // 2:], q_t[:, :dk // 2]], axis=1)
    q_t = (q_t * cq_ref[...][None] + partner * sq_ref[...][None]).astype(BF16)
    for g in range(G):
        for h in range(tm // tq):
            q_ref[0, g, h] = jnp.concatenate(
                [q_t[g * R + r][:, h * tq:(h + 1) * tq] for r in range(R)], axis=1)

    kk = jnp.dot(xb, wk_ref[...], preferred_element_type=F32)
    cos_k, sin_k = ck_ref[...], sk_ref[...]
    k_sel = _rope_rows(kk[:, :KV_WIDTH], cos_k, sin_k)
    k_win = _rope_rows(kk[:, KV_WIDTH:2 * KV_WIDTH], cos_k, sin_k)
    kc_ref[...] = _rope_rows(kk[:, 2 * KV_WIDTH:], cos_k, sin_k)
    pos = (pl.program_id(0) % row_tiles_per_seq) * tm + lax.broadcasted_iota(I32, (tm, nsl), 0)
    block_onehot = jnp.where((pos >> (SEL_BLOCK.bit_length() - 1))
                             == lax.broadcasted_iota(I32, (tm, nsl), 1), 1.0, 0.0)
    for g in range(G):
        ks_ref[0, g] = jnp.concatenate([k_sel[:, g * dk:(g + 1) * dk], block_onehot],
                                       axis=1).astype(BF16)
        kw_ref[0, g] = k_win[:, g * dk:(g + 1) * dk].astype(BF16)

    v_t = _nt_dot(wv_ref[...], xb)
    extra = jnp.where(lax.broadcasted_iota(I32, (V_PAD, tm), 0) == 0, 1.0, 0.0)
    for g in range(G):
        vs = jnp.concatenate([v_t[g * dk:(g + 1) * dk], extra], axis=0).astype(BF16)
        for c in range(tm // tk):
            vs_ref[0, g, c] = vs[:, c * tk:(c + 1) * tk]
        vw = jnp.concatenate([v_t[KV_WIDTH + g * dk:KV_WIDTH + (g + 1) * dk], extra],
                             axis=0).astype(BF16)
        tw = vw_ref.shape[4]
        for h in range(tm // tw):
            vw_ref[0, g, h] = vw[:, h * tw:(h + 1) * tw]

    vc_ref[...] = jnp.dot(xb, wc_ref[...], preferred_element_type=F32)

    gl = _nt_dot(wg_ref[...], xb)
    for g in range(G):
        for h in range(tm // tq):
            gl_ref[0, g, h] = jnp.concatenate(
                [jnp.concatenate([gl[(g * 3 + i) * R + r:(g * 3 + i) * R + r + 1, h * tq:(h + 1) * tq]
                                  for r in range(R)], axis=1) for i in range(3)], axis=0)


def _nsa_project(x, bsz, seq, w_in):
    G, R, dk = N_KV_GROUPS, HEADS_PER_GROUP, HEAD_DIM
    t, d = x.shape
    qd = N_HEADS * dk
    tm = min(ROW_TILE, seq)
    tq = min(Q_TILE, seq)
    tk = min(SEL_KEY_TILE, seq)
    tw = min(WIN_Q_TILE, tq)
    nsl = seq // SEL_BLOCK
    ns = seq // tm
    assert tm % tq == 0 and tm % tk == 0 and tq % tw == 0
    cut = [qd + j * KV_WIDTH for j in range(7)]
    w = w_in.astype(BF16)
    w_q, w_kc, w_vc, w_ks, w_vs, w_kw, w_vw, w_gl = (
        w[:, :qd], w[:, cut[0]:cut[1]], w[:, cut[1]:cut[2]], w[:, cut[2]:cut[3]],
        w[:, cut[3]:cut[4]], w[:, cut[4]:cut[5]], w[:, cut[5]:cut[6]], w[:, cut[6]:])
    wq_t = w_q.T
    wk = jnp.concatenate([w_ks, w_kw, w_kc], axis=1)
    wv_t = jnp.concatenate([w_vs, w_vw], axis=1).T
    wc = w_vc
    wg_t = w_gl.reshape(d, G, R, 3).transpose(1, 3, 2, 0).reshape(G * 3 * R, d)
    wg_t = jnp.concatenate([wg_t, jnp.zeros((LANES - G * 3 * R, d), BF16)], axis=0)
    pos = jnp.arange(seq, dtype=F32)
    inv = ROPE_THETA ** (-jnp.arange(0, dk, 2, dtype=F32) / dk)
    ang = pos[:, None] * inv[None, :]
    cos, sin = jnp.cos(ang), jnp.sin(ang)
    cos_h = jnp.concatenate([cos, cos], axis=1)
    sin_h = jnp.concatenate([-sin, sin], axis=1)
    cq, sq = (cos_h * QK_SCALE).T, (sin_h * QK_SCALE).T
    ck, sk = jnp.tile(cos_h, (1, G)), jnp.tile(sin_h, (1, G))
    full = lambda a: pl.BlockSpec(a.shape, lambda i: (0,) * a.ndim)
    b_of = lambda i: i // ns
    s_of = lambda i: i % ns
    out_shape = [jax.ShapeDtypeStruct((bsz, G, seq // tq, dk, R * tq), BF16),
                 jax.ShapeDtypeStruct((bsz, G, seq, dk + nsl), BF16),
                 jax.ShapeDtypeStruct((bsz, G, seq, dk), BF16),
                 jax.ShapeDtypeStruct((t, KV_WIDTH), F32),
                 jax.ShapeDtypeStruct((t, KV_WIDTH), F32),
                 jax.ShapeDtypeStruct((bsz, G, seq // tk, dk + V_PAD, tk), BF16),
                 jax.ShapeDtypeStruct((bsz, G, seq // tw, dk + V_PAD, tw), BF16),
                 jax.ShapeDtypeStruct((bsz, G, seq // tq, 3, R * tq), F32)]
    out_specs = [pl.BlockSpec((1, G, tm // tq, dk, R * tq), lambda i: (b_of(i), 0, s_of(i), 0, 0)),
                 pl.BlockSpec((1, G, tm, dk + nsl), lambda i: (b_of(i), 0, s_of(i), 0)),
                 pl.BlockSpec((1, G, tm, dk), lambda i: (b_of(i), 0, s_of(i), 0)),
                 pl.BlockSpec((tm, KV_WIDTH), lambda i: (i, 0)),
                 pl.BlockSpec((tm, KV_WIDTH), lambda i: (i, 0)),
                 pl.BlockSpec((1, G, tm // tk, dk + V_PAD, tk), lambda i: (b_of(i), 0, s_of(i), 0, 0)),
                 pl.BlockSpec((1, G, tm // tw, dk + V_PAD, tw), lambda i: (b_of(i), 0, s_of(i), 0, 0)),
                 pl.BlockSpec((1, G, tm // tq, 3, R * tq), lambda i: (b_of(i), 0, s_of(i), 0, 0))]
    return pl.pallas_call(
        functools.partial(_nsa_proj_kernel, row_tiles_per_seq=ns),
        grid=(t // tm,),
        in_specs=[pl.BlockSpec((tm, d), lambda i: (i, 0)),
                  full(wq_t), full(wk), full(wv_t), full(wc), full(wg_t),
                  pl.BlockSpec((dk, tm), lambda i: (0, s_of(i))),
                  pl.BlockSpec((dk, tm), lambda i: (0, s_of(i))),
                  pl.BlockSpec((tm, KV_WIDTH), lambda i: (s_of(i), 0)),
                  pl.BlockSpec((tm, KV_WIDTH), lambda i: (s_of(i), 0))],
        out_specs=out_specs, out_shape=out_shape,
        compiler_params=_params("arbitrary"),
    )(x, wq_t, wk, wv_t, wc, wg_t, cq, sq, ck, sk)


def _nsa_mixer_heads(x, bsz, seq, w_in, cmp_k_pe, cmp_k_w1, cmp_k_w2, cmp_v_pe, cmp_v_w1, cmp_v_w2):
    q_t, ks_aug, kw, kc, vc, vs_t, vw_t, gl_t = _nsa_project(x, bsz, seq, w_in)
    kcmp, _ = _compress(kc, bsz, seq, cmp_k_pe, cmp_k_w1, cmp_k_w2)
    _, vcmp_t = _compress(vc, bsz, seq, cmp_v_pe, cmp_v_w1, cmp_v_w2)
    o = _nsa_attention(q_t, kcmp, vcmp_t, ks_aug, vs_t, kw, vw_t, gl_t)
    return o.reshape(bsz * seq, N_HEADS * HEAD_DIM)


def _even_mixer_cat(x, bsz, seq, w_in, conv_w, lru_conv_w, lru_conv_b, w_a, b_a, w_x, b_x, lam):
    wa_bd = jax.scipy.linalg.block_diag(*w_a).astype(BF16)
    wx_bd = jax.scipy.linalg.block_diag(*w_x).astype(BF16)
    return _l0_mixer(x, w_in.astype(BF16), bsz, seq, conv_w, lru_conv_w, lru_conv_b, wa_bd, b_a, wx_bd, b_x, lam)


def kernel(x, l0_w_in, l0_conv_w, l0_lru_conv_w, l0_lru_conv_b, l0_lru_w_a, l0_lru_b_a, l0_lru_w_x, l0_lru_b_x, l0_lru_lambda, l0_w_out, l0_ln1_g, l0_ln1_b, l0_router_group_w, l0_router_group_b, l0_router_expert_w, l0_router_expert_b, l0_expert_w_gate, l0_expert_w_up, l0_expert_w_down, l0_ln2_g, l0_ln2_b, l1_w_in, l1_cmp_k_pe, l1_cmp_k_w1, l1_cmp_k_w2, l1_cmp_v_pe, l1_cmp_v_w1, l1_cmp_v_w2, l1_w_out, l1_ln1_g, l1_ln1_b, l1_router_group_w, l1_router_group_b, l1_router_expert_w, l1_router_expert_b, l1_expert_w_gate, l1_expert_w_up, l1_expert_w_down, l1_ln2_g, l1_ln2_b):
    bsz, seq, d = x.shape
    h = x.reshape(bsz * seq, d)
    y = _even_mixer_cat(h, bsz, seq, l0_w_in, l0_conv_w, l0_lru_conv_w, l0_lru_conv_b, l0_lru_w_a,
                        l0_lru_b_a, l0_lru_w_x, l0_lru_b_x, l0_lru_lambda)
    h, *routing = _matmul_residual_ln_route(
        y, l0_w_out.astype(BF16), h, l0_ln1_g, l0_ln1_b,
        (l0_router_group_w, l0_router_group_b, l0_router_expert_w, l0_router_expert_b))
    h = _hier_moe_ln(h, *routing, l0_expert_w_gate, l0_expert_w_up, l0_expert_w_down, l0_ln2_g, l0_ln2_b)
    y = _nsa_mixer_heads(h, bsz, seq, l1_w_in, l1_cmp_k_pe, l1_cmp_k_w1, l1_cmp_k_w2, l1_cmp_v_pe,
                         l1_cmp_v_w1, l1_cmp_v_w2)
    h, *routing = _matmul_residual_ln_route(
        y, l1_w_out.astype(BF16), h, l1_ln1_g, l1_ln1_b,
        (l1_router_group_w, l1_router_group_b, l1_router_expert_w, l1_router_expert_b))
    h = _hier_moe_ln(h, *routing, l1_expert_w_gate, l1_expert_w_up, l1_expert_w_down, l1_ln2_g, l1_ln2_b)
    return h.reshape(bsz, seq, d)
```

```python
import functools

import jax
import jax.numpy as jnp
from jax import lax
from jax.experimental import pallas as pl
from jax.experimental.pallas import tpu as pltpu

F32 = jnp.float32
BF16 = jnp.bfloat16
I32 = jnp.int32

DEPTH = 2
DN_ALPHA = (2 * DEPTH) ** 0.25
LN_EPS = 1e-5
CONV_WIDTH = 512
CONV_TAPS = 3
LRU_WIDTH = 512
LRU_CONV_TAPS = 4
LRU_C = 8.0
N_HEADS = 16
HEAD_DIM = 64
N_KV_GROUPS = 4
HEADS_PER_GROUP = N_HEADS // N_KV_GROUPS
KV_WIDTH = N_KV_GROUPS * HEAD_DIM
CMP_STRIDE = 16
CMP_BLOCK = 32
SEL_BLOCK = 64
N_SEL = 16
WINDOW = 512
ROPE_THETA = 10000.0
FORCE = 1e4
NEG = -1e30
TINY = 1e-30
QK_SCALE = HEAD_DIM ** -0.5 * 1.4426950408889634
N_GROUPS = 4
EXPERTS_PER_GROUP = 8
N_EXPERTS = N_GROUPS * EXPERTS_PER_GROUP
TOP_K = 2

LANES = 128
VMEM_LIMIT_BYTES = 48 * 1024 * 1024
ROW_TILE = 512
SCAN_CHUNK = 128
MIXER_SUBCHUNKS = 4
GATHER_TILE = 512
MOE_BLOCK = 512
Q_TILE = 512
WIN_Q_TILE = 256
DIAG_Q_TILE = 256
SEL_KEY_TILE = 512
V_PAD = 16


def _params(*sem):
    return pltpu.CompilerParams(dimension_semantics=sem, vmem_limit_bytes=VMEM_LIMIT_BYTES)


def _layernorm(v, g, b):
    mu = jnp.mean(v, axis=-1, keepdims=True)
    d = v - mu
    var = jnp.mean(d * d, axis=-1, keepdims=True)
    return d * lax.rsqrt(var + LN_EPS) * g + b


def _gelu_tanh(x):
    return 0.5 * x * (1.0 + jnp.tanh(0.7978845608028654 * (x + 0.044715 * (x * x * x))))


def _nt_dot(a, b):
    return lax.dot_general(a, b, (((1,), (1,)), ((), ())), preferred_element_type=F32)


def _router_operands(w_rg, b_rg, w_re, b_re):
    d = w_rg.shape[0]
    npad = LANES - N_GROUPS - N_EXPERTS
    w = jnp.concatenate([w_rg, w_re, jnp.zeros((d, npad), F32)], axis=1)
    wh = w.astype(BF16)
    wl = (w - wh.astype(F32)).astype(BF16)
    b = jnp.concatenate([b_rg, b_re, jnp.zeros((npad,), F32)]).reshape(1, LANES)
    return wh, wl, b


def _mm_res_ln_route_kernel(y_ref, w_ref, x_ref, g_ref, b_ref, wh_ref, wl_ref, rb_ref,
                            o_ref, info_ref, info_t_ref, cnt_ref):
    acc = jnp.dot(y_ref[...].astype(BF16), w_ref[...], preferred_element_type=F32)
    out = _layernorm(DN_ALPHA * x_ref[...] + acc, g_ref[...], b_ref[...])
    o_ref[...] = out
    _route_tile(out, wh_ref, wl_ref, rb_ref, info_ref, info_t_ref, cnt_ref)


def _matmul_residual_ln_route(y, w, x, g, b, router):
    m, k = y.shape
    d = w.shape[1]
    tm = min(ROW_TILE, m)
    wh, wl, rb = _router_operands(*router)
    const = lambda a: pl.BlockSpec(a.shape, lambda i: (0,) * a.ndim)
    return pl.pallas_call(
        _mm_res_ln_route_kernel,
        grid=(m // tm,),
        in_specs=[pl.BlockSpec((tm, k), lambda i: (i, 0)),
                  const(w),
                  pl.BlockSpec((tm, d), lambda i: (i, 0)),
                  pl.BlockSpec((1, d), lambda i: (0, 0)),
                  pl.BlockSpec((1, d), lambda i: (0, 0)),
                  const(wh), const(wl), const(rb)],
        out_specs=[pl.BlockSpec((tm, d), lambda i: (i, 0)),
                   pl.BlockSpec((tm, LANES), lambda i: (i, 0)),
                   pl.BlockSpec((8, tm), lambda i: (0, i)),
                   pl.BlockSpec((8, LANES), lambda i: (0, 0))],
        out_shape=[jax.ShapeDtypeStruct((m, d), F32), jax.ShapeDtypeStruct((m, LANES), F32),
                   jax.ShapeDtypeStruct((8, m), F32), jax.ShapeDtypeStruct((8, LANES), F32)],
        compiler_params=_params("arbitrary"),
    )(y, w, x, g.reshape(1, d), b.reshape(1, d), wh, wl, rb)


def _shift_rows(cur, tail, d, row8):
    rolled = pltpu.roll(cur, d, 0)
    head = jnp.where(row8 < d, pltpu.roll(tail, d, 0), rolled[:8])
    return jnp.concatenate([head, rolled[8:]], axis=0)


def _l0_mixer_kernel(x_ref, win_ref, cw_ref, lw_ref, lb_ref, wa_ref, ba_ref, wx_ref, bx_ref, lam_ref,
                     o_ref, tail_c, tail_x, h_state):
    W = CONV_WIDTH
    tc = x_ref.shape[0] // MIXER_SUBCHUNKS

    @pl.when(pl.program_id(1) == 0)
    def _():
        tail_c[...] = jnp.zeros_like(tail_c)
        tail_x[...] = jnp.zeros_like(tail_x)
        h_state[...] = jnp.zeros_like(h_state)

    row8 = lax.broadcasted_iota(I32, (8, W), 0)
    row = lax.broadcasted_iota(I32, (tc, W), 0)
    nl = -lam_ref[...]
    softplus = jnp.maximum(nl, 0.0) + jnp.log1p(jnp.exp(-jnp.abs(nl)))
    zs = [jnp.dot(x_ref[k * tc:(k + 1) * tc, :].astype(BF16), win_ref[...], preferred_element_type=F32)
          for k in range(MIXER_SUBCHUNKS)]
    tc_prev, tx_prev, h_prev = tail_c[...], tail_x[...], h_state[0:1, :]

    for k, z in enumerate(zs):
        rows = slice(k * tc, (k + 1) * tc)
        ch = z[:, W:2 * W] * z[:, 2 * W:3 * W]
        conv = ch * cw_ref[CONV_TAPS - 1:CONV_TAPS, :]
        for d in range(1, CONV_TAPS):
            conv = conv + _shift_rows(ch, tc_prev, d, row8) * cw_ref[CONV_TAPS - 1 - d:CONV_TAPS - d, :]
        tc_prev = ch[tc - 8:, :]
        o_ref[rows, :W] = (z[:, :W] * conv).astype(o_ref.dtype)

        xl = z[:, 4 * W:5 * W]
        xc = xl * lw_ref[LRU_CONV_TAPS - 1:LRU_CONV_TAPS, :]
        for d in range(1, LRU_CONV_TAPS):
            xc = xc + _shift_rows(xl, tx_prev, d, row8) * lw_ref[LRU_CONV_TAPS - 1 - d:LRU_CONV_TAPS - d, :]
        xc = xc + lb_ref[...]
        tx_prev = xl[tc - 8:, :]

        xcb = xc.astype(BF16)
        r = jax.nn.sigmoid(jnp.dot(xcb, wa_ref[...], preferred_element_type=F32) + ba_ref[...])
        ig = jax.nn.sigmoid(jnp.dot(xcb, wx_ref[...], preferred_element_type=F32) + bx_ref[...])
        log_a = -LRU_C * r * softplus
        a = jnp.exp(log_a)
        mult = jnp.sqrt(jnp.maximum(1.0 - a * a, 0.0))
        u = mult * ig * xc

        d = 1
        while d < tc:
            keep = row >= d
            a_s = jnp.where(keep, pltpu.roll(a, d, 0), 1.0)
            u_s = jnp.where(keep, pltpu.roll(u, d, 0), 0.0)
            u = a * u_s + u
            a = a * a_s
            d *= 2
        h = a * h_prev + u
        h_prev = h[tc - 1:tc, :]
        o_ref[rows, W:] = (h * _gelu_tanh(z[:, 3 * W:4 * W])).astype(o_ref.dtype)

    tail_c[...] = tc_prev
    tail_x[...] = tx_prev
    h_state[...] = jnp.broadcast_to(h_prev, h_state.shape)


def _l0_mixer(x, w_in, bsz, seq, conv_w, lru_conv_w, lru_conv_b, wa_bd, b_a, wx_bd, b_x, lam):
    W = CONV_WIDTH
    tc = min(SCAN_CHUNK * MIXER_SUBCHUNKS, seq)
    nchunk = seq // tc
    vec = lambda i, j: (0, 0)
    return pl.pallas_call(
        _l0_mixer_kernel,
        grid=(bsz, nchunk),
        in_specs=[pl.BlockSpec((tc, x.shape[1]), lambda i, j: (i * nchunk + j, 0)),
                  pl.BlockSpec(w_in.shape, vec),
                  pl.BlockSpec((CONV_TAPS, W), vec),
                  pl.BlockSpec((LRU_CONV_TAPS, W), vec),
                  pl.BlockSpec((1, W), vec),
                  pl.BlockSpec((W, W), vec),
                  pl.BlockSpec((1, W), vec),
                  pl.BlockSpec((W, W), vec),
                  pl.BlockSpec((1, W), vec),
                  pl.BlockSpec((1, W), vec)],
        out_specs=pl.BlockSpec((tc, 2 * W), lambda i, j: (i * nchunk + j, 0)),
        out_shape=jax.ShapeDtypeStruct((bsz * seq, 2 * W), BF16),
        scratch_shapes=[pltpu.VMEM((8, W), F32), pltpu.VMEM((8, W), F32), pltpu.VMEM((8, W), F32)],
        compiler_params=_params("arbitrary", "arbitrary"),
    )(x, w_in, conv_w, lru_conv_w, lru_conv_b.reshape(1, W), wa_bd, b_a.reshape(1, W), wx_bd,
      b_x.reshape(1, W), lam.reshape(1, W))


def _split_bf16(v):
    hi = v.astype(BF16)
    lo = (v - hi.astype(F32)).astype(BF16)
    return hi, lo


def _route_tile(x, wh_ref, wl_ref, b_ref, info_ref, info_t_ref, cnt_ref):
    tm = x.shape[0]

    @pl.when(pl.program_id(0) == 0)
    def _():
        cnt_ref[...] = jnp.zeros_like(cnt_ref)

    xh, xl = _split_bf16(x)
    both = jnp.dot(xh, jnp.concatenate([wh_ref[...], wl_ref[...]], axis=1), preferred_element_type=F32)
    logits = (both[:, :LANES] + both[:, LANES:]
              + jnp.dot(xl, wh_ref[...], preferred_element_type=F32)) + b_ref[...]
    lane = lax.broadcasted_iota(I32, logits.shape, 1).astype(F32)
    ninf = -jnp.inf

    is_g = lane < N_GROUPS
    gmax = jnp.max(jnp.where(is_g, logits, ninf), axis=-1, keepdims=True)
    g_star = jnp.min(jnp.where(is_g & (logits == gmax), lane, float(LANES)), axis=-1, keepdims=True)
    gsum = jnp.sum(jnp.where(is_g, jnp.exp(logits - gmax), 0.0), axis=-1, keepdims=True)
    p_grp = 1.0 / gsum

    lo_lane = N_GROUPS + EXPERTS_PER_GROUP * g_star
    is_e = (lane >= lo_lane) & (lane < lo_lane + EXPERTS_PER_GROUP)
    emax = jnp.max(jnp.where(is_e, logits, ninf), axis=-1, keepdims=True)
    ex = jnp.where(is_e, jnp.exp(logits - emax), 0.0)
    pe = ex / jnp.sum(ex, axis=-1, keepdims=True)
    pe_m = jnp.where(is_e, pe, ninf)
    v1 = jnp.max(pe_m, axis=-1, keepdims=True)
    l1 = jnp.min(jnp.where(pe_m == v1, lane, float(LANES)), axis=-1, keepdims=True)
    pe_m2 = jnp.where(lane == l1, ninf, pe_m)
    v2 = jnp.max(pe_m2, axis=-1, keepdims=True)
    l2 = jnp.min(jnp.where(pe_m2 == v2, lane, float(LANES)), axis=-1, keepdims=True)
    vs = v1 + v2
    w1 = p_grp * v1 / vs
    w2 = p_grp * v2 / vs

    hit1 = lane == l1
    hit2 = lane == l2
    onehot = jnp.where(hit1 | hit2, 1.0, 0.0)
    r_i = lax.broadcasted_iota(I32, (tm, tm), 0)
    c_i = lax.broadcasted_iota(I32, (tm, tm), 1)
    tri = jnp.where(c_i < r_i, 1.0, 0.0).astype(BF16)
    before = jnp.dot(tri, onehot.astype(BF16), preferred_element_type=F32) + cnt_ref[0:1, :]
    rank1 = jnp.sum(jnp.where(hit1, before, 0.0), axis=-1, keepdims=True)
    rank2 = jnp.sum(jnp.where(hit2, before, 0.0), axis=-1, keepdims=True)
    cnt_ref[...] = cnt_ref[...] + jnp.sum(onehot, axis=0, keepdims=True)

    e1 = l1 - N_GROUPS
    e2 = l2 - N_GROUPS
    info = jnp.where(lane == 0, e1, jnp.where(lane == 1, e2, jnp.where(lane == 2, w1, jnp.where(
        lane == 3, w2, jnp.where(lane == 4, rank1, jnp.where(lane == 5, rank2, 0.0))))))
    info_ref[...] = info
    info_t_ref[...] = info.T[0:8, :]


def _row_copy(src_ref, src_row, dst_ref, dst_row, sem):
    return pltpu.make_async_copy(src_ref.at[pl.ds(src_row, 1), :], dst_ref.at[pl.ds(dst_row, 1), :], sem)


def _dispatch_kernel(meta_ref, dest_ref, x_ref, xs_ref, zeros, sem, zsem):
    tt = x_ref.shape[0]
    n_blocks = xs_ref.shape[0] // MOE_BLOCK

    def zero_block(row0):
        return pltpu.make_async_copy(zeros, xs_ref.at[pl.ds(row0, MOE_BLOCK), :], zsem)

    @pl.when(pl.program_id(0) == 0)
    def _():
        zeros[...] = jnp.zeros_like(zeros)
        n_used = meta_ref[N_EXPERTS]

        def over_fill_targets(fn):
            for e in range(N_EXPERTS):
                @pl.when(meta_ref[e] >= 0)
                def _():
                    fn(pl.multiple_of(meta_ref[e], MOE_BLOCK))

            def trailing(j, c):
                fn(pl.multiple_of(j * MOE_BLOCK, MOE_BLOCK))
                return c

            lax.fori_loop(n_used, n_blocks, trailing, 0)

        over_fill_targets(lambda row0: zero_block(row0).start())
        over_fill_targets(lambda row0: zero_block(row0).wait())

    def issue(i, c):
        for k in range(TOP_K):
            _row_copy(x_ref, i, xs_ref, dest_ref[0, 0, k * tt + i], sem).start(priority=k)
        return c

    lax.fori_loop(0, tt, issue, 0, unroll=8)
    for _ in range(TOP_K):
        pltpu.make_async_copy(x_ref, xs_ref.at[pl.ds(0, tt), :], sem).wait()


def _moe_dispatch(x, dest3, meta, n_rows):
    t, d = x.shape
    tt = dest3.shape[2] // TOP_K
    return pl.pallas_call(
        _dispatch_kernel,
        grid_spec=pltpu.PrefetchScalarGridSpec(
            num_scalar_prefetch=1, grid=(t // tt,),
            in_specs=[pl.BlockSpec((1, 1, TOP_K * tt), lambda i, m: (i, 0, 0), memory_space=pltpu.SMEM),
                      pl.BlockSpec((tt, d), lambda i, m: (i, 0))],
            out_specs=pl.BlockSpec(memory_space=pl.ANY),
            scratch_shapes=[pltpu.VMEM((MOE_BLOCK, d), F32), pltpu.SemaphoreType.DMA(()),
                            pltpu.SemaphoreType.DMA(())]),
        out_shape=jax.ShapeDtypeStruct((n_rows, d), F32),
        compiler_params=_params("arbitrary"),
    )(meta, dest3, x)


def _expert_kernel(be_ref, xs_ref, wg_ref, wu_ref, wd_ref, ys_ref, wg_bf, wu_bf, wd_bf):
    i = pl.program_id(0)
    n_used = be_ref[pl.num_programs(0)]

    @pl.when((i == 0) | (be_ref[i] != be_ref[jnp.maximum(i - 1, 0)]))
    def _():
        wg_bf[...] = wg_ref[0].astype(BF16)
        wu_bf[...] = wu_ref[0].astype(BF16)
        wd_bf[...] = wd_ref[0].astype(BF16)

    @pl.when(i < n_used)
    def _():
        xb = xs_ref[...].astype(BF16)
        gate = jnp.dot(xb, wg_bf[...], preferred_element_type=F32)
        up = jnp.dot(xb, wu_bf[...], preferred_element_type=F32)
        h = gate * jax.nn.sigmoid(gate) * up
        ys_ref[...] = jnp.dot(h.astype(BF16), wd_bf[...], preferred_element_type=F32)

    @pl.when(i >= n_used)
    def _():
        ys_ref[...] = jnp.zeros_like(ys_ref)


def _moe_experts(xs, block_e, wg, wu, wd):
    p, d = xs.shape
    ff = wg.shape[2]
    nb = p // MOE_BLOCK
    return pl.pallas_call(
        _expert_kernel,
        grid_spec=pltpu.PrefetchScalarGridSpec(
            num_scalar_prefetch=1, grid=(nb,),
            in_specs=[pl.BlockSpec((MOE_BLOCK, d), lambda i, be: (i, 0)),
                      pl.BlockSpec((1, d, ff), lambda i, be: (be[i], 0, 0)),
                      pl.BlockSpec((1, d, ff), lambda i, be: (be[i], 0, 0)),
                      pl.BlockSpec((1, ff, d), lambda i, be: (be[i], 0, 0))],
            out_specs=pl.BlockSpec((MOE_BLOCK, d), lambda i, be: (i, 0)),
            scratch_shapes=[pltpu.VMEM((d, ff), BF16), pltpu.VMEM((d, ff), BF16), pltpu.VMEM((ff, d), BF16)]),
        out_shape=jax.ShapeDtypeStruct((p, d), F32),
        compiler_params=_params("arbitrary"),
    )(block_e, xs, wg, wu, wd)


def _combine_ln_kernel(dest_ref, dest_next_ref, x_ref, info_ref, g_ref, b_ref, ys_ref, o_ref, buf, sem):
    tt = x_ref.shape[0]
    i = pl.program_id(0)
    slot = i % 2

    def issue_tile(d_ref, s):
        def issue(r, c):
            for k in range(TOP_K):
                _row_copy(ys_ref, d_ref[0, 0, k * tt + r], buf.at[s], k * tt + r, sem.at[s]).start(priority=k)
            return c

        lax.fori_loop(0, tt, issue, 0, unroll=8)

    @pl.when(i == 0)
    def _():
        issue_tile(dest_ref, 0)

    @pl.when(i + 1 < pl.num_programs(0))
    def _():
        issue_tile(dest_next_ref, 1 - slot)

    pltpu.make_async_copy(ys_ref.at[pl.ds(0, TOP_K * tt), :], buf.at[slot], sem.at[slot]).wait()
    rows = buf[slot]
    y = info_ref[:, 2:3] * rows[0:tt, :] + info_ref[:, 3:4] * rows[tt:2 * tt, :]
    o_ref[...] = _layernorm(DN_ALPHA * x_ref[...] + y, g_ref[...], b_ref[...])


def _moe_combine_ln(x, info, dest3, ys, g, b):
    t, d = x.shape
    tt = dest3.shape[2] // TOP_K
    n = t // tt
    return pl.pallas_call(
        _combine_ln_kernel,
        grid=(n,),
        in_specs=[pl.BlockSpec((1, 1, TOP_K * tt), lambda i: (i, 0, 0), memory_space=pltpu.SMEM),
                  pl.BlockSpec((1, 1, TOP_K * tt), lambda i: (jnp.minimum(i + 1, n - 1), 0, 0),
                               memory_space=pltpu.SMEM),
                  pl.BlockSpec((tt, d), lambda i: (i, 0)),
                  pl.BlockSpec((tt, LANES), lambda i: (i, 0)),
                  pl.BlockSpec((1, d), lambda i: (0, 0)),
                  pl.BlockSpec((1, d), lambda i: (0, 0)),
                  pl.BlockSpec(memory_space=pl.ANY)],
        out_specs=pl.BlockSpec((tt, d), lambda i: (i, 0)),
        out_shape=jax.ShapeDtypeStruct((t, d), F32),
        scratch_shapes=[pltpu.VMEM((2, TOP_K * tt, d), F32), pltpu.SemaphoreType.DMA((2,))],
        compiler_params=_params("arbitrary"),
    )(dest3, dest3, x, info, g.reshape(1, d), b.reshape(1, d), ys)


def _hier_moe_ln(x, info, info_t, cnt, w_gate, w_up, w_down, ln_g, ln_b):
    t, d = x.shape
    a_total = t * TOP_K
    counts = cnt[0, N_GROUPS:N_GROUPS + N_EXPERTS].astype(I32)
    padded = (counts + MOE_BLOCK - 1) // MOE_BLOCK * MOE_BLOCK
    pad_end = jnp.cumsum(padded)
    pad_start = pad_end - padded
    n_blocks = -(-a_total // MOE_BLOCK) + N_EXPERTS
    e = info_t[0:TOP_K].astype(I32)
    rank = info_t[4:4 + TOP_K].astype(I32)
    expert_ids = jnp.arange(N_EXPERTS, dtype=I32)[:, None, None]
    dest = rank + jnp.sum(jnp.where(e[None] == expert_ids, pad_start[:, None, None], 0), axis=0)
    blk_start = jnp.arange(n_blocks, dtype=I32) * MOE_BLOCK
    block_e = jnp.minimum(jnp.sum((pad_end[None, :] <= blk_start[:, None]).astype(I32), axis=1),
                          N_EXPERTS - 1).astype(I32)
    tt = min(GATHER_TILE, t)
    dest3 = dest.reshape(TOP_K, t // tt, tt).transpose(1, 0, 2).reshape(t // tt, 1, TOP_K * tt)
    last_blk = jnp.where(counts % MOE_BLOCK != 0, pad_end - MOE_BLOCK, -1)
    meta = jnp.concatenate([last_blk, pad_end[-1:] // MOE_BLOCK]).astype(I32)
    xs = _moe_dispatch(x, dest3, meta, n_blocks * MOE_BLOCK)
    ys = _moe_experts(xs, jnp.concatenate([block_e, meta[N_EXPERTS:]]), w_gate, w_up, w_down)
    return _moe_combine_ln(x, info, dest3, ys, ln_g, ln_b)


def _compress_kernel(x_ref, pe_ref, w1_ref, w2_ref, w2t_ref, o_ref, ot_ref):
    n = o_ref.shape[2]
    dk = HEAD_DIM

    def half(p0):
        acc = None
        for p in range(CMP_STRIDE):
            rows = x_ref[pl.ds(p, n, stride=CMP_STRIDE), :] + pe_ref[p0 + p:p0 + p + 1, :]
            part = jnp.dot(rows.astype(BF16), w1_ref[p0 + p], preferred_element_type=F32)
            acc = part if acc is None else acc + part
        return acc

    h = half(0) + pltpu.roll(half(CMP_STRIDE), n - 1, 0)
    hb = _gelu_tanh(h).astype(BF16)
    out = jnp.dot(hb, w2_ref[...], preferred_element_type=F32)
    out_t = _nt_dot(w2t_ref[...], hb)
    extra = jnp.where(lax.broadcasted_iota(I32, (V_PAD, n), 0) == 0, 1.0, 0.0)
    for j in range(2):
        o_ref[0, j] = out[:, j * dk:(j + 1) * dk].astype(o_ref.dtype)
        ot_ref[0, j] = jnp.concatenate([out_t[j * dk:(j + 1) * dk], extra], axis=0).astype(ot_ref.dtype)


def _compress(x, bsz, seq, pe, w1, w2):
    G, dk = N_KV_GROUPS, HEAD_DIM
    n = seq // CMP_STRIDE
    hid = w1.shape[1]
    z = jnp.zeros((CMP_BLOCK, dk, hid), F32)
    w1p = w1.reshape(CMP_BLOCK, dk, hid)
    w1_pair = jnp.concatenate([jnp.concatenate([w1p, z], axis=2),
                               jnp.concatenate([z, w1p], axis=2)], axis=1).astype(BF16)
    z2 = jnp.zeros((hid, dk), F32)
    w2_pair = jnp.concatenate([jnp.concatenate([w2, z2], axis=1),
                               jnp.concatenate([z2, w2], axis=1)], axis=0).astype(BF16)
    pe_pair = jnp.tile(pe, (1, 2))
    const = lambda a: pl.BlockSpec(a.shape, lambda b, j: (0,) * a.ndim)
    return pl.pallas_call(
        _compress_kernel,
        grid=(bsz, G // 2),
        in_specs=[pl.BlockSpec((seq, 2 * dk), lambda b, j: (b, j)),
                  const(pe_pair), const(w1_pair), const(w2_pair), const(w2_pair.T)],
        out_specs=[pl.BlockSpec((1, 2, n, dk), lambda b, j: (b, j, 0, 0)),
                   pl.BlockSpec((1, 2, dk + V_PAD, n), lambda b, j: (b, j, 0, 0))],
        out_shape=[jax.ShapeDtypeStruct((bsz, G, n, dk), BF16),
                   jax.ShapeDtypeStruct((bsz, G, dk + V_PAD, n), BF16)],
        compiler_params=_params("arbitrary", "arbitrary"),
    )(x, pe_pair, w1_pair, w2_pair, w2_pair.T)


def _softmax_step(carry, s_t, v_t):
    m, acc = carry
    m_new = jnp.maximum(m, jnp.max(s_t, axis=0, keepdims=True))
    p = jnp.exp2(s_t - m_new).astype(BF16)
    acc = jnp.exp2(m - m_new) * acc + jnp.dot(v_t, p, preferred_element_type=F32)
    return m_new, acc


def _softmax_finish(acc):
    return acc[:HEAD_DIM] / jnp.maximum(acc[HEAD_DIM:HEAD_DIM + 1], TINY)


def _tile_heads(a):
    return jnp.concatenate([a] * HEADS_PER_GROUP, axis=1)


def _nsa_kernel(q_ref, kc_ref, vc_ref, ks_ref, vs_ref, kw_ref, vw_ref, gl_ref, ov_ref, o_ref):
    R = HEADS_PER_GROUP
    dk = HEAD_DIM
    tq = q_ref.shape[4] // R
    ncp = kc_ref.shape[2]
    nsl = ov_ref.shape[0]
    seq = kw_ref.shape[2]
    n_top = min(N_SEL, nsl)
    qi = pl.program_id(2)
    qs = qi * tq
    q_t = q_ref[0, 0, 0]
    t_lane = qs + lax.broadcasted_iota(I32, (1, tq), 1)

    init = (jnp.full((1, R * tq), NEG, F32), jnp.zeros((dk + V_PAD, R * tq), F32))

    c_row = lax.broadcasted_iota(I32, (ncp, tq), 0)
    bias_c = jnp.where(c_row * CMP_STRIDE + (CMP_BLOCK - 1) <= t_lane, 0.0, NEG)
    s_t = jnp.dot(kc_ref[0, 0], q_t, preferred_element_type=F32) + _tile_heads(bias_c)
    p = jnp.exp2(s_t - jnp.max(s_t, axis=0, keepdims=True)).astype(BF16)
    acc = jnp.dot(jnp.concatenate([vc_ref[0, 0], ov_ref[...]], axis=0), p, preferred_element_type=F32)
    has_c = jnp.where(t_lane >= CMP_BLOCK - 1, 1.0, 0.0)
    inv_l = _tile_heads(has_c) / jnp.maximum(acc[dk:dk + 1], TINY)
    o_c = acc[:dk] * inv_l

    imp_heads = acc[dk + V_PAD:] * inv_l
    imp = imp_heads[:, 0:tq]
    for r in range(1, R):
        imp = imp + imp_heads[:, r * tq:(r + 1) * tq]
    j_row = lax.broadcasted_iota(I32, (nsl, tq), 0).astype(F32)
    bt = (t_lane >> (SEL_BLOCK.bit_length() - 1)).astype(F32)
    forced = (j_row == 0) | (j_row == bt) | (j_row == bt - 1)
    future = j_row > bt
    work = jnp.where(forced, FORCE, jnp.where(future, -FORCE, imp))
    for _ in range(n_top):
        mx = jnp.max(work, axis=0, keepdims=True)
        cand = jnp.where(work == mx, j_row, float(nsl))
        work = jnp.where(cand == jnp.min(cand, axis=0, keepdims=True), -jnp.inf, work)
    sel_bias = jnp.where((work == -jnp.inf) & jnp.logical_not(future), 0.0, NEG).astype(BF16)
    q_aug = jnp.concatenate([q_t, _tile_heads(sel_bias)], axis=0)

    tw = vw_ref.shape[4]
    nw = WINDOW + tw
    o_w_parts = []
    for h in range(tq // tw):
        cols = [slice(r * tq + h * tw, r * tq + (h + 1) * tw) for r in range(R)]
        q_h = jnp.concatenate([q_t[:, c] for c in cols], axis=1)
        t_h = t_lane[:, h * tw:(h + 1) * tw]
        w0 = pl.multiple_of(jnp.maximum(qs + h * tw - WINDOW, 0), tw)
        wpos = w0 + lax.broadcasted_iota(I32, (nw, tw), 0)
        bias_w = jnp.where((wpos <= t_h) & (wpos > t_h - WINDOW), 0.0, NEG)
        s = jnp.dot(kw_ref[0, 0, pl.ds(w0, nw), :], q_h, preferred_element_type=F32) + _tile_heads(bias_w)
        c0 = w0 // tw
        vw_slab = jnp.concatenate([vw_ref[0, 0, c0 + i] for i in range(nw // tw)], axis=1)
        init_h = (jnp.full((1, R * tw), NEG, F32), jnp.zeros((dk + V_PAD, R * tw), F32))
        o_w_parts.append(_softmax_finish(_softmax_step(init_h, s, vw_slab)[1]))
    o_w = jnp.concatenate([o_w_parts[h][:, r * tw:(r + 1) * tw]
                           for r in range(R) for h in range(tq // tw)], axis=1)

    tk = min(SEL_KEY_TILE, seq)

    def sel_scores(kt):
        k0 = pl.multiple_of(kt * tk, tk)
        s = jnp.dot(ks_ref[0, 0, pl.ds(k0, tk), :], q_aug, preferred_element_type=F32)
        return s, vs_ref[0, 0, kt], k0

    def sel_step(kt, carry):
        s, v_t, _ = sel_scores(kt)
        return _softmax_step(carry, s, v_t)

    def sel_pair(i, carry):
        return sel_step(2 * i + 1, sel_step(2 * i, carry))

    n_full = qs // tk
    carry = lax.fori_loop(0, n_full // 2, sel_pair, init)
    carry = lax.cond(n_full % 2 == 1, lambda c: sel_step(n_full - 1, c), lambda c: c, carry)
    k0 = pl.multiple_of(n_full * tk, tk)
    n_sub = tq // DIAG_Q_TILE if (tq % tk == 0 and tq % DIAG_Q_TILE == 0) else 1
    tsub = tq // n_sub
    o_s_parts = []
    for h in range(n_sub):
        cols = [slice(r * tq + h * tsub, r * tq + (h + 1) * tsub) for r in range(R)]
        pick = lambda a: a if n_sub == 1 else jnp.concatenate([a[:, c] for c in cols], axis=1)
        nk = tk if n_sub == 1 else (h + 1) * tsub
        t_h = t_lane[:, h * tsub:(h + 1) * tsub]
        kpos = k0 + lax.broadcasted_iota(I32, (nk, tsub), 0)
        s = (jnp.dot(ks_ref[0, 0, pl.ds(k0, nk), :], pick(q_aug), preferred_element_type=F32)
             + _tile_heads(jnp.where(kpos <= t_h, 0.0, NEG)))
        sub_carry = (pick(carry[0]), pick(carry[1]))
        o_s_parts.append(_softmax_finish(_softmax_step(sub_carry, s, vs_ref[0, 0, n_full][:, :nk])[1]))
    o_s = jnp.concatenate([o_s_parts[h][:, r * tsub:(r + 1) * tsub]
                           for r in range(R) for h in range(n_sub)], axis=1)

    gates = jax.nn.sigmoid(gl_ref[0, 0, 0])
    o = gates[0:1, :] * o_c + gates[1:2, :] * o_s + gates[2:3, :] * o_w
    o_ref[0] = jnp.concatenate([o[:, r * tq:(r + 1) * tq].T for r in range(R)], axis=1).astype(o_ref.dtype)


def _nsa_attention(q_t, kcmp, vcmp_t, ks_aug, vs_t, kw, vw_t, gl_t):
    bsz, g, nq, dk, rq = q_t.shape
    seq = kw.shape[2]
    ncp = kcmp.shape[2]
    nsl = seq // SEL_BLOCK
    assert seq >= WINDOW + rq // HEADS_PER_GROUP and seq % min(SEL_KEY_TILE, seq) == 0
    assert min(SEL_KEY_TILE, seq) % (rq // HEADS_PER_GROUP) == 0
    c_start = jnp.arange(ncp) * CMP_STRIDE
    j_start = jnp.arange(nsl) * SEL_BLOCK
    overlap_t = ((c_start[None, :] < j_start[:, None] + SEL_BLOCK)
                 & (c_start[None, :] + CMP_BLOCK > j_start[:, None])).astype(BF16)
    per_bg = lambda shape: pl.BlockSpec((1, 1) + shape, lambda b, j, i: (b, j) + (0,) * len(shape))
    per_q = lambda shape: pl.BlockSpec((1, 1, 1) + shape, lambda b, j, i: (b, j, i, 0, 0))
    tk = min(SEL_KEY_TILE, seq)
    tq = rq // HEADS_PER_GROUP
    return pl.pallas_call(
        _nsa_kernel,
        grid=(bsz, g, nq),
        in_specs=[per_q((dk, rq)),
                  per_bg((ncp, dk)), per_bg((dk + V_PAD, ncp)),
                  per_bg((seq, dk + nsl)), per_bg((seq // tk, dk + V_PAD, tk)),
                  per_bg((seq, dk)), per_bg(vw_t.shape[2:]),
                  per_q((3, rq)),
                  pl.BlockSpec((nsl, ncp), lambda b, j, i: (0, 0))],
        out_specs=pl.BlockSpec((1, tq, HEADS_PER_GROUP * dk), lambda b, j, i: (b, i, j)),
        out_shape=jax.ShapeDtypeStruct((bsz, seq, g * HEADS_PER_GROUP * dk), BF16),
        compiler_params=_params("arbitrary", "arbitrary", "arbitrary"),
    )(q_t, kcmp, vcmp_t, ks_aug, vs_t, kw, vw_t, gl_t, overlap_t)


def _rope_rows(v, cos, sin):
    w = v.shape[1]
    lane = lax.broadcasted_iota(I32, v.shape, 1)
    first_half = (lane & (HEAD_DIM // 2)) == 0
    partner = jnp.where(first_half, pltpu.roll(v, w - HEAD_DIM // 2, 1), pltpu.roll(v, HEAD_DIM // 2, 1))
    return v * cos + partner * sin


def _nsa_proj_kernel(x_ref, wq_ref, wk_ref, wv_ref, wc_ref, wg_ref, cq_ref, sq_ref, ck_ref, sk_ref,
                     q_ref, ks_ref, kw_ref, kc_ref, vc_ref, vs_ref, vw_ref, gl_ref, *, row_tiles_per_seq):
    G, R, dk = N_KV_GROUPS, HEADS_PER_GROUP, HEAD_DIM
    tm = x_ref.shape[0]
    tq = q_ref.shape[4] // R
    tk = vs_ref.shape[4]
    nsl = ks_ref.shape[3] - dk
    xb = x_ref[...].astype(BF16)

    q_t = _nt_dot(wq_ref[...], xb).reshape(N_HEADS, dk, tm)
    partner = jnp.concatenate([q_t[:, dk // 2:], q_t[:, :dk // 2]], axis=1)
    q_t = (q_t * cq_ref[...][None] + partner * sq_ref[...][None]).astype(BF16)
    for g in range(G):
        for h in range(tm // tq):
            q_ref[0, g, h] = jnp.concatenate(
                [q_t[g * R + r][:, h * tq:(h + 1) * tq] for r in range(R)], axis=1)

    kk = jnp.dot(xb, wk_ref[...], preferred_element_type=F32)
    cos_k, sin_k = ck_ref[...], sk_ref[...]
    k_sel = _rope_rows(kk[:, :KV_WIDTH], cos_k, sin_k)
    k_win = _rope_rows(kk[:, KV_WIDTH:2 * KV_WIDTH], cos_k, sin_k)
    kc_ref[...] = _rope_rows(kk[:, 2 * KV_WIDTH:], cos_k, sin_k)
    pos = (pl.program_id(0) % row_tiles_per_seq) * tm + lax.broadcasted_iota(I32, (tm, nsl), 0)
    block_onehot = jnp.where((pos >> (SEL_BLOCK.bit_length() - 1))
                             == lax.broadcasted_iota(I32, (tm, nsl), 1), 1.0, 0.0)
    for g in range(G):
        ks_ref[0, g] = jnp.concatenate([k_sel[:, g * dk:(g + 1) * dk], block_onehot],
                                       axis=1).astype(BF16)
        kw_ref[0, g] = k_win[:, g * dk:(g + 1) * dk].astype(BF16)

    v_t = _nt_dot(wv_ref[...], xb)
    extra = jnp.where(lax.broadcasted_iota(I32, (V_PAD, tm), 0) == 0, 1.0, 0.0)
    for g in range(G):
        vs = jnp.concatenate([v_t[g * dk:(g + 1) * dk], extra], axis=0).astype(BF16)
        for c in range(tm // tk):
            vs_ref[0, g, c] = vs[:, c * tk:(c + 1) * tk]
        vw = jnp.concatenate([v_t[KV_WIDTH + g * dk:KV_WIDTH + (g + 1) * dk], extra],
                             axis=0).astype(BF16)
        tw = vw_ref.shape[4]
        for h in range(tm // tw):
            vw_ref[0, g, h] = vw[:, h * tw:(h + 1) * tw]

    vc_ref[...] = jnp.dot(xb, wc_ref[...], preferred_element_type=F32)

    gl = _nt_dot(wg_ref[...], xb)
    for g in range(G):
        for h in range(tm // tq):
            gl_ref[0, g, h] = jnp.concatenate(
                [jnp.concatenate([gl[(g * 3 + i) * R + r:(g * 3 + i) * R + r + 1, h * tq:(h + 1) * tq]
                                  for r in range(R)], axis=1) for i in range(3)], axis=0)


def _nsa_project(x, bsz, seq, w_in):
    G, R, dk = N_KV_GROUPS, HEADS_PER_GROUP, HEAD_DIM
    t, d = x.shape
    qd = N_HEADS * dk
    tm = min(ROW_TILE, seq)
    tq = min(Q_TILE, seq)
    tk = min(SEL_KEY_TILE, seq)
    tw = min(WIN_Q_TILE, tq)
    nsl = seq // SEL_BLOCK
    ns = seq // tm
    assert tm % tq == 0 and tm % tk == 0 and tq % tw == 0
    cut = [qd + j * KV_WIDTH for j in range(7)]
    w = w_in.astype(BF16)
    w_q, w_kc, w_vc, w_ks, w_vs, w_kw, w_vw, w_gl = (
        w[:, :qd], w[:, cut[0]:cut[1]], w[:, cut[1]:cut[2]], w[:, cut[2]:cut[3]],
        w[:, cut[3]:cut[4]], w[:, cut[4]:cut[5]], w[:, cut[5]:cut[6]], w[:, cut[6]:])
    wq_t = w_q.T
    wk = jnp.concatenate([w_ks, w_kw, w_kc], axis=1)
    wv_t = jnp.concatenate([w_vs, w_vw], axis=1).T
    wc = w_vc
    wg_t = w_gl.reshape(d, G, R, 3).transpose(1, 3, 2, 0).reshape(G * 3 * R, d)
    wg_t = jnp.concatenate([wg_t, jnp.zeros((LANES - G * 3 * R, d), BF16)], axis=0)
    pos = jnp.arange(seq, dtype=F32)
    inv = ROPE_THETA ** (-jnp.arange(0, dk, 2, dtype=F32) / dk)
    ang = pos[:, None] * inv[None, :]
    cos, sin = jnp.cos(ang), jnp.sin(ang)
    cos_h = jnp.concatenate([cos, cos], axis=1)
    sin_h = jnp.concatenate([-sin, sin], axis=1)
    cq, sq = (cos_h * QK_SCALE).T, (sin_h * QK_SCALE).T
    ck, sk = jnp.tile(cos_h, (1, G)), jnp.tile(sin_h, (1, G))
    full = lambda a: pl.BlockSpec(a.shape, lambda i: (0,) * a.ndim)
    b_of = lambda i: i // ns
    s_of = lambda i: i % ns
    out_shape = [jax.ShapeDtypeStruct((bsz, G, seq // tq, dk, R * tq), BF16),
                 jax.ShapeDtypeStruct((bsz, G, seq, dk + nsl), BF16),
                 jax.ShapeDtypeStruct((bsz, G, seq, dk), BF16),
                 jax.ShapeDtypeStruct((t, KV_WIDTH), F32),
                 jax.ShapeDtypeStruct((t, KV_WIDTH), F32),
                 jax.ShapeDtypeStruct((bsz, G, seq // tk, dk + V_PAD, tk), BF16),
                 jax.ShapeDtypeStruct((bsz, G, seq // tw, dk + V_PAD, tw), BF16),
                 jax.ShapeDtypeStruct((bsz, G, seq // tq, 3, R * tq), F32)]
    out_specs = [pl.BlockSpec((1, G, tm // tq, dk, R * tq), lambda i: (b_of(i), 0, s_of(i), 0, 0)),
                 pl.BlockSpec((1, G, tm, dk + nsl), lambda i: (b_of(i), 0, s_of(i), 0)),
                 pl.BlockSpec((1, G, tm, dk), lambda i: (b_of(i), 0, s_of(i), 0)),
                 pl.BlockSpec((tm, KV_WIDTH), lambda i: (i, 0)),
                 pl.BlockSpec((tm, KV_WIDTH), lambda i: (i, 0)),
                 pl.BlockSpec((1, G, tm // tk, dk + V_PAD, tk), lambda i: (b_of(i), 0, s_of(i), 0, 0)),
                 pl.BlockSpec((1, G, tm // tw, dk + V_PAD, tw), lambda i: (b_of(i), 0, s_of(i), 0, 0)),
                 pl.BlockSpec((1, G, tm // tq, 3, R * tq), lambda i: (b_of(i), 0, s_of(i), 0, 0))]
    return pl.pallas_call(
        functools.partial(_nsa_proj_kernel, row_tiles_per_seq=ns),
        grid=(t // tm,),
        in_specs=[pl.BlockSpec((tm, d), lambda i: (i, 0)),
                  full(wq_t), full(wk), full(wv_t), full(wc), full(wg_t),
                  pl.BlockSpec((dk, tm), lambda i: (0, s_of(i))),
                  pl.BlockSpec((dk, tm), lambda i: (0, s_of(i))),
                  pl.BlockSpec((tm, KV_WIDTH), lambda i: (s_of(i), 0)),
                  pl.BlockSpec((tm, KV_WIDTH), lambda i: (s_of(i), 0))],
        out_specs=out_specs, out_shape=out_shape,
        compiler_params=_params("arbitrary"),
    )(x, wq_t, wk, wv_t, wc, wg_t, cq, sq, ck, sk)


def _nsa_mixer_heads(x, bsz, seq, w_in, cmp_k_pe, cmp_k_w1, cmp_k_w2, cmp_v_pe, cmp_v_w1, cmp_v_w2):
    q_t, ks_aug, kw, kc, vc, vs_t, vw_t, gl_t = _nsa_project(x, bsz, seq, w_in)
    kcmp, _ = _compress(kc, bsz, seq, cmp_k_pe, cmp_k_w1, cmp_k_w2)
    _, vcmp_t = _compress(vc, bsz, seq, cmp_v_pe, cmp_v_w1, cmp_v_w2)
    o = _nsa_attention(q_t, kcmp, vcmp_t, ks_aug, vs_t, kw, vw_t, gl_t)
    return o.reshape(bsz * seq, N_HEADS * HEAD_DIM)


def _even_mixer_cat(x, bsz, seq, w_in, conv_w, lru_conv_w, lru_conv_b, w_a, b_a, w_x, b_x, lam):
    wa_bd = jax.scipy.linalg.block_diag(*w_a).astype(BF16)
    wx_bd = jax.scipy.linalg.block_diag(*w_x).astype(BF16)
    return _l0_mixer(x, w_in.astype(BF16), bsz, seq, conv_w, lru_conv_w, lru_conv_b, wa_bd, b_a, wx_bd, b_x, lam)


def kernel(x, l0_w_in, l0_conv_w, l0_lru_conv_w, l0_lru_conv_b, l0_lru_w_a, l0_lru_b_a, l0_lru_w_x, l0_lru_b_x, l0_lru_lambda, l0_w_out, l0_ln1_g, l0_ln1_b, l0_router_group_w, l0_router_group_b, l0_router_expert_w, l0_router_expert_b, l0_expert_w_gate, l0_expert_w_up, l0_expert_w_down, l0_ln2_g, l0_ln2_b, l1_w_in, l1_cmp_k_pe, l1_cmp_k_w1, l1_cmp_k_w2, l1_cmp_v_pe, l1_cmp_v_w1, l1_cmp_v_w2, l1_w_out, l1_ln1_g, l1_ln1_b, l1_router_group_w, l1_router_group_b, l1_router_expert_w, l1_router_expert_b, l1_expert_w_gate, l1_expert_w_up, l1_expert_w_down, l1_ln2_g, l1_ln2_b):
    bsz, seq, d = x.shape
    h = x.reshape(bsz * seq, d)
    y = _even_mixer_cat(h, bsz, seq, l0_w_in, l0_conv_w, l0_lru_conv_w, l0_lru_conv_b, l0_lru_w_a,
                        l0_lru_b_a, l0_lru_w_x, l0_lru_b_x, l0_lru_lambda)
    h, *routing = _matmul_residual_ln_route(
        y, l0_w_out.astype(BF16), h, l0_ln1_g, l0_ln1_b,
        (l0_router_group_w, l0_router_group_b, l0_router_expert_w, l0_router_expert_b))
    h = _hier_moe_ln(h, *routing, l0_expert_w_gate, l0_expert_w_up, l0_expert_w_down, l0_ln2_g, l0_ln2_b)
    y = _nsa_mixer_heads(h, bsz, seq, l1_w_in, l1_cmp_k_pe, l1_cmp_k_w1, l1_cmp_k_w2, l1_cmp_v_pe,
                         l1_cmp_v_w1, l1_cmp_v_w2)
    h, *routing = _matmul_residual_ln_route(
        y, l1_w_out.astype(BF16), h, l1_ln1_g, l1_ln1_b,
        (l1_router_group_w, l1_router_group_b, l1_router_expert_w, l1_router_expert_b))
    h = _hier_moe_ln(h, *routing, l1_expert_w_gate, l1_expert_w_up, l1_expert_w_down, l1_ln2_g, l1_ln2_b)
    return h.reshape(bsz, seq, d)
```
